```python
import math
import jax
import jax.numpy as jnp
from jax import lax
import numpy as np

D_MODEL = 1024
BATCH = 8
SEQ = 4096
DEPTH = 2

MEM_LEN = 256
N_EVEN = (DEPTH + 1) // 2
N_ODD = DEPTH // 2
NORM_EPS = 1e-6
CONV_W = 4

W_A = D_MODEL
A_HEADS = 8
A_HEAD_DIM = W_A // A_HEADS
LRU_C = 8.0
W_B = D_MODEL
B_HEADS = 8
B_HEAD_DIM = W_B // B_HEADS
B_CHUNK = 32
E_SPLITS = [W_A, 2 * W_A, 2 * W_A + W_B, 2 * W_A + 2 * W_B, 2 * W_A + 3 * W_B]
E_IN = 2 * W_A + 4 * W_B

D_INNER = 2 * D_MODEL
M_HEAD_DIM = 64
M_HEADS = D_INNER // M_HEAD_DIM
M_GROUPS = 8
M_HPG = M_HEADS // M_GROUPS
D_STATE = 128
M_CHUNK = 128
CONV_DIM = D_INNER + 2 * M_GROUPS * D_STATE
O_SPLITS = [D_INNER, D_INNER + CONV_DIM]
O_IN = D_INNER + CONV_DIM + M_HEADS

X_HEADS = 4
X_HEAD_DIM = D_MODEL // X_HEADS
D_FF = 4 * D_MODEL

kernel_name = 'hybrid_rglru_hgrn2_ssd_memxattn'


def rmsnorm(x, g):
    xf = x.astype(jnp.float32)
    y = xf * lax.rsqrt(jnp.mean(xf * xf, axis=-1, keepdims=True) + NORM_EPS)
    return (y * g.astype(jnp.float32)).astype(x.dtype)


def group_rmsnorm(xf, g, n_groups):
    shp = xf.shape
    xg = xf.reshape(shp[:-1] + (n_groups, shp[-1] // n_groups))
    xg = xg * lax.rsqrt(jnp.mean(xg * xg, axis=-1, keepdims=True) + NORM_EPS)
    return xg.reshape(shp) * g.astype(jnp.float32)


def causal_dwconv(x, w, b):
    k, c = w.shape
    y = lax.conv_general_dilated(x, w[:, None, :].astype(x.dtype), window_strides=(1,),
                                 padding=[(k - 1, 0)], dimension_numbers=('NWC', 'WIO', 'NWC'),
                                 feature_group_count=c)
    return y + b.astype(x.dtype)


def _lin_combine(c1, c2):
    a1, b1 = c1
    a2, b2 = c2
    return a1 * a2, a2 * b1 + b2


def rg_lru(x, w_r, b_r, w_i, b_i, lam):
    bsz, s, _ = x.shape
    xf = x.astype(jnp.float32)
    xh = xf.reshape(bsz, s, A_HEADS, A_HEAD_DIM)
    r = jax.nn.sigmoid(jnp.einsum('bshi,hij->bshj', xh, w_r.astype(jnp.float32)).reshape(bsz, s, W_A)
                       + b_r.astype(jnp.float32))
    i = jax.nn.sigmoid(jnp.einsum('bshi,hij->bshj', xh, w_i.astype(jnp.float32)).reshape(bsz, s, W_A)
                       + b_i.astype(jnp.float32))
    log_a = -LRU_C * r * jax.nn.softplus(-lam.astype(jnp.float32))
    a = jnp.exp(log_a)
    u = jnp.sqrt(-jnp.expm1(2.0 * log_a)) * (i * xf)
    _, h = lax.associative_scan(_lin_combine, (a, u), axis=1)
    return h


def hgrn2_mix(q, f_logit, v, g, lb, norm_g):
    bsz, s, _ = q.shape
    nc = s // B_CHUNK
    qf = jax.nn.silu(q.astype(jnp.float32))
    f = lb + (1.0 - lb) * jax.nn.sigmoid(f_logit.astype(jnp.float32))
    k = 1.0 - f
    logf = jnp.log(f)

    def to_chunks(t):
        return t.reshape(bsz, nc, B_CHUNK, B_HEADS, B_HEAD_DIM).transpose(1, 0, 3, 2, 4)

    mask = jnp.tril(jnp.ones((B_CHUNK, B_CHUNK), dtype=bool))[None, None, :, :, None]

    def step(state, inp):
        qc, kc, vc, gc = inp
        bc = jnp.cumsum(gc, axis=2)
        o_inter = jnp.einsum('bhtk,bhkv->bhtv', qc * jnp.exp(bc), state)
        diff = bc[:, :, :, None, :] - bc[:, :, None, :, :]
        decay = jnp.where(mask, jnp.exp(jnp.where(mask, diff, 0.0)), 0.0)
        scores = jnp.einsum('bhtk,bhtsk,bhsk->bhts', qc, decay, kc)
        o_intra = jnp.einsum('bhts,bhsv->bhtv', scores, vc)
        b_last = bc[:, :, -1]
        k_dec = kc * jnp.exp(b_last[:, :, None, :] - bc)
        new_state = jnp.exp(b_last)[..., None] * state + jnp.einsum('bhsk,bhsv->bhkv', k_dec, vc)
        return new_state, o_inter + o_intra

    state0 = jnp.zeros((bsz, B_HEADS, B_HEAD_DIM, B_HEAD_DIM), jnp.float32)
    _, o = lax.scan(step, state0, (to_chunks(qf), to_chunks(k), to_chunks(v.astype(jnp.float32)),
                                   to_chunks(logf)))
    o = o.transpose(1, 0, 3, 2, 4).reshape(bsz, s, B_HEADS, B_HEAD_DIM)
    o = o * lax.rsqrt(jnp.mean(o * o, axis=-1, keepdims=True) + NORM_EPS)
    o = o.reshape(bsz, s, W_B) * norm_g.astype(jnp.float32) * jax.nn.silu(g.astype(jnp.float32))
    return o


def ssd_chunked(xdt, da, bm, cm):
    bsz, s = xdt.shape[:2]
    nc = s // M_CHUNK
    xc_all = jnp.moveaxis(xdt.reshape(bsz, nc, M_CHUNK, M_GROUPS, M_HPG, M_HEAD_DIM), 1, 0)
    ac_all = jnp.moveaxis(da.reshape(bsz, nc, M_CHUNK, M_GROUPS, M_HPG), 1, 0)
    bc_all = jnp.moveaxis(bm.reshape(bsz, nc, M_CHUNK, M_GROUPS, D_STATE), 1, 0)
    cc_all = jnp.moveaxis(cm.reshape(bsz, nc, M_CHUNK, M_GROUPS, D_STATE), 1, 0)
    mask = jnp.tril(jnp.ones((M_CHUNK, M_CHUNK), dtype=bool))[None, :, :, None, None]

    def step(state, inp):
        xc, ac, bc, cc = inp
        acum = jnp.cumsum(ac, axis=1)
        seg = acum[:, :, None] - acum[:, None, :]
        lmat = jnp.where(mask, jnp.exp(jnp.where(mask, seg, 0.0)), 0.0)
        cb = jnp.einsum('btgn,bsgn->btsg', cc, bc)
        y_intra = jnp.einsum('btsg,btsgr,bsgrp->btgrp', cb, lmat, xc)
        y_inter = jnp.einsum('btgn,btgr,bgrpn->btgrp', cc, jnp.exp(acum), state)
        a_last = acum[:, -1]
        w_s = jnp.exp(a_last[:, None] - acum)
        new_state = (jnp.exp(a_last)[..., None, None] * state
                     + jnp.einsum('bsgn,bsgr,bsgrp->bgrpn', bc, w_s, xc))
        return new_state, y_intra + y_inter

    state0 = jnp.zeros((bsz, M_GROUPS, M_HPG, M_HEAD_DIM, D_STATE), jnp.float32)
    _, y = lax.scan(step, state0, (xc_all, ac_all, bc_all, cc_all))
    return jnp.moveaxis(y, 0, 1).reshape(bsz, s, M_HEADS, M_HEAD_DIM)


def mamba2_mix(z, xm, bm, cm, dt_raw, dt_bias, a_log, d_skip, norm_g):
    bsz, s, _ = xm.shape
    dt = jax.nn.softplus(dt_raw.astype(jnp.float32) + dt_bias.astype(jnp.float32))
    a = -jnp.exp(a_log.astype(jnp.float32))
    xh = xm.astype(jnp.float32).reshape(bsz, s, M_HEADS, M_HEAD_DIM)
    y = ssd_chunked(xh * dt[..., None], dt * a,
                    bm.astype(jnp.float32).reshape(bsz, s, M_GROUPS, D_STATE),
                    cm.astype(jnp.float32).reshape(bsz, s, M_GROUPS, D_STATE))
    y = y + d_skip.astype(jnp.float32)[:, None] * xh
    y = y.reshape(bsz, s, D_INNER) * jax.nn.silu(z.astype(jnp.float32))
    return group_rmsnorm(y, norm_g, M_GROUPS)


def mem_cross_attn(h, mem_n, wq, wk, wv, wo):
    bsz, s, _ = h.shape
    m = mem_n.shape[1]
    q = (h @ wq).reshape(bsz, s, X_HEADS, X_HEAD_DIM)
    k = (mem_n @ wk).reshape(bsz, m, X_HEADS, X_HEAD_DIM)
    v = (mem_n @ wv).reshape(bsz, m, X_HEADS, X_HEAD_DIM)
    sc = jnp.einsum('bshd,bmhd->bhsm', q, k).astype(jnp.float32) * (X_HEAD_DIM ** -0.5)
    p = jax.nn.softmax(sc, axis=-1).astype(v.dtype)
    o = jnp.einsum('bhsm,bmhd->bshd', p, v).reshape(bsz, s, D_MODEL)
    return o @ wo


def setup_inputs(seed: int = 0) -> dict:
    key = jax.random.key(seed)
    ks = iter(jax.random.split(key, 48))

    def nrm(shape, scale):
        return jax.random.normal(next(ks), shape, jnp.float32) * scale

    def gain(shape):
        return 1.0 + nrm(shape, 0.01)

    d = D_MODEL
    inp = {}
    inp['x'] = nrm((BATCH, SEQ, d), 1.0)
    inp['mem'] = nrm((BATCH, MEM_LEN, d), 1.0)
    inp['norm_mix_g'] = gain((DEPTH, d))
    inp['norm_mem_q_g'] = gain((DEPTH, d))
    inp['norm_mem_kv_g'] = gain((DEPTH, d))
    inp['norm_ffn_g'] = gain((DEPTH, d))
    inp['final_norm_g'] = gain((d,))
    inp['e_w_in'] = nrm((N_EVEN, d, E_IN), d ** -0.5)
    inp['a_conv_w'] = nrm((N_EVEN, CONV_W, W_A), CONV_W ** -0.5)
    inp['a_conv_b'] = nrm((N_EVEN, W_A), 0.01)
    inp['a_gate_r_w'] = nrm((N_EVEN, A_HEADS, A_HEAD_DIM, A_HEAD_DIM), A_HEAD_DIM ** -0.5)
    inp['a_gate_r_b'] = nrm((N_EVEN, W_A), 0.01)
    inp['a_gate_i_w'] = nrm((N_EVEN, A_HEADS, A_HEAD_DIM, A_HEAD_DIM), A_HEAD_DIM ** -0.5)
    inp['a_gate_i_b'] = nrm((N_EVEN, W_A), 0.01)
    a8 = jax.random.uniform(next(ks), (N_EVEN, W_A), jnp.float32, 0.9, 0.999)
    a_base = a8 ** (1.0 / LRU_C)
    inp['a_lambda'] = jnp.log(a_base) - jnp.log1p(-a_base)
    inp['b_lb_logits'] = nrm((DEPTH + 1, W_B), 0.1)
    inp['b_norm_g'] = gain((N_EVEN, W_B))
    inp['e_w_out'] = nrm((N_EVEN, W_A + W_B, d), (W_A + W_B) ** -0.5)
    inp['o_w_in'] = nrm((N_ODD, d, O_IN), d ** -0.5)
    inp['m_conv_w'] = nrm((N_ODD, CONV_W, CONV_DIM), CONV_W ** -0.5)
    inp['m_conv_b'] = nrm((N_ODD, CONV_DIM), 0.01)
    u = jax.random.uniform(next(ks), (N_ODD, M_HEADS), jnp.float32)
    dt0 = jnp.exp(u * (math.log(0.1) - math.log(0.001)) + math.log(0.001))
    inp['m_dt_bias'] = dt0 + jnp.log(-jnp.expm1(-dt0))
    inp['m_a_log'] = jnp.log(jax.random.uniform(next(ks), (N_ODD, M_HEADS), jnp.float32, 1.0, 16.0))
    inp['m_d'] = 1.0 + nrm((N_ODD, M_HEADS), 0.1)
    inp['m_norm_g'] = gain((N_ODD, D_INNER))
    inp['o_w_out'] = nrm((N_ODD, D_INNER, d), D_INNER ** -0.5)
    inp['xq_w'] = nrm((DEPTH, d, d), d ** -0.5)
    inp['xk_w'] = nrm((DEPTH, d, d), d ** -0.5)
    inp['xv_w'] = nrm((DEPTH, d, d), d ** -0.5)
    inp['xo_w'] = nrm((DEPTH, d, d), d ** -0.5)
    inp['ffn_w1'] = nrm((DEPTH, d, D_FF), d ** -0.5)
    inp['ffn_w2'] = nrm((DEPTH, D_FF, d), D_FF ** -0.5)
    return inp


def reference(x, mem, norm_mix_g, norm_mem_q_g, norm_mem_kv_g, norm_ffn_g, final_norm_g,
              e_w_in, a_conv_w, a_conv_b, a_gate_r_w, a_gate_r_b, a_gate_i_w, a_gate_i_b,
              a_lambda, b_lb_logits, b_norm_g, e_w_out,
              o_w_in, m_conv_w, m_conv_b, m_dt_bias, m_a_log, m_d, m_norm_g, o_w_out,
              xq_w, xk_w, xv_w, xo_w, ffn_w1, ffn_w2):
    lb_table = jnp.cumsum(jax.nn.softmax(b_lb_logits.astype(jnp.float32), axis=0), axis=0)
    for l in range(DEPTH):
        h = rmsnorm(x, norm_mix_g[l])
        if l % 2 == 0:
            e = l // 2
            proj = h @ e_w_in[e]
            xa, ga, qb, fb, ib, gb = jnp.split(proj, E_SPLITS, axis=-1)
            xa = causal_dwconv(xa, a_conv_w[e], a_conv_b[e])
            ya = rg_lru(xa, a_gate_r_w[e], a_gate_r_b[e], a_gate_i_w[e], a_gate_i_b[e], a_lambda[e])
            ya = ya * jax.nn.gelu(ga.astype(jnp.float32))
            yb = hgrn2_mix(qb, fb, ib, gb, lb_table[l], b_norm_g[e])
            mix = jnp.concatenate([ya, yb], axis=-1).astype(x.dtype) @ e_w_out[e]
        else:
            o = l // 2
            proj = h @ o_w_in[o]
            z, xbc, dt_raw = jnp.split(proj, O_SPLITS, axis=-1)
            xbc = jax.nn.silu(causal_dwconv(xbc, m_conv_w[o], m_conv_b[o]))
            xm, bm, cm = jnp.split(xbc, [D_INNER, D_INNER + M_GROUPS * D_STATE], axis=-1)
            ym = mamba2_mix(z, xm, bm, cm, dt_raw, m_dt_bias[o], m_a_log[o], m_d[o], m_norm_g[o])
            mix = ym.astype(x.dtype) @ o_w_out[o]
        x = x + mix
        x = x + mem_cross_attn(rmsnorm(x, norm_mem_q_g[l]), rmsnorm(mem, norm_mem_kv_g[l]),
                               xq_w[l], xk_w[l], xv_w[l], xo_w[l])
        hf = rmsnorm(x, norm_ffn_g[l])
        x = x + jnp.square(jax.nn.relu(hf @ ffn_w1[l])) @ ffn_w2[l]
    return rmsnorm(x, final_norm_g)
```

```python
import functools

import jax
import jax.numpy as jnp
import numpy as np
from jax import lax
from jax.experimental import pallas as pl
from jax.experimental.pallas import tpu as pltpu

F32 = jnp.float32
BF16 = jnp.bfloat16

NORM_EPS = 1e-6
CONV_W = 4
LANES = 128
SUBLANES = 8
VMEM_LIMIT_BYTES = 56 * 1024 * 1024

A_HEADS = 8
LRU_C = 8.0
B_HEADS = 8
M_HEAD_DIM = 64
M_GROUPS = 8
M_HPG = 4
D_STATE = 128
X_HEADS = 4
CHUNK = 128

TOKEN_TILE = 512
LRU_TILE = 256
MIX_TILE = 512

NT_DIMS = (((1,), (1,)), ((), ()))
TN_DIMS = (((0,), (0,)), ((), ()))


def _params(n_axes):
    return pltpu.CompilerParams(
        dimension_semantics=("arbitrary",) * n_axes, vmem_limit_bytes=VMEM_LIMIT_BYTES)


def _resident(shape):
    nd = len(shape)
    return pl.BlockSpec(shape, lambda *_: (0,) * nd, pipeline_mode=pl.Buffered(1))


def _rms(x, g):
    return x * lax.rsqrt(jnp.mean(x * x, axis=-1, keepdims=True) + NORM_EPS) * g


def _sigmoid(x):
    return jax.nn.sigmoid(x)


def _silu(x):
    return x * jax.nn.sigmoid(x)


def _softplus(x):
    return jnp.maximum(x, 0.0) + jnp.log1p(jnp.exp(-jnp.abs(x)))


def _dot(a, b):
    return jnp.dot(a, b, preferred_element_type=F32)


def _dot_f32(a, b):
    return jnp.dot(a, b, preferred_element_type=F32, precision=lax.Precision.HIGHEST)


def _split3(x):
    hi = x.astype(BF16)
    r = x - hi.astype(F32)
    mid = r.astype(BF16)
    lo = (r - mid.astype(F32)).astype(BF16)
    return hi, mid, lo


def _inproj_kernel(x_ref, g_ref, *refs, kinds, col_chunk):
    n = len(kinds)
    w_refs, o_refs = refs[:n], refs[n:]
    xn = _rms(x_ref[...], g_ref[...]).astype(BF16)
    for kind, w_ref, o_ref in zip(kinds, w_refs, o_refs):
        if kind == "nn":
            for c in range(0, w_ref.shape[1], col_chunk):
                o_ref[:, c:c + col_chunk] = _dot(xn, w_ref[:, c:c + col_chunk]).astype(o_ref.dtype)
        else:
            o_ref[...] = lax.dot_general(w_ref[...], xn, NT_DIMS,
                                         preferred_element_type=F32).astype(o_ref.dtype)


def _inproj(x2d, g, weights, kinds, out_dtypes, tm=TOKEN_TILE, col_chunk=1024):
    t, d = x2d.shape
    in_specs = [pl.BlockSpec((tm, d), lambda i: (i, 0)), _resident((1, d))]
    out_specs, out_shapes = [], []
    for w, kind, dt in zip(weights, kinds, out_dtypes):
        in_specs.append(_resident(w.shape))
        if kind == "nn":
            out_specs.append(pl.BlockSpec((tm, w.shape[1]), lambda i: (i, 0)))
            out_shapes.append(jax.ShapeDtypeStruct((t, w.shape[1]), dt))
        else:
            out_specs.append(pl.BlockSpec((w.shape[0], tm), lambda i: (0, i)))
            out_shapes.append(jax.ShapeDtypeStruct((w.shape[0], t), dt))
    return pl.pallas_call(
        functools.partial(_inproj_kernel, kinds=tuple(kinds), col_chunk=col_chunk),
        grid=(t // tm,), in_specs=in_specs, out_specs=out_specs, out_shape=out_shapes,
        compiler_params=_params(1), name="inproj",
    )(x2d, g.reshape(1, d), *weights)


def _causal_conv(x, buf, w_ref, b_ref):
    ts = x.shape[0]
    buf[SUBLANES:SUBLANES + ts, :] = x
    y = b_ref[...] + w_ref[CONV_W - 1:CONV_W, :] * x
    for k in range(CONV_W - 1):
        off = SUBLANES - (CONV_W - 1) + k
        y = y + w_ref[k:k + 1, :] * buf[off:off + ts, :]
    buf[0:SUBLANES, :] = x[ts - SUBLANES:ts, :]
    return y


def _rglru_kernel(xa_ref, ga_ref, cw_ref, cb_ref, wg_ref, br_ref, bi_ref, lam_ref, o_ref,
                  xbuf, a_s, u_s, h_s, hc):
    ts, width = xa_ref.shape
    hd = width // A_HEADS

    @pl.when(pl.program_id(1) == 0)
    def _():
        xbuf[0:SUBLANES, :] = jnp.zeros((SUBLANES, width), F32)
        hc[...] = jnp.zeros_like(hc)

    conv = _causal_conv(xa_ref[...].astype(F32), xbuf, cw_ref, cb_ref)
    convb = conv.astype(BF16)
    r_pre, i_pre = [], []
    for h in range(A_HEADS):
        gate = _dot(convb[:, h * hd:(h + 1) * hd], wg_ref[h])
        r_pre.append(gate[:, :hd])
        i_pre.append(gate[:, hd:])
    r = _sigmoid(jnp.concatenate(r_pre, axis=-1) + br_ref[...])
    i = _sigmoid(jnp.concatenate(i_pre, axis=-1) + bi_ref[...])
    log_a = (-LRU_C) * r * _softplus(-lam_ref[...])
    a = jnp.exp(log_a)
    u = jnp.sqrt(1.0 - a * a) * (i * conv)

    row = lax.broadcasted_iota(jnp.int32, (ts, width), 0) & (SUBLANES - 1)
    for sh in (1, 2, 4):
        keep = row >= sh
        u = jnp.where(keep, a * pltpu.roll(u, sh, 0) + u, u)
        a = jnp.where(keep, a * pltpu.roll(a, sh, 0), a)
    a_s[...] = a
    u_s[...] = u

    def block(j, h):
        base = pl.multiple_of(j * SUBLANES, SUBLANES)
        hb = u_s[pl.ds(base, SUBLANES), :] + a_s[pl.ds(base, SUBLANES), :] * h
        h_s[pl.ds(base, SUBLANES), :] = hb
        return hb[SUBLANES - 1:SUBLANES, :]

    hc[0:1, :] = lax.fori_loop(0, ts // SUBLANES, block, hc[0:1, :])
    ga = ga_ref[...].astype(F32)
    o_ref[...] = (h_s[...] * jax.nn.gelu(ga)).astype(o_ref.dtype)


def _rglru(proj, conv_w, conv_b, wg, b_r, b_i, lam, bsz, seq, ts=LRU_TILE):
    width = conv_w.shape[1]
    nt = seq // ts
    row = lambda b, t: b * nt + t
    vec = lambda: _resident((1, width))
    return pl.pallas_call(
        _rglru_kernel,
        grid=(bsz, nt),
        in_specs=[pl.BlockSpec((ts, width), lambda b, t: (row(b, t), 0)),
                  pl.BlockSpec((ts, width), lambda b, t: (row(b, t), 1)),
                  _resident(conv_w.shape), vec(), _resident(wg.shape), vec(), vec(), vec()],
        out_specs=pl.BlockSpec((ts, width), lambda b, t: (row(b, t), 0)),
        out_shape=jax.ShapeDtypeStruct((bsz * seq, width), BF16),
        scratch_shapes=[pltpu.VMEM((ts + SUBLANES, width), F32), pltpu.VMEM((ts, width), F32),
                        pltpu.VMEM((ts, width), F32), pltpu.VMEM((ts, width), F32),
                        pltpu.VMEM((SUBLANES, width), F32)],
        compiler_params=_params(2), name="rglru",
    )(proj, proj, conv_w, conv_b.reshape(1, width), wg, b_r.reshape(1, width),
      b_i.reshape(1, width), lam.reshape(1, width))


def _pair_level(n):
    t = lax.broadcasted_iota(jnp.int32, (n, n), 0)
    s = lax.broadcasted_iota(jnp.int32, (n, n), 1)
    x = t ^ s
    lvl = jnp.zeros((n, n), jnp.int32)
    w = 1
    while w < n:
        lvl = lvl + (x >= w).astype(jnp.int32)
        w *= 2
    return jnp.where(s > t, -1, lvl)


def _hgrn2_kernel(q_ref, f_ref, v_ref, g_ref, lbl_ref, ng_ref, o_ref, st, bc_s, *, layer):
    ts, dk = q_ref.shape
    c = CHUNK

    @pl.when(pl.program_id(2) == 0)
    def _():
        st[...] = jnp.zeros_like(st)

    lg = lbl_ref[...]
    e = jnp.exp(lg - jnp.max(lg, axis=0, keepdims=True))
    lb = jnp.sum(e[0:layer + 1, :], axis=0, keepdims=True) / jnp.sum(e, axis=0, keepdims=True)

    row = lax.broadcasted_iota(jnp.int32, (c, dk), 0)
    tt = lax.broadcasted_iota(jnp.int32, (c, c), 0)
    ss = lax.broadcasted_iota(jnp.int32, (c, c), 1)
    tri = (ss <= tt).astype(F32)
    lvl = _pair_level(c)
    n_lvl = c.bit_length()

    def chunk(ci, carry):
        r0 = pl.multiple_of(ci * c, c)
        qv = q_ref[pl.ds(r0, c), :].astype(F32)
        qf = _silu(qv)
        f = lb + (1.0 - lb) * _sigmoid(f_ref[pl.ds(r0, c), :])
        kk = 1.0 - f
        bc = _dot_f32(tri, jnp.log(f))
        bc_s[...] = bc
        vb = v_ref[pl.ds(r0, c), :]

        scores = jnp.zeros((c, c), F32)
        for p in range(n_lvl):
            if p == 0:
                ql, kl = qf, kk
            else:
                w = 1 << (p - 1)
                if w == 1:
                    gat = jnp.where((row & 1) == 1, pltpu.roll(bc, 1, 0), bc)
                elif w == 2:
                    r4 = row & 3
                    gat = jnp.where(r4 == 0, pltpu.roll(bc, c - 1, 0),
                                    jnp.where(r4 == 1, bc,
                                              jnp.where(r4 == 2, pltpu.roll(bc, 1, 0),
                                                        pltpu.roll(bc, 2, 0))))
                else:
                    gat = jnp.concatenate(
                        [jnp.broadcast_to(bc_s[b * 2 * w + w - 1:b * 2 * w + w, :], (2 * w, dk))
                         for b in range(c // (2 * w))], axis=0)
                dec = jnp.exp(-jnp.abs(bc - gat))
                ql, kl = qf * dec, kk * dec
            s_l = lax.dot_general(ql.astype(BF16), kl.astype(BF16), NT_DIMS,
                                  preferred_element_type=F32)
            scores = jnp.where(lvl == p, s_l, scores)
        o = _dot(scores.astype(BF16), vb)

        b_last = bc[c - 1:c, :]
        o = o + lax.dot_general((qf * jnp.exp(bc)).astype(BF16), st[...].astype(BF16), NT_DIMS,
                                preferred_element_type=F32)
        kd = (kk * jnp.exp(b_last - bc)).astype(BF16)
        st[...] = st[...] * jnp.exp(b_last) + lax.dot_general(vb, kd, TN_DIMS,
                                                              preferred_element_type=F32)

        o = o * lax.rsqrt(jnp.mean(o * o, axis=-1, keepdims=True) + NORM_EPS)
        gv = g_ref[pl.ds(r0, c), :].astype(F32)
        o_ref[pl.ds(r0, c), :] = (o * ng_ref[...] * _silu(gv)).astype(o_ref.dtype)
        return carry

    lax.fori_loop(0, ts // c, chunk, 0)


def _hgrn2(proj, fproj, lb_logits, norm_g, layer, bsz, seq, col0, ts=MIX_TILE):
    dk = LANES
    width = fproj.shape[1]
    nt = seq // ts
    q0, v0, g0 = col0
    row = lambda b, t: b * nt + t
    n_layers = lb_logits.shape[0]
    return pl.pallas_call(
        functools.partial(_hgrn2_kernel, layer=layer),
        grid=(bsz, B_HEADS, nt),
        in_specs=[pl.BlockSpec((ts, dk), lambda b, h, t: (row(b, t), q0 + h)),
                  pl.BlockSpec((ts, dk), lambda b, h, t: (row(b, t), h)),
                  pl.BlockSpec((ts, dk), lambda b, h, t: (row(b, t), v0 + h)),
                  pl.BlockSpec((ts, dk), lambda b, h, t: (row(b, t), g0 + h)),
                  pl.BlockSpec((n_layers, dk), lambda b, h, t: (0, h)),
                  pl.BlockSpec((1, dk), lambda b, h, t: (0, h))],
        out_specs=pl.BlockSpec((ts, dk), lambda b, h, t: (row(b, t), h)),
        out_shape=jax.ShapeDtypeStruct((bsz * seq, width), BF16),
        scratch_shapes=[pltpu.VMEM((dk, dk), F32), pltpu.VMEM((CHUNK, dk), F32)],
        compiler_params=_params(3), name="hgrn2",
    )(proj, fproj, proj, proj, lb_logits, norm_g.reshape(1, width))


def _ssd_expand_matrix():
    e = np.zeros((LANES, 4 * LANES + 2 * M_HPG * M_HEAD_DIM), np.float32)
    for r in range(M_HPG):
        e[SUBLANES + r, r * LANES:(r + 1) * LANES] = 1.0
        e[r, 4 * LANES + r * M_HEAD_DIM:4 * LANES + (r + 1) * M_HEAD_DIM] = 1.0
        e[SUBLANES + r, 6 * LANES + r * M_HEAD_DIM:6 * LANES + (r + 1) * M_HEAD_DIM] = 1.0
    return e


def _ssd_kernel(z_ref, x_ref, b_ref, c_ref, dt_ref, cwx_ref, cbx_ref, cwb_ref, cbb_ref,
                cwc_ref, cbc_ref, bias_ref, alog_ref, drow_ref, ng_ref, e_ref, o_ref,
                xbuf, bbuf, cbuf, xc_s, bc_s, cc_s, st):
    ts, gw = x_ref.shape
    n = D_STATE
    c = CHUNK

    @pl.when(pl.program_id(2) == 0)
    def _():
        xbuf[0:SUBLANES, :] = jnp.zeros((SUBLANES, gw), F32)
        bbuf[0:SUBLANES, :] = jnp.zeros((SUBLANES, n), F32)
        cbuf[0:SUBLANES, :] = jnp.zeros((SUBLANES, n), F32)
        st[...] = jnp.zeros_like(st)

    xc_s[...] = _silu(_causal_conv(x_ref[...].astype(F32), xbuf, cwx_ref, cbx_ref))
    bc_s[...] = _silu(_causal_conv(b_ref[...].astype(F32), bbuf, cwb_ref, cbb_ref)).astype(BF16)
    cc_s[...] = _silu(_causal_conv(c_ref[...].astype(F32), cbuf, cwc_ref, cbc_ref)).astype(BF16)

    tt = lax.broadcasted_iota(jnp.int32, (c, c), 0)
    ss = lax.broadcasted_iota(jnp.int32, (c, c), 1)
    causal = ss <= tt
    upper = (tt <= ss).astype(F32)
    head_of_lane = lax.broadcasted_iota(jnp.int32, (c, gw), 1) // M_HEAD_DIM
    neg_a = -jnp.exp(alog_ref[0])
    e_mat = e_ref[...]

    def chunk(ci, carry):
        r0 = pl.multiple_of(ci * c, c)
        dt = _softplus(dt_ref[0, ci] + bias_ref[0])
        ac_row = _dot_f32(dt * neg_a, upper)
        pt = jnp.concatenate([dt, ac_row, jnp.zeros((c - 2 * SUBLANES, c), F32)], axis=0).T
        ex = sum(_dot(part, e_mat) for part in _split3(pt))
        dt_ch = ex[:, 4 * LANES:4 * LANES + gw]
        ac_ch = ex[:, 4 * LANES + gw:4 * LANES + 2 * gw]

        xc = xc_s[pl.ds(r0, c), :]
        bm = bc_s[pl.ds(r0, c), :]
        cm = cc_s[pl.ds(r0, c), :]
        xdt = xc * dt_ch
        cb = lax.dot_general(cm, bm, NT_DIMS, preferred_element_type=F32)
        y = _dot(cm, st[...].astype(BF16)) * jnp.exp(ac_ch)
        for r in range(M_HPG):
            seg = ex[:, r * LANES:(r + 1) * LANES] - ac_row[r:r + 1, :]
            lmat = jnp.where(causal, jnp.exp(jnp.where(causal, seg, 0.0)), 0.0)
            xr = jnp.where(head_of_lane == r, xdt, 0.0).astype(BF16)
            y = y + _dot((cb * lmat).astype(BF16), xr)
        y = y + drow_ref[...] * xc

        a_last = ac_ch[c - 1:c, :]
        st[...] = st[...] * jnp.exp(a_last) + lax.dot_general(
            bm, (xdt * jnp.exp(a_last - ac_ch)).astype(BF16), TN_DIMS, preferred_element_type=F32)

        y = y * _silu(z_ref[pl.ds(r0, c), :].astype(F32))
        y = y * lax.rsqrt(jnp.mean(y * y, axis=-1, keepdims=True) + NORM_EPS)
        o_ref[pl.ds(r0, c), :] = (y * ng_ref[...]).astype(o_ref.dtype)
        return carry

    lax.fori_loop(0, ts // c, chunk, 0)


def _ssd(proj, dt_t, conv_w, conv_b, dt_bias, a_log, d_skip, norm_g, bsz, seq, ts=MIX_TILE):
    gw = M_HPG * M_HEAD_DIM
    n = D_STATE
    d_inner = M_GROUPS * gw
    nt = seq // ts
    nck = ts // CHUNK
    row = lambda b, t: b * nt + t
    zb, xb = 0, d_inner // gw
    bb, cb = 2 * d_inner // n, 2 * d_inner // n + M_GROUPS
    cw2 = conv_w.reshape(CONV_W, -1)
    cb2 = conv_b.reshape(1, -1)

    def per_head(v):
        v = jnp.pad(v.reshape(M_GROUPS, M_HPG), ((0, 0), (0, SUBLANES - M_HPG)))
        return jnp.broadcast_to(v[:, :, None], (M_GROUPS, SUBLANES, LANES)).astype(F32)

    drow = jnp.repeat(d_skip.astype(F32), M_HEAD_DIM).reshape(1, d_inner)
    e_mat = jnp.asarray(_ssd_expand_matrix(), BF16)
    g3 = lambda b, g, t: (g, 0, 0)
    return pl.pallas_call(
        _ssd_kernel,
        grid=(bsz, M_GROUPS, nt),
        in_specs=[pl.BlockSpec((ts, gw), lambda b, g, t: (row(b, t), zb + g)),
                  pl.BlockSpec((ts, gw), lambda b, g, t: (row(b, t), xb + g)),
                  pl.BlockSpec((ts, n), lambda b, g, t: (row(b, t), bb + g)),
                  pl.BlockSpec((ts, n), lambda b, g, t: (row(b, t), cb + g)),
                  pl.BlockSpec((1, nck, SUBLANES, LANES), lambda b, g, t: (g, row(b, t), 0, 0)),
                  pl.BlockSpec((CONV_W, gw), lambda b, g, t: (0, g)),
                  pl.BlockSpec((1, gw), lambda b, g, t: (0, g)),
                  pl.BlockSpec((CONV_W, n), lambda b, g, t: (0, bb - xb * 2 + g)),
                  pl.BlockSpec((1, n), lambda b, g, t: (0, bb - xb * 2 + g)),
                  pl.BlockSpec((CONV_W, n), lambda b, g, t: (0, cb - xb * 2 + g)),
                  pl.BlockSpec((1, n), lambda b, g, t: (0, cb - xb * 2 + g)),
                  pl.BlockSpec((1, SUBLANES, LANES), g3),
                  pl.BlockSpec((1, SUBLANES, LANES), g3),
                  pl.BlockSpec((1, gw), lambda b, g, t: (0, g)),
                  pl.BlockSpec((1, gw), lambda b, g, t: (0, g)),
                  _resident(e_mat.shape)],
        out_specs=pl.BlockSpec((ts, gw), lambda b, g, t: (row(b, t), g)),
        out_shape=jax.ShapeDtypeStruct((bsz * seq, d_inner), BF16),
        scratch_shapes=[pltpu.VMEM((ts + SUBLANES, gw), F32), pltpu.VMEM((ts + SUBLANES, n), F32),
                        pltpu.VMEM((ts + SUBLANES, n), F32), pltpu.VMEM((ts, gw), F32),
                        pltpu.VMEM((ts, n), BF16), pltpu.VMEM((ts, n), BF16),
                        pltpu.VMEM((n, gw), F32)],
        compiler_params=_params(3), name="ssd",
    )(proj, proj, proj, proj, dt_t, cw2, cb2, cw2, cb2, cw2, cb2,
      per_head(dt_bias), per_head(a_log), drow, norm_g.reshape(1, d_inner), e_mat)


def _post_kernel(*refs, n_mix, final_norm, ff_chunk):
    mix_refs = refs[:n_mix]
    wmix_refs = refs[n_mix:2 * n_mix]
    (x_ref, gq_ref, wq_ref, k_ref, v_ref, wo_ref, gf_ref, w1_ref, w2_ref, gfin_ref,
     o_ref) = refs[2 * n_mix:]
    d = x_ref.shape[1]
    hd = d // X_HEADS

    x = x_ref[...]
    for m_ref, w_ref in zip(mix_refs, wmix_refs):
        x = x + _dot(m_ref[...], w_ref[...])

    q = _dot(_rms(x, gq_ref[...]).astype(BF16), wq_ref[...]).astype(BF16)
    heads = []
    for h in range(X_HEADS):
        sc = lax.dot_general(q[:, h * hd:(h + 1) * hd], k_ref[0, :, h * hd:(h + 1) * hd], NT_DIMS,
                             preferred_element_type=F32) * (hd ** -0.5)
        p = jnp.exp(sc - jnp.max(sc, axis=-1, keepdims=True))
        p = p / jnp.sum(p, axis=-1, keepdims=True)
        heads.append(_dot(p.astype(BF16), v_ref[0, :, h * hd:(h + 1) * hd]))
    x = x + _dot(jnp.concatenate(heads, axis=-1).astype(BF16), wo_ref[...])

    hn = _rms(x, gf_ref[...]).astype(BF16)
    y = x
    for c in range(0, w1_ref.shape[1], ff_chunk):
        hid = jnp.square(jnp.maximum(_dot(hn, w1_ref[:, c:c + ff_chunk]), 0.0))
        y = y + _dot(hid.astype(BF16), w2_ref[c:c + ff_chunk, :])
    if final_norm:
        y = _rms(y, gfin_ref[...])
    o_ref[...] = y


def _post(mixes, wmixes, x2d, gq, wq, kv, wo, gf, w1, w2, gfin, final_norm, bsz, seq,
          tm=TOKEN_TILE, ff_chunk=1024):
    t, d = x2d.shape
    nt = seq // tm
    mem_len = kv.shape[1]
    tile = lambda w: pl.BlockSpec((tm, w), lambda b, i: (b * nt + i, 0))
    in_specs = [tile(m.shape[1]) for m in mixes] + [_resident(w.shape) for w in wmixes]
    in_specs += [tile(d), _resident((1, d)), _resident(wq.shape),
                 pl.BlockSpec((1, mem_len, d), lambda b, i: (b, 0, 0)),
                 pl.BlockSpec((1, mem_len, d), lambda b, i: (b, 0, 1)),
                 _resident(wo.shape), _resident((1, d)), _resident(w1.shape),
                 _resident(w2.shape), _resident((1, d))]
    return pl.pallas_call(
        functools.partial(_post_kernel, n_mix=len(mixes), final_norm=final_norm,
                          ff_chunk=ff_chunk),
        grid=(bsz, nt), in_specs=in_specs, out_specs=tile(d),
        out_shape=jax.ShapeDtypeStruct((t, d), F32),
        compiler_params=_params(2), name="post",
    )(*mixes, *wmixes, x2d, gq.reshape(1, d), wq, kv, kv, wo, gf.reshape(1, d), w1, w2,
      gfin.reshape(1, d))


def kernel(x, mem, norm_mix_g, norm_mem_q_g, norm_mem_kv_g, norm_ffn_g, final_norm_g, e_w_in, a_conv_w, a_conv_b, a_gate_r_w, a_gate_r_b, a_gate_i_w, a_gate_i_b, a_lambda, b_lb_logits, b_norm_g, e_w_out, o_w_in, m_conv_w, m_conv_b, m_dt_bias, m_a_log, m_d, m_norm_g, o_w_out, xq_w, xk_w, xv_w, xo_w, ffn_w1, ffn_w2):
    bsz, seq, d = x.shape
    mem_len = mem.shape[1]
    depth = norm_mix_g.shape[0]
    t = bsz * seq
    bf = lambda w: w.astype(BF16)
    xs = x.reshape(t, d)
    mem2d = mem.reshape(bsz * mem_len, d)

    for l in range(depth):
        (kv,) = _inproj(mem2d, norm_mem_kv_g[l],
                        [bf(jnp.concatenate([xk_w[l], xv_w[l]], axis=1))], ["nn"], [BF16])
        kv = kv.reshape(bsz, mem_len, 2 * d)
        if l % 2 == 0:
            e = l // 2
            w = e_w_in[e]
            wa = d
            sec = lambda k: w[:, k * wa:(k + 1) * wa]
            w_main = bf(jnp.concatenate([sec(0), sec(1), sec(2), sec(4), sec(5)], axis=1))
            proj, fproj = _inproj(xs, norm_mix_g[l], [w_main, bf(sec(3))], ["nn", "nn"],
                                  [BF16, F32])
            wg = bf(jnp.concatenate([a_gate_r_w[e], a_gate_i_w[e]], axis=-1))
            ya = _rglru(proj, a_conv_w[e], a_conv_b[e], wg, a_gate_r_b[e], a_gate_i_b[e],
                        a_lambda[e], bsz, seq)
            blk = wa // LANES
            yb = _hgrn2(proj, fproj, b_lb_logits, b_norm_g[e], l, bsz, seq,
                        (2 * blk, 3 * blk, 4 * blk))
            mixes = [ya, yb]
            wmixes = [bf(e_w_out[e][:wa]), bf(e_w_out[e][wa:])]
        else:
            o = l // 2
            w = o_w_in[o]
            n_main = w.shape[1] - M_GROUPS * M_HPG
            w_dt = w[:, n_main:].T.reshape(M_GROUPS, M_HPG, d)
            w_dt = jnp.pad(w_dt, ((0, 0), (0, SUBLANES - M_HPG), (0, 0))).reshape(-1, d)
            proj, dt_t = _inproj(xs, norm_mix_g[l], [bf(w[:, :n_main]), bf(w_dt)], ["nn", "nt"],
                                 [BF16, F32])
            dt_t = dt_t.reshape(M_GROUPS, SUBLANES, t // CHUNK, CHUNK).transpose(0, 2, 1, 3)
            ym = _ssd(proj, dt_t, m_conv_w[o], m_conv_b[o], m_dt_bias[o], m_a_log[o], m_d[o],
                      m_norm_g[o], bsz, seq)
            mixes = [ym]
            wmixes = [bf(o_w_out[o])]
        xs = _post(mixes, wmixes, xs, norm_mem_q_g[l], bf(xq_w[l]), kv, bf(xo_w[l]),
                   norm_ffn_g[l], bf(ffn_w1[l]), bf(ffn_w2[l]), final_norm_g,
                   l == depth - 1, bsz, seq)
    return xs.reshape(bsz, seq, d)
```

```python
import functools

import jax
import jax.numpy as jnp
import numpy as np
from jax import lax
from jax.experimental import pallas as pl
from jax.experimental.pallas import tpu as pltpu

F32 = jnp.float32
BF16 = jnp.bfloat16

NORM_EPS = 1e-6
CONV_W = 4
LANES = 128
SUBLANES = 8
VMEM_LIMIT_BYTES = 56 * 1024 * 1024

A_HEADS = 8
LRU_C = 8.0
B_HEADS = 8
M_HEAD_DIM = 64
M_GROUPS = 8
M_HPG = 4
D_STATE = 128
X_HEADS = 4
CHUNK = 128

TOKEN_TILE = 512
LRU_TILE = 256
MIX_TILE = 512

NT_DIMS = (((1,), (1,)), ((), ()))
TN_DIMS = (((0,), (0,)), ((), ()))


def _params(n_axes):
    return pltpu.CompilerParams(
        dimension_semantics=("arbitrary",) * n_axes, vmem_limit_bytes=VMEM_LIMIT_BYTES)


def _resident(shape):
    nd = len(shape)
    return pl.BlockSpec(shape, lambda *_: (0,) * nd, pipeline_mode=pl.Buffered(1))


def _rms(x, g):
    return x * lax.rsqrt(jnp.mean(x * x, axis=-1, keepdims=True) + NORM_EPS) * g


def _sigmoid(x):
    return 0.5 * jnp.tanh(0.5 * x) + 0.5


def _silu(x):
    h = 0.5 * x
    return h * jnp.tanh(h) + h


def _softplus(x):
    return jnp.maximum(x, 0.0) + jnp.log1p(jnp.exp(-jnp.abs(x)))


def _dot(a, b):
    return jnp.dot(a, b, preferred_element_type=F32)


def _dot_f32(a, b):
    return jnp.dot(a, b, preferred_element_type=F32, precision=lax.Precision.HIGHEST)


def _split3(x):
    hi = x.astype(BF16)
    r = x - hi.astype(F32)
    mid = r.astype(BF16)
    lo = (r - mid.astype(F32)).astype(BF16)
    return hi, mid, lo


def _inproj_kernel(x_ref, g_ref, *refs, kinds, col_chunk):
    n = len(kinds)
    w_refs, o_refs = refs[:n], refs[n:]
    xn = _rms(x_ref[...], g_ref[...]).astype(BF16)
    for kind, w_ref, o_ref in zip(kinds, w_refs, o_refs):
        if kind == "nn":
            for c in range(0, w_ref.shape[1], col_chunk):
                o_ref[:, c:c + col_chunk] = _dot(xn, w_ref[:, c:c + col_chunk]).astype(o_ref.dtype)
        else:
            o_ref[...] = lax.dot_general(w_ref[...], xn, NT_DIMS,
                                         preferred_element_type=F32).astype(o_ref.dtype)


def _inproj(x2d, g, weights, kinds, out_dtypes, tm=TOKEN_TILE, col_chunk=1024):
    t, d = x2d.shape
    in_specs = [pl.BlockSpec((tm, d), lambda i: (i, 0)), _resident((1, d))]
    out_specs, out_shapes = [], []
    for w, kind, dt in zip(weights, kinds, out_dtypes):
        in_specs.append(_resident(w.shape))
        if kind == "nn":
            out_specs.append(pl.BlockSpec((tm, w.shape[1]), lambda i: (i, 0)))
            out_shapes.append(jax.ShapeDtypeStruct((t, w.shape[1]), dt))
        else:
            out_specs.append(pl.BlockSpec((w.shape[0], tm), lambda i: (0, i)))
            out_shapes.append(jax.ShapeDtypeStruct((w.shape[0], t), dt))
    return pl.pallas_call(
        functools.partial(_inproj_kernel, kinds=tuple(kinds), col_chunk=col_chunk),
        grid=(t // tm,), in_specs=in_specs, out_specs=out_specs, out_shape=out_shapes,
        compiler_params=_params(1), name="inproj",
    )(x2d, g.reshape(1, d), *weights)


def _causal_conv(x, buf, w_ref, b_ref):
    ts = x.shape[0]
    buf[SUBLANES:SUBLANES + ts, :] = x
    y = b_ref[...] + w_ref[CONV_W - 1:CONV_W, :] * x
    for k in range(CONV_W - 1):
        off = SUBLANES - (CONV_W - 1) + k
        y = y + w_ref[k:k + 1, :] * buf[off:off + ts, :]
    buf[0:SUBLANES, :] = x[ts - SUBLANES:ts, :]
    return y


def _rglru_kernel(xa_ref, ga_ref, cw_ref, cb_ref, wg_ref, br_ref, bi_ref, lam_ref, o_ref,
                  xbuf, a_s, u_s, h_s, hc):
    ts, width = xa_ref.shape
    hd = width // A_HEADS

    @pl.when(pl.program_id(1) == 0)
    def _():
        xbuf[0:SUBLANES, :] = jnp.zeros((SUBLANES, width), F32)
        hc[...] = jnp.zeros_like(hc)

    conv = _causal_conv(xa_ref[...].astype(F32), xbuf, cw_ref, cb_ref)
    convb = conv.astype(BF16)
    r_pre, i_pre = [], []
    for h in range(A_HEADS):
        gate = _dot(convb[:, h * hd:(h + 1) * hd], wg_ref[h])
        r_pre.append(gate[:, :hd])
        i_pre.append(gate[:, hd:])
    r = _sigmoid(jnp.concatenate(r_pre, axis=-1) + br_ref[...])
    i = _sigmoid(jnp.concatenate(i_pre, axis=-1) + bi_ref[...])
    log_a = (-LRU_C) * r * _softplus(-lam_ref[...])
    a = jnp.exp(log_a)
    u = jnp.sqrt(1.0 - a * a) * (i * conv)

    row = lax.broadcasted_iota(jnp.int32, (ts, width), 0) & (SUBLANES - 1)
    for sh in (1, 2, 4):
        keep = row >= sh
        u = jnp.where(keep, a * pltpu.roll(u, sh, 0) + u, u)
        a = jnp.where(keep, a * pltpu.roll(a, sh, 0), a)
    a_s[...] = a
    u_s[...] = u

    def block(j, h):
        base = pl.multiple_of(j * SUBLANES, SUBLANES)
        hb = u_s[pl.ds(base, SUBLANES), :] + a_s[pl.ds(base, SUBLANES), :] * h
        h_s[pl.ds(base, SUBLANES), :] = hb
        return hb[SUBLANES - 1:SUBLANES, :]

    hc[0:1, :] = lax.fori_loop(0, ts // SUBLANES, block, hc[0:1, :])
    ga = ga_ref[...].astype(F32)
    o_ref[...] = (h_s[...] * jax.nn.gelu(ga)).astype(o_ref.dtype)


def _rglru(proj, conv_w, conv_b, wg, b_r, b_i, lam, bsz, seq, ts=LRU_TILE):
    width = conv_w.shape[1]
    nt = seq // ts
    row = lambda b, t: b * nt + t
    vec = lambda: _resident((1, width))
    return pl.pallas_call(
        _rglru_kernel,
        grid=(bsz, nt),
        in_specs=[pl.BlockSpec((ts, width), lambda b, t: (row(b, t), 0)),
                  pl.BlockSpec((ts, width), lambda b, t: (row(b, t), 1)),
                  _resident(conv_w.shape), vec(), _resident(wg.shape), vec(), vec(), vec()],
        out_specs=pl.BlockSpec((ts, width), lambda b, t: (row(b, t), 0)),
        out_shape=jax.ShapeDtypeStruct((bsz * seq, width), BF16),
        scratch_shapes=[pltpu.VMEM((ts + SUBLANES, width), F32), pltpu.VMEM((ts, width), F32),
                        pltpu.VMEM((ts, width), F32), pltpu.VMEM((ts, width), F32),
                        pltpu.VMEM((SUBLANES, width), F32)],
        compiler_params=_params(2), name="rglru",
    )(proj, proj, conv_w, conv_b.reshape(1, width), wg, b_r.reshape(1, width),
      b_i.reshape(1, width), lam.reshape(1, width))


def _pair_level(n):
    t = lax.broadcasted_iota(jnp.int32, (n, n), 0)
    s = lax.broadcasted_iota(jnp.int32, (n, n), 1)
    x = t ^ s
    lvl = jnp.zeros((n, n), jnp.int32)
    w = 1
    while w < n:
        lvl = lvl + (x >= w).astype(jnp.int32)
        w *= 2
    return jnp.where(s > t, -1, lvl)


def _hgrn2_kernel(q_ref, f_ref, v_ref, g_ref, lbl_ref, ng_ref, o_ref, st, bc_s, *, layer):
    ts, dk = q_ref.shape
    c = CHUNK

    @pl.when(pl.program_id(2) == 0)
    def _():
        st[...] = jnp.zeros_like(st)

    lg = lbl_ref[...]
    e = jnp.exp(lg - jnp.max(lg, axis=0, keepdims=True))
    lb = jnp.sum(e[0:layer + 1, :], axis=0, keepdims=True) / jnp.sum(e, axis=0, keepdims=True)

    row = lax.broadcasted_iota(jnp.int32, (c, dk), 0)
    tt = lax.broadcasted_iota(jnp.int32, (c, c), 0)
    ss = lax.broadcasted_iota(jnp.int32, (c, c), 1)
    tri = (ss <= tt).astype(BF16)
    lvl = _pair_level(c)
    n_lvl = c.bit_length()
    at_level = [lvl == p for p in range(n_lvl)]
    in_right = [None] + [(row & (1 << (p - 1))) != 0 for p in range(1, n_lvl)]
    r4 = row & 3

    for ci in range(ts // c):
        r0 = ci * c
        qf = _silu(q_ref[r0:r0 + c, :].astype(F32))
        f = lb + (1.0 - lb) * _sigmoid(f_ref[r0:r0 + c, :])
        kk = 1.0 - f
        bc = sum(_dot(tri, part) for part in _split3(jnp.log(f)))
        bc_s[ci] = bc
        vb = v_ref[r0:r0 + c, :]

        scores = jnp.where(at_level[0],
                           lax.dot_general(qf.astype(BF16), kk.astype(BF16), NT_DIMS,
                                           preferred_element_type=F32), 0.0)
        for p in range(1, n_lvl):
            w = 1 << (p - 1)
            if w == 1:
                dec = jnp.where(in_right[p], f, 1.0)
            elif w == 2:
                dec = jnp.where(r4 == 0, pltpu.roll(f, c - 1, 0),
                                jnp.where(r4 == 1, 1.0,
                                          jnp.where(r4 == 2, f, f * pltpu.roll(f, 1, 0))))
            else:
                gat = jnp.concatenate(
                    [jnp.broadcast_to(bc_s[ci, b * 2 * w + w - 1:b * 2 * w + w, :], (2 * w, dk))
                     for b in range(c // (2 * w))], axis=0)
                dec = jnp.exp(-jnp.abs(bc - gat))
            z = (jnp.where(in_right[p], qf, kk) * dec).astype(BF16)
            scores = jnp.where(at_level[p],
                               lax.dot_general(z, z, NT_DIMS, preferred_element_type=F32), scores)
        o = _dot(scores.astype(BF16), vb)

        b_last = bc[c - 1:c, :]
        o = o + lax.dot_general((qf * jnp.exp(bc)).astype(BF16), st[...].astype(BF16), NT_DIMS,
                                preferred_element_type=F32)
        kd = (kk * jnp.exp(b_last - bc)).astype(BF16)
        st[...] = st[...] * jnp.exp(b_last) + lax.dot_general(vb, kd, TN_DIMS,
                                                              preferred_element_type=F32)

        o = o * lax.rsqrt(jnp.mean(o * o, axis=-1, keepdims=True) + NORM_EPS)
        gv = g_ref[r0:r0 + c, :].astype(F32)
        o_ref[r0:r0 + c, :] = (o * ng_ref[...] * _silu(gv)).astype(o_ref.dtype)


def _hgrn2(proj, fproj, lb_logits, norm_g, layer, bsz, seq, col0, ts=MIX_TILE):
    dk = LANES
    width = fproj.shape[1]
    nt = seq // ts
    q0, v0, g0 = col0
    row = lambda b, t: b * nt + t
    n_layers = lb_logits.shape[0]
    return pl.pallas_call(
        functools.partial(_hgrn2_kernel, layer=layer),
        grid=(bsz, B_HEADS, nt),
        in_specs=[pl.BlockSpec((ts, dk), lambda b, h, t: (row(b, t), q0 + h)),
                  pl.BlockSpec((ts, dk), lambda b, h, t: (row(b, t), h)),
                  pl.BlockSpec((ts, dk), lambda b, h, t: (row(b, t), v0 + h)),
                  pl.BlockSpec((ts, dk), lambda b, h, t: (row(b, t), g0 + h)),
                  pl.BlockSpec((n_layers, dk), lambda b, h, t: (0, h)),
                  pl.BlockSpec((1, dk), lambda b, h, t: (0, h))],
        out_specs=pl.BlockSpec((ts, dk), lambda b, h, t: (row(b, t), h)),
        out_shape=jax.ShapeDtypeStruct((bsz * seq, width), BF16),
        scratch_shapes=[pltpu.VMEM((dk, dk), F32), pltpu.VMEM((ts // CHUNK, CHUNK, dk), F32)],
        compiler_params=_params(3), name="hgrn2",
    )(proj, fproj, proj, proj, lb_logits, norm_g.reshape(1, width))


def _ssd_expand_matrix():
    e = np.zeros((LANES, M_HPG * LANES + M_HPG * M_HEAD_DIM), np.float32)
    for r in range(M_HPG):
        e[r, r * LANES:(r + 1) * LANES] = 1.0
        e[SUBLANES + r, M_HPG * LANES + r * M_HEAD_DIM:M_HPG * LANES + (r + 1) * M_HEAD_DIM] = 1.0
    return e


def _ssd_kernel(z_ref, x_ref, b_ref, c_ref, dt_ref, cwx_ref, cbx_ref, cwb_ref, cbb_ref,
                cwc_ref, cbc_ref, bias_ref, alog_ref, drow_ref, ng_ref, e_ref, o_ref,
                xbuf, bbuf, cbuf, xc_s, bc_s, cc_s, st):
    ts, gw = x_ref.shape
    n = D_STATE
    c = CHUNK

    @pl.when(pl.program_id(2) == 0)
    def _():
        xbuf[0:SUBLANES, :] = jnp.zeros((SUBLANES, gw), F32)
        bbuf[0:SUBLANES, :] = jnp.zeros((SUBLANES, n), F32)
        cbuf[0:SUBLANES, :] = jnp.zeros((SUBLANES, n), F32)
        st[...] = jnp.zeros_like(st)

    xc_s[...] = _silu(_causal_conv(x_ref[...].astype(F32), xbuf, cwx_ref, cbx_ref))
    bc_s[...] = _silu(_causal_conv(b_ref[...].astype(F32), bbuf, cwb_ref, cbb_ref)).astype(BF16)
    cc_s[...] = _silu(_causal_conv(c_ref[...].astype(F32), cbuf, cwc_ref, cbc_ref)).astype(BF16)

    tt = lax.broadcasted_iota(jnp.int32, (c, c), 0)
    ss = lax.broadcasted_iota(jnp.int32, (c, c), 1)
    causal = ss <= tt
    upper = (tt <= ss).astype(F32)
    head_of_lane = lax.broadcasted_iota(jnp.int32, (c, gw), 1) // M_HEAD_DIM
    in_head = [head_of_lane == r for r in range(M_HPG)]
    low_half = lax.broadcasted_iota(jnp.int32, (1, LANES), 1) < M_HEAD_DIM
    neg_a = -jnp.exp(alog_ref[0])
    e_mat = e_ref[...]
    nb = M_HPG * LANES

    for ci in range(ts // c):
        r0 = ci * c
        dt = _softplus(dt_ref[0, ci] + bias_ref[0])
        a_row = _dot_f32(dt * neg_a, upper)
        dtw = dt * jnp.exp(a_row[:, c - 1:c] - a_row)
        src_row = a_row - jnp.log(dt)
        pt = jnp.concatenate([a_row, dtw, jnp.zeros((c - 2 * SUBLANES, c), F32)], axis=0).T
        hi, mid, lo = _split3(pt)
        ex = _dot(hi, e_mat)
        a_col = ex[:, :nb] + _dot(mid, e_mat[:, :nb]) + _dot(lo, e_mat[:, :nb])
        dtw_ch = ex[:, nb:]

        xc = xc_s[r0:r0 + c, :]
        xb = xc.astype(BF16)
        bm = bc_s[r0:r0 + c, :]
        cm = cc_s[r0:r0 + c, :]
        cmf = cm.astype(F32)
        sb = st[...].astype(BF16)
        cb = lax.dot_general(cm, bm, NT_DIMS, preferred_element_type=F32)
        y = drow_ref[...] * xc
        for r in range(M_HPG):
            a_t = a_col[:, r * LANES:(r + 1) * LANES]
            lmat = jnp.exp(jnp.where(causal, a_t - src_row[r:r + 1, :], -1e30))
            lhs = jnp.concatenate([(cb * lmat).astype(BF16), (cmf * jnp.exp(a_t)).astype(BF16)],
                                  axis=1)
            rhs = jnp.concatenate([jnp.where(in_head[r], xb, 0), jnp.where(in_head[r], sb, 0)],
                                  axis=0)
            y = y + _dot(lhs, rhs)

        e_last = [jnp.exp(a_col[c - 1:c, r * LANES:(r + 1) * LANES]) for r in range(M_HPG)]
        st_decay = jnp.concatenate([jnp.where(low_half, e_last[0], e_last[1]),
                                    jnp.where(low_half, e_last[2], e_last[3])], axis=1)
        st[...] = st[...] * st_decay + lax.dot_general(
            bm, (xc * dtw_ch).astype(BF16), TN_DIMS, preferred_element_type=F32)

        y = y * _silu(z_ref[r0:r0 + c, :].astype(F32))
        y = y * lax.rsqrt(jnp.mean(y * y, axis=-1, keepdims=True) + NORM_EPS)
        o_ref[r0:r0 + c, :] = (y * ng_ref[...]).astype(o_ref.dtype)


def _ssd(proj, dt_t, conv_w, conv_b, dt_bias, a_log, d_skip, norm_g, bsz, seq, ts=MIX_TILE):
    gw = M_HPG * M_HEAD_DIM
    n = D_STATE
    d_inner = M_GROUPS * gw
    nt = seq // ts
    nck = ts // CHUNK
    row = lambda b, t: b * nt + t
    zb, xb = 0, d_inner // gw
    bb, cb = 2 * d_inner // n, 2 * d_inner // n + M_GROUPS
    wbb, wcb = d_inner // n, d_inner // n + M_GROUPS
    cw2 = conv_w.reshape(CONV_W, -1)
    cb2 = conv_b.reshape(1, -1)

    def per_head(v):
        v = jnp.pad(v.reshape(M_GROUPS, M_HPG), ((0, 0), (0, SUBLANES - M_HPG)))
        return jnp.broadcast_to(v[:, :, None], (M_GROUPS, SUBLANES, LANES)).astype(F32)

    drow = jnp.repeat(d_skip.astype(F32), M_HEAD_DIM).reshape(1, d_inner)
    e_mat = jnp.asarray(_ssd_expand_matrix(), BF16)
    g3 = lambda b, g, t: (g, 0, 0)
    return pl.pallas_call(
        _ssd_kernel,
        grid=(bsz, M_GROUPS, nt),
        in_specs=[pl.BlockSpec((ts, gw), lambda b, g, t: (row(b, t), zb + g)),
                  pl.BlockSpec((ts, gw), lambda b, g, t: (row(b, t), xb + g)),
                  pl.BlockSpec((ts, n), lambda b, g, t: (row(b, t), bb + g)),
                  pl.BlockSpec((ts, n), lambda b, g, t: (row(b, t), cb + g)),
                  pl.BlockSpec((1, nck, SUBLANES, LANES), lambda b, g, t: (g, row(b, t), 0, 0)),
                  pl.BlockSpec((CONV_W, gw), lambda b, g, t: (0, g)),
                  pl.BlockSpec((1, gw), lambda b, g, t: (0, g)),
                  pl.BlockSpec((CONV_W, n), lambda b, g, t: (0, wbb + g)),
                  pl.BlockSpec((1, n), lambda b, g, t: (0, wbb + g)),
                  pl.BlockSpec((CONV_W, n), lambda b, g, t: (0, wcb + g)),
                  pl.BlockSpec((1, n), lambda b, g, t: (0, wcb + g)),
                  pl.BlockSpec((1, SUBLANES, LANES), g3),
                  pl.BlockSpec((1, SUBLANES, LANES), g3),
                  pl.BlockSpec((1, gw), lambda b, g, t: (0, g)),
                  pl.BlockSpec((1, gw), lambda b, g, t: (0, g)),
                  _resident(e_mat.shape)],
        out_specs=pl.BlockSpec((ts, gw), lambda b, g, t: (row(b, t), g)),
        out_shape=jax.ShapeDtypeStruct((bsz * seq, d_inner), BF16),
        scratch_shapes=[pltpu.VMEM((ts + SUBLANES, gw), F32), pltpu.VMEM((ts + SUBLANES, n), F32),
                        pltpu.VMEM((ts + SUBLANES, n), F32), pltpu.VMEM((ts, gw), F32),
                        pltpu.VMEM((ts, n), BF16), pltpu.VMEM((ts, n), BF16),
                        pltpu.VMEM((n, gw), F32)],
        compiler_params=_params(3), name="ssd",
    )(proj, proj, proj, proj, dt_t, cw2, cb2, cw2, cb2, cw2, cb2,
      per_head(dt_bias), per_head(a_log), drow, norm_g.reshape(1, d_inner), e_mat)


def _post_kernel(*refs, n_mix, final_norm, ff_chunk):
    mix_refs = refs[:n_mix]
    wmix_refs = refs[n_mix:2 * n_mix]
    (x_ref, gq_ref, wq_ref, k_ref, v_ref, wo_ref, gf_ref, w1_ref, w2_ref, gfin_ref,
     o_ref) = refs[2 * n_mix:]
    d = x_ref.shape[1]
    hd = d // X_HEADS

    x = x_ref[...]
    for m_ref, w_ref in zip(mix_refs, wmix_refs):
        x = x + _dot(m_ref[...], w_ref[...])

    q = _dot(_rms(x, gq_ref[...]).astype(BF16), wq_ref[...]).astype(BF16)
    heads = []
    for h in range(X_HEADS):
        sc = lax.dot_general(q[:, h * hd:(h + 1) * hd], k_ref[0, :, h * hd:(h + 1) * hd], NT_DIMS,
                             preferred_element_type=F32) * (hd ** -0.5)
        p = jnp.exp(sc - jnp.max(sc, axis=-1, keepdims=True))
        p = p / jnp.sum(p, axis=-1, keepdims=True)
        heads.append(_dot(p.astype(BF16), v_ref[0, :, h * hd:(h + 1) * hd]))
    x = x + _dot(jnp.concatenate(heads, axis=-1).astype(BF16), wo_ref[...])

    hn = _rms(x, gf_ref[...]).astype(BF16)
    y = x
    for c in range(0, w1_ref.shape[1], ff_chunk):
        hid = jnp.square(jnp.maximum(_dot(hn, w1_ref[:, c:c + ff_chunk]), 0.0))
        y = y + _dot(hid.astype(BF16), w2_ref[c:c + ff_chunk, :])
    if final_norm:
        y = _rms(y, gfin_ref[...])
    o_ref[...] = y


def _post(mixes, wmixes, x2d, gq, wq, kv, wo, gf, w1, w2, gfin, final_norm, bsz, seq,
          tm=TOKEN_TILE, ff_chunk=1024):
    t, d = x2d.shape
    nt = seq // tm
    mem_len = kv.shape[1]
    tile = lambda w: pl.BlockSpec((tm, w), lambda b, i: (b * nt + i, 0))
    in_specs = [tile(m.shape[1]) for m in mixes] + [_resident(w.shape) for w in wmixes]
    in_specs += [tile(d), _resident((1, d)), _resident(wq.shape),
                 pl.BlockSpec((1, mem_len, d), lambda b, i: (b, 0, 0)),
                 pl.BlockSpec((1, mem_len, d), lambda b, i: (b, 0, 1)),
                 _resident(wo.shape), _resident((1, d)), _resident(w1.shape),
                 _resident(w2.shape), _resident((1, d))]
    return pl.pallas_call(
        functools.partial(_post_kernel, n_mix=len(mixes), final_norm=final_norm,
                          ff_chunk=ff_chunk),
        grid=(bsz, nt), in_specs=in_specs, out_specs=tile(d),
        out_shape=jax.ShapeDtypeStruct((t, d), F32),
        compiler_params=_params(2), name="post",
    )(*mixes, *wmixes, x2d, gq.reshape(1, d), wq, kv, kv, wo, gf.reshape(1, d), w1, w2,
      gfin.reshape(1, d))


def kernel(x, mem, norm_mix_g, norm_mem_q_g, norm_mem_kv_g, norm_ffn_g, final_norm_g, e_w_in, a_conv_w, a_conv_b, a_gate_r_w, a_gate_r_b, a_gate_i_w, a_gate_i_b, a_lambda, b_lb_logits, b_norm_g, e_w_out, o_w_in, m_conv_w, m_conv_b, m_dt_bias, m_a_log, m_d, m_norm_g, o_w_out, xq_w, xk_w, xv_w, xo_w, ffn_w1, ffn_w2):
    bsz, seq, d = x.shape
    mem_len = mem.shape[1]
    depth = norm_mix_g.shape[0]
    t = bsz * seq
    bf = lambda w: w.astype(BF16)
    xs = x.reshape(t, d)
    mem2d = mem.reshape(bsz * mem_len, d)

    for l in range(depth):
        (kv,) = _inproj(mem2d, norm_mem_kv_g[l],
                        [bf(jnp.concatenate([xk_w[l], xv_w[l]], axis=1))], ["nn"], [BF16])
        kv = kv.reshape(bsz, mem_len, 2 * d)
        if l % 2 == 0:
            e = l // 2
            w = e_w_in[e]
            wa = d
            sec = lambda k: w[:, k * wa:(k + 1) * wa]
            w_main = bf(jnp.concatenate([sec(0), sec(1), sec(2), sec(4), sec(5)], axis=1))
            proj, fproj = _inproj(xs, norm_mix_g[l], [w_main, bf(sec(3))], ["nn", "nn"],
                                  [BF16, F32])
            wg = bf(jnp.concatenate([a_gate_r_w[e], a_gate_i_w[e]], axis=-1))
            ya = _rglru(proj, a_conv_w[e], a_conv_b[e], wg, a_gate_r_b[e], a_gate_i_b[e],
                        a_lambda[e], bsz, seq)
            blk = wa // LANES
            yb = _hgrn2(proj, fproj, b_lb_logits, b_norm_g[e], l, bsz, seq,
                        (2 * blk, 3 * blk, 4 * blk))
            mixes = [ya, yb]
            wmixes = [bf(e_w_out[e][:wa]), bf(e_w_out[e][wa:])]
        else:
            o = l // 2
            w = o_w_in[o]
            n_main = w.shape[1] - M_GROUPS * M_HPG
            w_dt = w[:, n_main:].T.reshape(M_GROUPS, M_HPG, d)
            w_dt = jnp.pad(w_dt, ((0, 0), (0, SUBLANES - M_HPG), (0, 0))).reshape(-1, d)
            proj, dt_t = _inproj(xs, norm_mix_g[l], [bf(w[:, :n_main]), bf(w_dt)], ["nn", "nt"],
                                 [BF16, F32])
            dt_t = dt_t.reshape(M_GROUPS, SUBLANES, t // CHUNK, CHUNK).transpose(0, 2, 1, 3)
            ym = _ssd(proj, dt_t, m_conv_w[o], m_conv_b[o], m_dt_bias[o], m_a_log[o], m_d[o],
                      m_norm_g[o], bsz, seq)
            mixes = [ym]
            wmixes = [bf(o_w_out[o])]
        xs = _post(mixes, wmixes, xs, norm_mem_q_g[l], bf(xq_w[l]), kv, bf(xo_w[l]),
                   norm_ffn_g[l], bf(ffn_w1[l]), bf(ffn_w2[l]), final_norm_g,
                   l == depth - 1, bsz, seq)
    return xs.reshape(bsz, seq, d)
```

```python
import functools

import jax
import jax.numpy as jnp
import numpy as np
from jax import lax
from jax.experimental import pallas as pl
from jax.experimental.pallas import tpu as pltpu

F32 = jnp.float32
BF16 = jnp.bfloat16

NORM_EPS = 1e-6
CONV_W = 4
LANES = 128
SUBLANES = 8
VMEM_LIMIT_BYTES = 56 * 1024 * 1024

A_HEADS = 8
LRU_C = 8.0
B_HEADS = 8
M_HEAD_DIM = 64
M_GROUPS = 8
M_HPG = 4
D_STATE = 128
X_HEADS = 4
CHUNK = 128

TOKEN_TILE = 512
LRU_TILE = 256
MIX_TILE = 1024

NT_DIMS = (((1,), (1,)), ((), ()))
TN_DIMS = (((0,), (0,)), ((), ()))


def _params(n_axes):
    return pltpu.CompilerParams(
        dimension_semantics=("arbitrary",) * n_axes, vmem_limit_bytes=VMEM_LIMIT_BYTES)


def _resident(shape):
    nd = len(shape)
    return pl.BlockSpec(shape, lambda *_: (0,) * nd, pipeline_mode=pl.Buffered(1))


def _rms(x, g):
    return x * lax.rsqrt(jnp.mean(x * x, axis=-1, keepdims=True) + NORM_EPS) * g


def _sigmoid(x):
    return 0.5 * jnp.tanh(0.5 * x) + 0.5


def _silu(x):
    h = 0.5 * x
    return h * jnp.tanh(h) + h


def _softplus(x):
    return jnp.maximum(x, 0.0) + jnp.log1p(jnp.exp(-jnp.abs(x)))


def _dot(a, b):
    return jnp.dot(a, b, preferred_element_type=F32)


def _dot_f32(a, b):
    return jnp.dot(a, b, preferred_element_type=F32, precision=lax.Precision.HIGHEST)


def _split3(x):
    hi = x.astype(BF16)
    r = x - hi.astype(F32)
    mid = r.astype(BF16)
    lo = (r - mid.astype(F32)).astype(BF16)
    return hi, mid, lo


def _inproj_kernel(x_ref, g_ref, *refs, kinds, col_chunk):
    n = len(kinds)
    w_refs, o_refs = refs[:n], refs[n:]
    xn = _rms(x_ref[...], g_ref[...]).astype(BF16)
    for kind, w_ref, o_ref in zip(kinds, w_refs, o_refs):
        if kind == "nn":
            for c in range(0, w_ref.shape[1], col_chunk):
                o_ref[:, c:c + col_chunk] = _dot(xn, w_ref[:, c:c + col_chunk]).astype(o_ref.dtype)
        else:
            o_ref[...] = lax.dot_general(w_ref[...], xn, NT_DIMS,
                                         preferred_element_type=F32).astype(o_ref.dtype)


def _inproj(x2d, g, weights, kinds, out_dtypes, tm=TOKEN_TILE, col_chunk=1024):
    t, d = x2d.shape
    tm = min(tm, t)
    assert t % tm == 0
    in_specs = [pl.BlockSpec((tm, d), lambda i: (i, 0)), _resident((1, d))]
    out_specs, out_shapes = [], []
    for w, kind, dt in zip(weights, kinds, out_dtypes):
        in_specs.append(_resident(w.shape))
        if kind == "nn":
            out_specs.append(pl.BlockSpec((tm, w.shape[1]), lambda i: (i, 0)))
            out_shapes.append(jax.ShapeDtypeStruct((t, w.shape[1]), dt))
        else:
            out_specs.append(pl.BlockSpec((w.shape[0], tm), lambda i: (0, i)))
            out_shapes.append(jax.ShapeDtypeStruct((w.shape[0], t), dt))
    return pl.pallas_call(
        functools.partial(_inproj_kernel, kinds=tuple(kinds), col_chunk=col_chunk),
        grid=(t // tm,), in_specs=in_specs, out_specs=out_specs, out_shape=out_shapes,
        compiler_params=_params(1), name="inproj",
    )(x2d, g.reshape(1, d), *weights)


def _causal_conv(x, buf, w_ref, b_ref):
    ts = x.shape[0]
    buf[SUBLANES:SUBLANES + ts, :] = x
    y = b_ref[...] + w_ref[CONV_W - 1:CONV_W, :] * x
    for k in range(CONV_W - 1):
        off = SUBLANES - (CONV_W - 1) + k
        y = y + w_ref[k:k + 1, :] * buf[off:off + ts, :]
    buf[0:SUBLANES, :] = x[ts - SUBLANES:ts, :]
    return y


def _conv_silu(src_ref, buf, w_ref, b_ref, dst_ref, rows=CHUNK):
    ts = src_ref.shape[0]
    buf[SUBLANES:SUBLANES + ts, :] = src_ref[...].astype(F32)
    for r0 in range(0, ts, rows):
        y = b_ref[...] + w_ref[CONV_W - 1:CONV_W, :] * buf[SUBLANES + r0:SUBLANES + r0 + rows, :]
        for k in range(CONV_W - 1):
            off = SUBLANES - (CONV_W - 1) + k + r0
            y = y + w_ref[k:k + 1, :] * buf[off:off + rows, :]
        dst_ref[r0:r0 + rows, :] = _silu(y).astype(dst_ref.dtype)
    buf[0:SUBLANES, :] = buf[ts:ts + SUBLANES, :]


def _rglru_kernel(xa_ref, ga_ref, cw_ref, cb_ref, wg_ref, br_ref, bi_ref, lam_ref, o_ref,
                  xbuf, a_s, u_s, h_s, hc):
    ts, width = xa_ref.shape
    hd = width // A_HEADS

    @pl.when(pl.program_id(1) == 0)
    def _():
        xbuf[0:SUBLANES, :] = jnp.zeros((SUBLANES, width), F32)
        hc[...] = jnp.zeros_like(hc)

    conv = _causal_conv(xa_ref[...].astype(F32), xbuf, cw_ref, cb_ref)
    convb = conv.astype(BF16)
    r_pre, i_pre = [], []
    for h in range(A_HEADS):
        gate = _dot(convb[:, h * hd:(h + 1) * hd], wg_ref[h])
        r_pre.append(gate[:, :hd])
        i_pre.append(gate[:, hd:])
    r = _sigmoid(jnp.concatenate(r_pre, axis=-1) + br_ref[...])
    i = _sigmoid(jnp.concatenate(i_pre, axis=-1) + bi_ref[...])
    log_a = (-LRU_C) * r * _softplus(-lam_ref[...])
    a = jnp.exp(log_a)
    u = jnp.sqrt(1.0 - a * a) * (i * conv)

    row = lax.broadcasted_iota(jnp.int32, (ts, width), 0) & (SUBLANES - 1)
    for sh in (1, 2, 4):
        keep = row >= sh
        u = jnp.where(keep, a * pltpu.roll(u, sh, 0) + u, u)
        a = jnp.where(keep, a * pltpu.roll(a, sh, 0), a)
    a_s[...] = a
    u_s[...] = u

    def block(j, h):
        base = pl.multiple_of(j * SUBLANES, SUBLANES)
        hb = u_s[pl.ds(base, SUBLANES), :] + a_s[pl.ds(base, SUBLANES), :] * h
        h_s[pl.ds(base, SUBLANES), :] = hb
        return hb[SUBLANES - 1:SUBLANES, :]

    hc[0:1, :] = lax.fori_loop(0, ts // SUBLANES, block, hc[0:1, :])
    ga = ga_ref[...].astype(F32)
    o_ref[...] = (h_s[...] * jax.nn.gelu(ga)).astype(o_ref.dtype)


def _rglru(proj, conv_w, conv_b, wg, b_r, b_i, lam, bsz, seq, ts=LRU_TILE):
    width = conv_w.shape[1]
    nt = seq // ts
    row = lambda b, t: b * nt + t
    vec = lambda: _resident((1, width))
    return pl.pallas_call(
        _rglru_kernel,
        grid=(bsz, nt),
        in_specs=[pl.BlockSpec((ts, width), lambda b, t: (row(b, t), 0)),
                  pl.BlockSpec((ts, width), lambda b, t: (row(b, t), 1)),
                  _resident(conv_w.shape), vec(), _resident(wg.shape), vec(), vec(), vec()],
        out_specs=pl.BlockSpec((ts, width), lambda b, t: (row(b, t), 0)),
        out_shape=jax.ShapeDtypeStruct((bsz * seq, width), BF16),
        scratch_shapes=[pltpu.VMEM((ts + SUBLANES, width), F32), pltpu.VMEM((ts, width), F32),
                        pltpu.VMEM((ts, width), F32), pltpu.VMEM((ts, width), F32),
                        pltpu.VMEM((SUBLANES, width), F32)],
        compiler_params=_params(2), name="rglru",
    )(proj, proj, conv_w, conv_b.reshape(1, width), wg, b_r.reshape(1, width),
      b_i.reshape(1, width), lam.reshape(1, width))


def _pair_level(n):
    t = lax.broadcasted_iota(jnp.int32, (n, n), 0)
    s = lax.broadcasted_iota(jnp.int32, (n, n), 1)
    x = t ^ s
    lvl = jnp.zeros((n, n), jnp.int32)
    w = 1
    while w < n:
        lvl = lvl + (x >= w).astype(jnp.int32)
        w *= 2
    return jnp.where(s > t, -1, lvl)


def _hgrn2_kernel(q_ref, f_ref, v_ref, g_ref, lbl_ref, ng_ref, o_ref, st, bc_s, *, layer):
    ts, dk = q_ref.shape
    c = CHUNK

    @pl.when(pl.program_id(2) == 0)
    def _():
        st[...] = jnp.zeros_like(st)

    lg = lbl_ref[...]
    e = jnp.exp(lg - jnp.max(lg, axis=0, keepdims=True))
    lb = jnp.sum(e[0:layer + 1, :], axis=0, keepdims=True) / jnp.sum(e, axis=0, keepdims=True)

    row = lax.broadcasted_iota(jnp.int32, (c, dk), 0)
    tt = lax.broadcasted_iota(jnp.int32, (c, c), 0)
    ss = lax.broadcasted_iota(jnp.int32, (c, c), 1)
    tri = (ss <= tt).astype(BF16)
    lvl = _pair_level(c)
    n_lvl = c.bit_length()
    at_level = [lvl == p for p in range(n_lvl)]
    in_right = [None] + [(row & (1 << (p - 1))) != 0 for p in range(1, n_lvl)]
    r4 = row & 3

    for ci in range(ts // c):
        r0 = ci * c
        qf = _silu(q_ref[r0:r0 + c, :].astype(F32))
        f = lb + (1.0 - lb) * _sigmoid(f_ref[r0:r0 + c, :])
        kk = 1.0 - f
        bc = sum(_dot(tri, part) for part in _split3(jnp.log(f)))
        bc_s[ci] = bc
        vb = v_ref[r0:r0 + c, :]

        scores = jnp.where(at_level[0],
                           lax.dot_general(qf.astype(BF16), kk.astype(BF16), NT_DIMS,
                                           preferred_element_type=F32), 0.0)
        for p in range(1, n_lvl):
            w = 1 << (p - 1)
            if w == 1:
                dec = jnp.where(in_right[p], f, 1.0)
            elif w == 2:
                dec = jnp.where(r4 == 0, pltpu.roll(f, c - 1, 0),
                                jnp.where(r4 == 1, 1.0,
                                          jnp.where(r4 == 2, f, f * pltpu.roll(f, 1, 0))))
            else:
                gat = jnp.concatenate(
                    [jnp.broadcast_to(bc_s[ci, b * 2 * w + w - 1:b * 2 * w + w, :], (2 * w, dk))
                     for b in range(c // (2 * w))], axis=0)
                dec = jnp.exp(-jnp.abs(bc - gat))
            z = (jnp.where(in_right[p], qf, kk) * dec).astype(BF16)
            scores = jnp.where(at_level[p],
                               lax.dot_general(z, z, NT_DIMS, preferred_element_type=F32), scores)
        o = _dot(scores.astype(BF16), vb)

        b_last = bc[c - 1:c, :]
        o = o + lax.dot_general((qf * jnp.exp(bc)).astype(BF16), st[...].astype(BF16), NT_DIMS,
                                preferred_element_type=F32)
        kd = (kk * jnp.exp(b_last - bc)).astype(BF16)
        st[...] = st[...] * jnp.exp(b_last) + lax.dot_general(vb, kd, TN_DIMS,
                                                              preferred_element_type=F32)

        o = o * lax.rsqrt(jnp.mean(o * o, axis=-1, keepdims=True) + NORM_EPS)
        gv = g_ref[r0:r0 + c, :].astype(F32)
        o_ref[r0:r0 + c, :] = (o * ng_ref[...] * _silu(gv)).astype(o_ref.dtype)


def _hgrn2(proj, fproj, lb_logits, norm_g, layer, bsz, seq, col0, ts=MIX_TILE):
    dk = LANES
    width = fproj.shape[1]
    nt = seq // ts
    q0, v0, g0 = col0
    row = lambda b, t: b * nt + t
    n_layers = lb_logits.shape[0]
    return pl.pallas_call(
        functools.partial(_hgrn2_kernel, layer=layer),
        grid=(bsz, B_HEADS, nt),
        in_specs=[pl.BlockSpec((ts, dk), lambda b, h, t: (row(b, t), q0 + h)),
                  pl.BlockSpec((ts, dk), lambda b, h, t: (row(b, t), h)),
                  pl.BlockSpec((ts, dk), lambda b, h, t: (row(b, t), v0 + h)),
                  pl.BlockSpec((ts, dk), lambda b, h, t: (row(b, t), g0 + h)),
                  pl.BlockSpec((n_layers, dk), lambda b, h, t: (0, h)),
                  pl.BlockSpec((1, dk), lambda b, h, t: (0, h))],
        out_specs=pl.BlockSpec((ts, dk), lambda b, h, t: (row(b, t), h)),
        out_shape=jax.ShapeDtypeStruct((bsz * seq, width), BF16),
        scratch_shapes=[pltpu.VMEM((dk, dk), F32), pltpu.VMEM((ts // CHUNK, CHUNK, dk), F32)],
        compiler_params=_params(3), name="hgrn2",
    )(proj, fproj, proj, proj, lb_logits, norm_g.reshape(1, width))


def _ssd_expand_matrices(nck):
    e = np.zeros((nck, LANES, M_HPG * LANES + M_HPG * M_HEAD_DIM), np.float32)
    for c in range(nck):
        for r in range(M_HPG):
            e[c, SUBLANES * c + r, r * LANES:(r + 1) * LANES] = 1.0
            e[c, SUBLANES * (nck + c) + r,
              M_HPG * LANES + r * M_HEAD_DIM:M_HPG * LANES + (r + 1) * M_HEAD_DIM] = 1.0
    return e


def _ssd_kernel(z_ref, x_ref, b_ref, c_ref, dt_ref, cwx_ref, cbx_ref, cwb_ref, cbb_ref,
                cwc_ref, cbc_ref, bias_ref, alog_ref, drow_ref, ng_ref, e_ref, o_ref,
                xbuf, bbuf, cbuf, xc_s, bc_s, cc_s, lhs_s, xw_s, sb_s, dec_s, st):
    ts, gw = x_ref.shape
    n = D_STATE
    c = CHUNK
    nck = ts // c
    nb = M_HPG * LANES

    @pl.when(pl.program_id(2) == 0)
    def _():
        xbuf[0:SUBLANES, :] = jnp.zeros((SUBLANES, gw), F32)
        bbuf[0:SUBLANES, :] = jnp.zeros((SUBLANES, n), F32)
        cbuf[0:SUBLANES, :] = jnp.zeros((SUBLANES, n), F32)
        st[...] = jnp.zeros_like(st)

    _conv_silu(x_ref, xbuf, cwx_ref, cbx_ref, xc_s)
    _conv_silu(b_ref, bbuf, cwb_ref, cbb_ref, bc_s)
    _conv_silu(c_ref, cbuf, cwc_ref, cbc_ref, cc_s)

    tt = lax.broadcasted_iota(jnp.int32, (c, c), 0)
    ss = lax.broadcasted_iota(jnp.int32, (c, c), 1)
    causal = ss <= tt
    upper = (tt <= ss).astype(F32)
    head_of_lane = lax.broadcasted_iota(jnp.int32, (c, gw), 1) // M_HEAD_DIM
    in_head = [head_of_lane == r for r in range(M_HPG)]
    low_half = lax.broadcasted_iota(jnp.int32, (1, LANES), 1) < M_HEAD_DIM

    bias = jnp.concatenate([bias_ref[0]] * nck, axis=0)
    neg_a = jnp.concatenate([-jnp.exp(alog_ref[0])] * nck, axis=0)
    dt = _softplus(dt_ref[0].reshape(nck * SUBLANES, c) + bias)
    a_row = _dot_f32(dt * neg_a, upper)
    dtw = dt * jnp.exp(a_row[:, c - 1:c] - a_row)
    src_row = a_row - jnp.log(dt)
    rows = [a_row, dtw]
    if 2 * nck * SUBLANES < c:
        rows.append(jnp.zeros((c - 2 * nck * SUBLANES, c), F32))
    hi, mid, lo = _split3(jnp.concatenate(rows, axis=0).T)

    for ci in range(nck):
        r0 = ci * c
        e_c = e_ref[ci]
        ex = _dot(hi, e_c)
        a_col = ex[:, :nb] + _dot(mid, e_c[:, :nb]) + _dot(lo, e_c[:, :nb])
        xw_s[r0:r0 + c, :] = (xc_s[r0:r0 + c, :] * ex[:, nb:]).astype(BF16)
        cm = cc_s[r0:r0 + c, :]
        cmf = cm.astype(F32)
        cb = lax.dot_general(cm, bc_s[r0:r0 + c, :], NT_DIMS, preferred_element_type=F32)
        for r in range(M_HPG):
            a_t = a_col[:, r * LANES:(r + 1) * LANES]
            src = src_row[ci * SUBLANES + r:ci * SUBLANES + r + 1, :]
            lmat = jnp.exp(jnp.where(causal, a_t - src, -1e30))
            lhs_s[ci * M_HPG + r] = jnp.concatenate(
                [(cb * lmat).astype(BF16), (cmf * jnp.exp(a_t)).astype(BF16)], axis=1)
        e_last = [jnp.exp(a_col[c - 1:c, r * LANES:(r + 1) * LANES]) for r in range(M_HPG)]
        dec_s[ci:ci + 1, :] = jnp.concatenate([jnp.where(low_half, e_last[0], e_last[1]),
                                               jnp.where(low_half, e_last[2], e_last[3])], axis=1)

    for ci in range(nck):
        r0 = ci * c
        sb_s[ci] = st[...].astype(BF16)
        st[...] = st[...] * dec_s[ci:ci + 1, :] + lax.dot_general(
            bc_s[r0:r0 + c, :], xw_s[r0:r0 + c, :], TN_DIMS, preferred_element_type=F32)

    for ci in range(nck):
        r0 = ci * c
        xc = xc_s[r0:r0 + c, :]
        xb = xc.astype(BF16)
        sb = sb_s[ci]
        y = drow_ref[...] * xc
        for r in range(M_HPG):
            rhs = jnp.concatenate([jnp.where(in_head[r], xb, 0), jnp.where(in_head[r], sb, 0)],
                                  axis=0)
            y = y + _dot(lhs_s[ci * M_HPG + r], rhs)
        y = y * _silu(z_ref[r0:r0 + c, :].astype(F32))
        y = y * lax.rsqrt(jnp.mean(y * y, axis=-1, keepdims=True) + NORM_EPS)
        o_ref[r0:r0 + c, :] = (y * ng_ref[...]).astype(o_ref.dtype)


def _ssd(proj, dt_t, conv_w, conv_b, dt_bias, a_log, d_skip, norm_g, bsz, seq, ts=MIX_TILE):
    gw = M_HPG * M_HEAD_DIM
    n = D_STATE
    d_inner = M_GROUPS * gw
    nt = seq // ts
    nck = ts // CHUNK
    row = lambda b, t: b * nt + t
    zb, xb = 0, d_inner // gw
    bb, cb = 2 * d_inner // n, 2 * d_inner // n + M_GROUPS
    wbb, wcb = d_inner // n, d_inner // n + M_GROUPS
    cw2 = conv_w.reshape(CONV_W, -1)
    cb2 = conv_b.reshape(1, -1)

    def per_head(v):
        v = jnp.pad(v.reshape(M_GROUPS, M_HPG), ((0, 0), (0, SUBLANES - M_HPG)))
        return jnp.broadcast_to(v[:, :, None], (M_GROUPS, SUBLANES, LANES)).astype(F32)

    drow = jnp.repeat(d_skip.astype(F32), M_HEAD_DIM).reshape(1, d_inner)
    e_mat = jnp.asarray(_ssd_expand_matrices(nck), BF16)
    g3 = lambda b, g, t: (g, 0, 0)
    return pl.pallas_call(
        _ssd_kernel,
        grid=(bsz, M_GROUPS, nt),
        in_specs=[pl.BlockSpec((ts, gw), lambda b, g, t: (row(b, t), zb + g)),
                  pl.BlockSpec((ts, gw), lambda b, g, t: (row(b, t), xb + g)),
                  pl.BlockSpec((ts, n), lambda b, g, t: (row(b, t), bb + g)),
                  pl.BlockSpec((ts, n), lambda b, g, t: (row(b, t), cb + g)),
                  pl.BlockSpec((1, nck, SUBLANES, LANES), lambda b, g, t: (g, row(b, t), 0, 0)),
                  pl.BlockSpec((CONV_W, gw), lambda b, g, t: (0, g)),
                  pl.BlockSpec((1, gw), lambda b, g, t: (0, g)),
                  pl.BlockSpec((CONV_W, n), lambda b, g, t: (0, wbb + g)),
                  pl.BlockSpec((1, n), lambda b, g, t: (0, wbb + g)),
                  pl.BlockSpec((CONV_W, n), lambda b, g, t: (0, wcb + g)),
                  pl.BlockSpec((1, n), lambda b, g, t: (0, wcb + g)),
                  pl.BlockSpec((1, SUBLANES, LANES), g3),
                  pl.BlockSpec((1, SUBLANES, LANES), g3),
                  pl.BlockSpec((1, gw), lambda b, g, t: (0, g)),
                  pl.BlockSpec((1, gw), lambda b, g, t: (0, g)),
                  _resident(e_mat.shape)],
        out_specs=pl.BlockSpec((ts, gw), lambda b, g, t: (row(b, t), g)),
        out_shape=jax.ShapeDtypeStruct((bsz * seq, d_inner), BF16),
        scratch_shapes=[pltpu.VMEM((ts + SUBLANES, gw), F32), pltpu.VMEM((ts + SUBLANES, n), F32),
                        pltpu.VMEM((ts + SUBLANES, n), F32), pltpu.VMEM((ts, gw), F32),
                        pltpu.VMEM((ts, n), BF16), pltpu.VMEM((ts, n), BF16),
                        pltpu.VMEM((nck * M_HPG, CHUNK, CHUNK + n), BF16),
                        pltpu.VMEM((ts, gw), BF16), pltpu.VMEM((nck, n, gw), BF16),
                        pltpu.VMEM((max(nck, SUBLANES), gw), F32), pltpu.VMEM((n, gw), F32)],
        compiler_params=_params(3), name="ssd",
    )(proj, proj, proj, proj, dt_t, cw2, cb2, cw2, cb2, cw2, cb2,
      per_head(dt_bias), per_head(a_log), drow, norm_g.reshape(1, d_inner), e_mat)


def _post_kernel(*refs, n_mix, final_norm, ff_chunk):
    mix_refs = refs[:n_mix]
    wmix_refs = refs[n_mix:2 * n_mix]
    (x_ref, gq_ref, wq_ref, k_ref, v_ref, wo_ref, gf_ref, w1_ref, w2_ref, gfin_ref,
     o_ref) = refs[2 * n_mix:]
    d = x_ref.shape[1]
    hd = d // X_HEADS

    x = x_ref[...]
    for m_ref, w_ref in zip(mix_refs, wmix_refs):
        x = x + _dot(m_ref[...], w_ref[...])

    q = _dot(_rms(x, gq_ref[...]).astype(BF16), wq_ref[...]).astype(BF16)
    heads = []
    for h in range(X_HEADS):
        sc = lax.dot_general(q[:, h * hd:(h + 1) * hd], k_ref[0, :, h * hd:(h + 1) * hd], NT_DIMS,
                             preferred_element_type=F32) * (hd ** -0.5)
        p = jnp.exp(sc - jnp.max(sc, axis=-1, keepdims=True))
        p = p / jnp.sum(p, axis=-1, keepdims=True)
        heads.append(_dot(p.astype(BF16), v_ref[0, :, h * hd:(h + 1) * hd]))
    x = x + _dot(jnp.concatenate(heads, axis=-1).astype(BF16), wo_ref[...])

    hn = _rms(x, gf_ref[...]).astype(BF16)
    y = x
    for c in range(0, w1_ref.shape[1], ff_chunk):
        hid = jnp.square(jnp.maximum(_dot(hn, w1_ref[:, c:c + ff_chunk]), 0.0))
        y = y + _dot(hid.astype(BF16), w2_ref[c:c + ff_chunk, :])
    if final_norm:
        y = _rms(y, gfin_ref[...])
    o_ref[...] = y


def _post(mixes, wmixes, x2d, gq, wq, kv, wo, gf, w1, w2, gfin, final_norm, bsz, seq,
          tm=TOKEN_TILE, ff_chunk=1024):
    t, d = x2d.shape
    nt = seq // tm
    mem_len = kv.shape[1]
    tile = lambda w: pl.BlockSpec((tm, w), lambda b, i: (b * nt + i, 0))
    in_specs = [tile(m.shape[1]) for m in mixes] + [_resident(w.shape) for w in wmixes]
    in_specs += [tile(d), _resident((1, d)), _resident(wq.shape),
                 pl.BlockSpec((1, mem_len, d), lambda b, i: (b, 0, 0)),
                 pl.BlockSpec((1, mem_len, d), lambda b, i: (b, 0, 1)),
                 _resident(wo.shape), _resident((1, d)), _resident(w1.shape),
                 _resident(w2.shape), _resident((1, d))]
    return pl.pallas_call(
        functools.partial(_post_kernel, n_mix=len(mixes), final_norm=final_norm,
                          ff_chunk=ff_chunk),
        grid=(bsz, nt), in_specs=in_specs, out_specs=tile(d),
        out_shape=jax.ShapeDtypeStruct((t, d), F32),
        compiler_params=_params(2), name="post",
    )(*mixes, *wmixes, x2d, gq.reshape(1, d), wq, kv, kv, wo, gf.reshape(1, d), w1, w2,
      gfin.reshape(1, d))


def kernel(x, mem, norm_mix_g, norm_mem_q_g, norm_mem_kv_g, norm_ffn_g, final_norm_g, e_w_in, a_conv_w, a_conv_b, a_gate_r_w, a_gate_r_b, a_gate_i_w, a_gate_i_b, a_lambda, b_lb_logits, b_norm_g, e_w_out, o_w_in, m_conv_w, m_conv_b, m_dt_bias, m_a_log, m_d, m_norm_g, o_w_out, xq_w, xk_w, xv_w, xo_w, ffn_w1, ffn_w2):
    bsz, seq, d = x.shape
    mem_len = mem.shape[1]
    depth = norm_mix_g.shape[0]
    t = bsz * seq
    bf = lambda w: w.astype(BF16)
    xs = x.reshape(t, d)
    mem2d = mem.reshape(bsz * mem_len, d)

    for l in range(depth):
        (kv,) = _inproj(mem2d, norm_mem_kv_g[l],
                        [bf(jnp.concatenate([xk_w[l], xv_w[l]], axis=1))], ["nn"], [BF16])
        kv = kv.reshape(bsz, mem_len, 2 * d)
        if l % 2 == 0:
            e = l // 2
            w = e_w_in[e]
            wa = d
            sec = lambda k: w[:, k * wa:(k + 1) * wa]
            w_main = bf(jnp.concatenate([sec(0), sec(1), sec(2), sec(4), sec(5)], axis=1))
            proj, fproj = _inproj(xs, norm_mix_g[l], [w_main, bf(sec(3))], ["nn", "nn"],
                                  [BF16, F32])
            wg = bf(jnp.concatenate([a_gate_r_w[e], a_gate_i_w[e]], axis=-1))
            ya = _rglru(proj, a_conv_w[e], a_conv_b[e], wg, a_gate_r_b[e], a_gate_i_b[e],
                        a_lambda[e], bsz, seq)
            blk = wa // LANES
            yb = _hgrn2(proj, fproj, b_lb_logits, b_norm_g[e], l, bsz, seq,
                        (2 * blk, 3 * blk, 4 * blk))
            mixes = [ya, yb]
            wmixes = [bf(e_w_out[e][:wa]), bf(e_w_out[e][wa:])]
        else:
            o = l // 2
            w = o_w_in[o]
            n_main = w.shape[1] - M_GROUPS * M_HPG
            w_dt = w[:, n_main:].T.reshape(M_GROUPS, M_HPG, d)
            w_dt = jnp.pad(w_dt, ((0, 0), (0, SUBLANES - M_HPG), (0, 0))).reshape(-1, d)
            proj, dt_t = _inproj(xs, norm_mix_g[l], [bf(w[:, :n_main]), bf(w_dt)], ["nn", "nt"],
                                 [BF16, F32])
            dt_t = dt_t.reshape(M_GROUPS, SUBLANES, t // CHUNK, CHUNK).transpose(0, 2, 1, 3)
            ym = _ssd(proj, dt_t, m_conv_w[o], m_conv_b[o], m_dt_bias[o], m_a_log[o], m_d[o],
                      m_norm_g[o], bsz, seq)
            mixes = [ym]
            wmixes = [bf(o_w_out[o])]
        xs = _post(mixes, wmixes, xs, norm_mem_q_g[l], bf(xq_w[l]), kv, bf(xo_w[l]),
                   norm_ffn_g[l], bf(ffn_w1[l]), bf(ffn_w2[l]), final_norm_g,
                   l == depth - 1, bsz, seq)
    return xs.reshape(bsz, seq, d)
```

```python
import functools

import jax
import jax.numpy as jnp
import numpy as np
from jax import lax
from jax.experimental import pallas as pl
from jax.experimental.pallas import tpu as pltpu

F32 = jnp.float32
BF16 = jnp.bfloat16

NORM_EPS = 1e-6
LOG2_E = 1.4426950408889634
CONV_W = 4
LANES = 128
SUBLANES = 8
VMEM_LIMIT_BYTES = 56 * 1024 * 1024

A_HEADS = 8
LRU_C = 8.0
B_HEADS = 8
M_HEAD_DIM = 64
M_GROUPS = 8
M_HPG = 4
D_STATE = 128
X_HEADS = 4
CHUNK = 128

TOKEN_TILE = 512
LRU_TILE = 256
MIX_TILE = 1024

NT_DIMS = (((1,), (1,)), ((), ()))
TN_DIMS = (((0,), (0,)), ((), ()))


def _params(n_axes):
    return pltpu.CompilerParams(
        dimension_semantics=("arbitrary",) * n_axes, vmem_limit_bytes=VMEM_LIMIT_BYTES)


def _resident(shape):
    nd = len(shape)
    return pl.BlockSpec(shape, lambda *_: (0,) * nd, pipeline_mode=pl.Buffered(1))


def _rms(x, g):
    return x * lax.rsqrt(jnp.mean(x * x, axis=-1, keepdims=True) + NORM_EPS) * g


def _sigmoid(x):
    return 0.5 * jnp.tanh(0.5 * x) + 0.5


def _silu(x):
    h = 0.5 * x
    return h * jnp.tanh(h) + h


def _softplus(x):
    return jnp.maximum(x, 0.0) + jnp.log1p(jnp.exp(-jnp.abs(x)))


def _dot(a, b):
    return jnp.dot(a, b, preferred_element_type=F32)


def _dot_f32(a, b):
    return jnp.dot(a, b, preferred_element_type=F32, precision=lax.Precision.HIGHEST)


def _split3(x):
    hi = x.astype(BF16)
    r = x - hi.astype(F32)
    mid = r.astype(BF16)
    lo = (r - mid.astype(F32)).astype(BF16)
    return hi, mid, lo


def _inproj_kernel(x_ref, g_ref, *refs, kinds, col_chunk):
    n = len(kinds)
    w_refs, o_refs = refs[:n], refs[n:]
    xn = _rms(x_ref[...], g_ref[...]).astype(BF16)
    for kind, w_ref, o_ref in zip(kinds, w_refs, o_refs):
        if kind == "nn":
            for c in range(0, w_ref.shape[1], col_chunk):
                o_ref[:, c:c + col_chunk] = _dot(xn, w_ref[:, c:c + col_chunk]).astype(o_ref.dtype)
        else:
            o_ref[...] = lax.dot_general(w_ref[...], xn, NT_DIMS,
                                         preferred_element_type=F32).astype(o_ref.dtype)


def _inproj(x2d, g, weights, kinds, out_dtypes, tm=TOKEN_TILE, col_chunk=1024):
    t, d = x2d.shape
    tm = min(tm, t)
    assert t % tm == 0
    in_specs = [pl.BlockSpec((tm, d), lambda i: (i, 0)), _resident((1, d))]
    out_specs, out_shapes = [], []
    for w, kind, dt in zip(weights, kinds, out_dtypes):
        in_specs.append(_resident(w.shape))
        if kind == "nn":
            out_specs.append(pl.BlockSpec((tm, w.shape[1]), lambda i: (i, 0)))
            out_shapes.append(jax.ShapeDtypeStruct((t, w.shape[1]), dt))
        else:
            out_specs.append(pl.BlockSpec((w.shape[0], tm), lambda i: (0, i)))
            out_shapes.append(jax.ShapeDtypeStruct((w.shape[0], t), dt))
    return pl.pallas_call(
        functools.partial(_inproj_kernel, kinds=tuple(kinds), col_chunk=col_chunk),
        grid=(t // tm,), in_specs=in_specs, out_specs=out_specs, out_shape=out_shapes,
        compiler_params=_params(1), name="inproj",
    )(x2d, g.reshape(1, d), *weights)


def _causal_conv(x, buf, w_ref, b_ref):
    ts = x.shape[0]
    buf[SUBLANES:SUBLANES + ts, :] = x
    y = b_ref[...] + w_ref[CONV_W - 1:CONV_W, :] * x
    for k in range(CONV_W - 1):
        off = SUBLANES - (CONV_W - 1) + k
        y = y + w_ref[k:k + 1, :] * buf[off:off + ts, :]
    buf[0:SUBLANES, :] = x[ts - SUBLANES:ts, :]
    return y


def _conv_silu(src_ref, buf, w_ref, b_ref, dst_ref, rows=CHUNK):
    ts = src_ref.shape[0]
    buf[SUBLANES:SUBLANES + ts, :] = src_ref[...].astype(F32)
    for r0 in range(0, ts, rows):
        y = b_ref[...] + w_ref[CONV_W - 1:CONV_W, :] * buf[SUBLANES + r0:SUBLANES + r0 + rows, :]
        for k in range(CONV_W - 1):
            off = SUBLANES - (CONV_W - 1) + k + r0
            y = y + w_ref[k:k + 1, :] * buf[off:off + rows, :]
        dst_ref[r0:r0 + rows, :] = _silu(y).astype(dst_ref.dtype)
    buf[0:SUBLANES, :] = buf[ts:ts + SUBLANES, :]


def _segment_permutation(ts):
    seg = ts // SUBLANES
    p = np.zeros((ts, ts), np.float32)
    for j in range(seg):
        for s in range(SUBLANES):
            p[SUBLANES * j + s, seg * s + j] = 1.0
    return p


def _rglru_kernel(xa_ref, ga_ref, perm_ref, cw_ref, cb_ref, wg_ref, br_ref, bi_ref, lam_ref, o_ref,
                  xe_s, a_s, u_s, h_s, ap_s, tail, hc):
    ts, width = xa_ref.shape
    hd = width // A_HEADS
    seg = ts // SUBLANES
    halo = (CONV_W - 1) * SUBLANES

    @pl.when(pl.program_id(1) == 0)
    def _():
        tail[...] = jnp.zeros_like(tail)
        hc[...] = jnp.zeros_like(hc)

    perm = perm_ref[...]
    xa = _dot(perm, xa_ref[...])
    ga = _dot(perm, ga_ref[...])

    first_sublane = lax.broadcasted_iota(jnp.int32, (SUBLANES, width), 0) == 0
    for i in range(CONV_W - 1):
        blk = xa[ts - halo + i * SUBLANES:ts - halo + (i + 1) * SUBLANES, :]
        xe_s[i * SUBLANES:(i + 1) * SUBLANES, :] = jnp.where(
            first_sublane, tail[i:i + 1, :], pltpu.roll(blk, 1, 0))
        tail[i:i + 1, :] = blk[SUBLANES - 1:SUBLANES, :]
    xe_s[halo:halo + ts, :] = xa
    conv = cb_ref[...] + cw_ref[CONV_W - 1:CONV_W, :] * xa
    for k in range(CONV_W - 1):
        conv = conv + cw_ref[k:k + 1, :] * xe_s[k * SUBLANES:k * SUBLANES + ts, :]

    convb = conv.astype(BF16)
    r_pre, i_pre = [], []
    for h in range(A_HEADS):
        gate = _dot(convb[:, h * hd:(h + 1) * hd], wg_ref[h])
        r_pre.append(gate[:, :hd])
        i_pre.append(gate[:, hd:])
    tr = jnp.tanh(jnp.concatenate(r_pre, axis=-1) + br_ref[...])
    ti = jnp.tanh(jnp.concatenate(i_pre, axis=-1) + bi_ref[...])
    half_c = (-0.5 * LRU_C) * _softplus(-lam_ref[...])
    a = jnp.exp(half_c * tr + half_c)
    a_s[...] = a
    v = 1.0 - a * a
    half_conv = 0.5 * conv
    u_s[...] = (v * lax.rsqrt(jnp.maximum(v, 1e-30))) * (half_conv * ti + half_conv)

    h = jnp.zeros((SUBLANES, width), F32)
    ap = jnp.ones((SUBLANES, width), F32)
    for j in range(seg):
        rows = slice(j * SUBLANES, (j + 1) * SUBLANES)
        aj = a_s[rows, :]
        h = aj * h + u_s[rows, :]
        ap = aj * ap
        h_s[rows, :] = h
        ap_s[rows, :] = ap

    sub = lax.broadcasted_iota(jnp.int32, (SUBLANES, width), 0)
    for sh in (1, 2, 4):
        keep = sub >= sh
        h = jnp.where(keep, ap * pltpu.roll(h, sh, 0) + h, h)
        ap = jnp.where(keep, ap * pltpu.roll(ap, sh, 0), ap)
    leave = h + ap * hc[0:1, :]
    enter = jnp.where(sub == 0, hc[0:1, :], pltpu.roll(leave, 1, 0))
    hc[0:1, :] = leave[SUBLANES - 1:SUBLANES, :]

    enter_t = jnp.concatenate([enter] * seg, axis=0)
    y = ((h_s[...] + ap_s[...] * enter_t) * jax.nn.gelu(ga)).astype(BF16)
    o_ref[...] = lax.dot_general(perm, y, TN_DIMS, preferred_element_type=F32).astype(o_ref.dtype)


def _rglru(proj, conv_w, conv_b, wg, b_r, b_i, lam, bsz, seq, ts=LRU_TILE):
    width = conv_w.shape[1]
    nt = seq // ts
    row = lambda b, t: b * nt + t
    vec = lambda: _resident((1, width))
    perm = jnp.asarray(_segment_permutation(ts), BF16)
    tile = lambda: pltpu.VMEM((ts, width), F32)
    return pl.pallas_call(
        _rglru_kernel,
        grid=(bsz, nt),
        in_specs=[pl.BlockSpec((ts, width), lambda b, t: (row(b, t), 0)),
                  pl.BlockSpec((ts, width), lambda b, t: (row(b, t), 1)),
                  _resident(perm.shape), _resident(conv_w.shape), vec(), _resident(wg.shape),
                  vec(), vec(), vec()],
        out_specs=pl.BlockSpec((ts, width), lambda b, t: (row(b, t), 0)),
        out_shape=jax.ShapeDtypeStruct((bsz * seq, width), BF16),
        scratch_shapes=[pltpu.VMEM((ts + (CONV_W - 1) * SUBLANES, width), F32),
                        tile(), tile(), tile(), tile(),
                        pltpu.VMEM((SUBLANES, width), F32), pltpu.VMEM((SUBLANES, width), F32)],
        compiler_params=_params(2), name="rglru",
    )(proj, proj, perm, conv_w, conv_b.reshape(1, width), wg, b_r.reshape(1, width),
      b_i.reshape(1, width), lam.reshape(1, width))


def _pair_level(n):
    t = lax.broadcasted_iota(jnp.int32, (n, n), 0)
    s = lax.broadcasted_iota(jnp.int32, (n, n), 1)
    x = t ^ s
    lvl = jnp.zeros((n, n), jnp.int32)
    w = 1
    while w < n:
        lvl = lvl + (x >= w).astype(jnp.int32)
        w *= 2
    return jnp.where(s > t, -1, lvl)


def _hgrn2_kernel(q_ref, f_ref, v_ref, g_ref, lbl_ref, ng_ref, o_ref, st, bc_s, *, layer):
    ts, dk = q_ref.shape
    c = CHUNK

    @pl.when(pl.program_id(2) == 0)
    def _():
        st[...] = jnp.zeros_like(st)

    lg = lbl_ref[...]
    e = jnp.exp(lg - jnp.max(lg, axis=0, keepdims=True))
    lb = jnp.sum(e[0:layer + 1, :], axis=0, keepdims=True) / jnp.sum(e, axis=0, keepdims=True)

    row = lax.broadcasted_iota(jnp.int32, (c, dk), 0)
    tt = lax.broadcasted_iota(jnp.int32, (c, c), 0)
    ss = lax.broadcasted_iota(jnp.int32, (c, c), 1)
    tri = (ss <= tt).astype(BF16)
    lvl = _pair_level(c)
    n_lvl = c.bit_length()
    lvl_b = lvl.astype(F32).astype(BF16)
    right_b = [None] + [((row >> (p - 1)) & 1).astype(F32).astype(BF16) for p in range(1, n_lvl)]
    side = [None] + [jnp.where((row & (1 << (p - 1))) != 0, 1.0, -1.0) for p in range(1, n_lvl)]
    r4 = row & 3

    for ci in range(ts // c):
        r0 = ci * c
        qf = _silu(q_ref[r0:r0 + c, :].astype(F32))
        f = lb + (1.0 - lb) * _sigmoid(f_ref[r0:r0 + c, :])
        kk = 1.0 - f
        qb, kb = qf.astype(BF16), kk.astype(BF16)
        bc = LOG2_E * sum(_dot(tri, part) for part in _split3(jnp.log(f)))
        bc_s[ci] = bc
        vb = v_ref[r0:r0 + c, :]

        scores = jnp.where(lvl_b == 0,
                           lax.dot_general(qb, kb, NT_DIMS,
                                           preferred_element_type=F32).astype(BF16), 0)
        for p in range(1, n_lvl):
            w = 1 << (p - 1)
            if w == 1:
                dec = jnp.where(right_b[p] > 0, f.astype(BF16), 1)
            elif w == 2:
                dec = jnp.where(r4 == 0, pltpu.roll(f, c - 1, 0),
                                jnp.where(r4 == 1, 1.0,
                                          jnp.where(r4 == 2, f, f * pltpu.roll(f, 1, 0)))
                                ).astype(BF16)
            else:
                gat = jnp.concatenate(
                    [jnp.broadcast_to(bc_s[ci, b * 2 * w + w - 1:b * 2 * w + w, :], (2 * w, dk))
                     for b in range(c // (2 * w))], axis=0)
                dec = jnp.exp2((bc - gat) * side[p]).astype(BF16)
            z = jnp.where(right_b[p] > 0, qb, kb) * dec
            scores = jnp.where(lvl_b == p,
                               lax.dot_general(z, z, NT_DIMS,
                                               preferred_element_type=F32).astype(BF16), scores)
        o = _dot(scores, vb)

        b_last = bc[c - 1:c, :]
        o = o + lax.dot_general((qf * jnp.exp2(bc)).astype(BF16), st[...].astype(BF16), NT_DIMS,
                                preferred_element_type=F32)
        kd = (kk * jnp.exp2(b_last - bc)).astype(BF16)
        st[...] = st[...] * jnp.exp2(b_last) + lax.dot_general(vb, kd, TN_DIMS,
                                                               preferred_element_type=F32)

        o = o * lax.rsqrt(jnp.mean(o * o, axis=-1, keepdims=True) + NORM_EPS)
        gv = g_ref[r0:r0 + c, :].astype(F32)
        o_ref[r0:r0 + c, :] = (o * ng_ref[...] * _silu(gv)).astype(o_ref.dtype)


def _hgrn2(proj, fproj, lb_logits, norm_g, layer, bsz, seq, col0, ts=MIX_TILE):
    dk = LANES
    width = fproj.shape[1]
    nt = seq // ts
    q0, v0, g0 = col0
    row = lambda b, t: b * nt + t
    n_layers = lb_logits.shape[0]
    return pl.pallas_call(
        functools.partial(_hgrn2_kernel, layer=layer),
        grid=(bsz, B_HEADS, nt),
        in_specs=[pl.BlockSpec((ts, dk), lambda b, h, t: (row(b, t), q0 + h)),
                  pl.BlockSpec((ts, dk), lambda b, h, t: (row(b, t), h)),
                  pl.BlockSpec((ts, dk), lambda b, h, t: (row(b, t), v0 + h)),
                  pl.BlockSpec((ts, dk), lambda b, h, t: (row(b, t), g0 + h)),
                  pl.BlockSpec((n_layers, dk), lambda b, h, t: (0, h)),
                  pl.BlockSpec((1, dk), lambda b, h, t: (0, h))],
        out_specs=pl.BlockSpec((ts, dk), lambda b, h, t: (row(b, t), h)),
        out_shape=jax.ShapeDtypeStruct((bsz * seq, width), BF16),
        scratch_shapes=[pltpu.VMEM((dk, dk), F32), pltpu.VMEM((ts // CHUNK, CHUNK, dk), F32)],
        compiler_params=_params(3), name="hgrn2",
    )(proj, fproj, proj, proj, lb_logits, norm_g.reshape(1, width))


def _ssd_expand_matrices(nck):
    e = np.zeros((nck, LANES, M_HPG * LANES + M_HPG * M_HEAD_DIM), np.float32)
    for c in range(nck):
        for r in range(M_HPG):
            e[c, SUBLANES * c + r, r * LANES:(r + 1) * LANES] = 1.0
            e[c, SUBLANES * (nck + c) + r,
              M_HPG * LANES + r * M_HEAD_DIM:M_HPG * LANES + (r + 1) * M_HEAD_DIM] = 1.0
    return e


def _ssd_kernel(z_ref, x_ref, b_ref, c_ref, dt_ref, cwx_ref, cbx_ref, cwb_ref, cbb_ref,
                cwc_ref, cbc_ref, bias_ref, alog_ref, drow_ref, ng_ref, e_ref, o_ref,
                xbuf, bbuf, cbuf, xc_s, bc_s, cc_s, lhs_s, xw_s, sb_s, dec_s, st):
    ts, gw = x_ref.shape
    n = D_STATE
    c = CHUNK
    nck = ts // c
    nb = M_HPG * LANES

    @pl.when(pl.program_id(2) == 0)
    def _():
        xbuf[0:SUBLANES, :] = jnp.zeros((SUBLANES, gw), F32)
        bbuf[0:SUBLANES, :] = jnp.zeros((SUBLANES, n), F32)
        cbuf[0:SUBLANES, :] = jnp.zeros((SUBLANES, n), F32)
        st[...] = jnp.zeros_like(st)

    _conv_silu(x_ref, xbuf, cwx_ref, cbx_ref, xc_s)
    _conv_silu(b_ref, bbuf, cwb_ref, cbb_ref, bc_s)
    _conv_silu(c_ref, cbuf, cwc_ref, cbc_ref, cc_s)

    tt = lax.broadcasted_iota(jnp.int32, (c, c), 0)
    ss = lax.broadcasted_iota(jnp.int32, (c, c), 1)
    causal = ss <= tt
    upper = (tt <= ss).astype(F32)
    low_half = lax.broadcasted_iota(jnp.int32, (1, LANES), 1) < M_HEAD_DIM
    low_lanes = lax.broadcasted_iota(jnp.int32, (c + D_STATE, LANES), 1) < M_HEAD_DIM

    bias = jnp.concatenate([bias_ref[0]] * nck, axis=0)
    neg_a = jnp.concatenate([-jnp.exp(alog_ref[0])] * nck, axis=0)
    dt = _softplus(dt_ref[0].reshape(nck * SUBLANES, c) + bias)
    a_row = LOG2_E * _dot_f32(dt * neg_a, upper)
    dtw = dt * jnp.exp2(a_row[:, c - 1:c] - a_row)
    src_row = a_row - LOG2_E * jnp.log(dt)
    rows = [a_row, dtw]
    if 2 * nck * SUBLANES < c:
        rows.append(jnp.zeros((c - 2 * nck * SUBLANES, c), F32))
    hi, mid, lo = _split3(jnp.concatenate(rows, axis=0).T)

    for ci in range(nck):
        r0 = ci * c
        e_c = e_ref[ci]
        ex = _dot(hi, e_c)
        a_col = ex[:, :nb] + _dot(mid, e_c[:, :nb]) + _dot(lo, e_c[:, :nb])
        xw_s[r0:r0 + c, :] = (xc_s[r0:r0 + c, :] * ex[:, nb:]).astype(BF16)
        cm = cc_s[r0:r0 + c, :]
        cmf = cm.astype(F32)
        cb = lax.dot_general(cm, bc_s[r0:r0 + c, :], NT_DIMS, preferred_element_type=F32)
        for r in range(M_HPG):
            a_t = a_col[:, r * LANES:(r + 1) * LANES]
            src = src_row[ci * SUBLANES + r:ci * SUBLANES + r + 1, :]
            lmat = jnp.exp2(jnp.where(causal, a_t - src, -1e30))
            lhs_s[ci * M_HPG + r] = jnp.concatenate(
                [(cb * lmat).astype(BF16), (cmf * jnp.exp2(a_t)).astype(BF16)], axis=1)
        e_last = [jnp.exp2(a_col[c - 1:c, r * LANES:(r + 1) * LANES]) for r in range(M_HPG)]
        dec_s[ci:ci + 1, :] = jnp.concatenate([jnp.where(low_half, e_last[0], e_last[1]),
                                               jnp.where(low_half, e_last[2], e_last[3])], axis=1)

    for ci in range(nck):
        r0 = ci * c
        sb_s[ci] = st[...].astype(BF16)
        st[...] = st[...] * dec_s[ci:ci + 1, :] + lax.dot_general(
            bc_s[r0:r0 + c, :], xw_s[r0:r0 + c, :], TN_DIMS, preferred_element_type=F32)

    for ci in range(nck):
        r0 = ci * c
        xc = xc_s[r0:r0 + c, :]
        rhs = jnp.concatenate([xc.astype(BF16), sb_s[ci]], axis=0)
        tiles = []
        for j in range(gw // LANES):
            rt = rhs[:, j * LANES:(j + 1) * LANES]
            tiles.append(_dot(lhs_s[ci * M_HPG + 2 * j], jnp.where(low_lanes, rt, 0))
                         + _dot(lhs_s[ci * M_HPG + 2 * j + 1], jnp.where(low_lanes, 0, rt)))
        y = drow_ref[...] * xc + jnp.concatenate(tiles, axis=1)
        y = y * _silu(z_ref[r0:r0 + c, :].astype(F32))
        y = y * lax.rsqrt(jnp.mean(y * y, axis=-1, keepdims=True) + NORM_EPS)
        o_ref[r0:r0 + c, :] = (y * ng_ref[...]).astype(o_ref.dtype)


def _ssd(proj, dt_t, conv_w, conv_b, dt_bias, a_log, d_skip, norm_g, bsz, seq, ts=MIX_TILE):
    gw = M_HPG * M_HEAD_DIM
    n = D_STATE
    d_inner = M_GROUPS * gw
    nt = seq // ts
    nck = ts // CHUNK
    row = lambda b, t: b * nt + t
    zb, xb = 0, d_inner // gw
    bb, cb = 2 * d_inner // n, 2 * d_inner // n + M_GROUPS
    wbb, wcb = d_inner // n, d_inner // n + M_GROUPS
    cw2 = conv_w.reshape(CONV_W, -1)
    cb2 = conv_b.reshape(1, -1)

    def per_head(v):
        v = jnp.pad(v.reshape(M_GROUPS, M_HPG), ((0, 0), (0, SUBLANES - M_HPG)))
        return jnp.broadcast_to(v[:, :, None], (M_GROUPS, SUBLANES, LANES)).astype(F32)

    drow = jnp.repeat(d_skip.astype(F32), M_HEAD_DIM).reshape(1, d_inner)
    e_mat = jnp.asarray(_ssd_expand_matrices(nck), BF16)
    g3 = lambda b, g, t: (g, 0, 0)
    return pl.pallas_call(
        _ssd_kernel,
        grid=(bsz, M_GROUPS, nt),
        in_specs=[pl.BlockSpec((ts, gw), lambda b, g, t: (row(b, t), zb + g)),
                  pl.BlockSpec((ts, gw), lambda b, g, t: (row(b, t), xb + g)),
                  pl.BlockSpec((ts, n), lambda b, g, t: (row(b, t), bb + g)),
                  pl.BlockSpec((ts, n), lambda b, g, t: (row(b, t), cb + g)),
                  pl.BlockSpec((1, nck, SUBLANES, LANES), lambda b, g, t: (g, row(b, t), 0, 0)),
                  pl.BlockSpec((CONV_W, gw), lambda b, g, t: (0, g)),
                  pl.BlockSpec((1, gw), lambda b, g, t: (0, g)),
                  pl.BlockSpec((CONV_W, n), lambda b, g, t: (0, wbb + g)),
                  pl.BlockSpec((1, n), lambda b, g, t: (0, wbb + g)),
                  pl.BlockSpec((CONV_W, n), lambda b, g, t: (0, wcb + g)),
                  pl.BlockSpec((1, n), lambda b, g, t: (0, wcb + g)),
                  pl.BlockSpec((1, SUBLANES, LANES), g3),
                  pl.BlockSpec((1, SUBLANES, LANES), g3),
                  pl.BlockSpec((1, gw), lambda b, g, t: (0, g)),
                  pl.BlockSpec((1, gw), lambda b, g, t: (0, g)),
                  _resident(e_mat.shape)],
        out_specs=pl.BlockSpec((ts, gw), lambda b, g, t: (row(b, t), g)),
        out_shape=jax.ShapeDtypeStruct((bsz * seq, d_inner), BF16),
        scratch_shapes=[pltpu.VMEM((ts + SUBLANES, gw), F32), pltpu.VMEM((ts + SUBLANES, n), F32),
                        pltpu.VMEM((ts + SUBLANES, n), F32), pltpu.VMEM((ts, gw), F32),
                        pltpu.VMEM((ts, n), BF16), pltpu.VMEM((ts, n), BF16),
                        pltpu.VMEM((nck * M_HPG, CHUNK, CHUNK + n), BF16),
                        pltpu.VMEM((ts, gw), BF16), pltpu.VMEM((nck, n, gw), BF16),
                        pltpu.VMEM((max(nck, SUBLANES), gw), F32), pltpu.VMEM((n, gw), F32)],
        compiler_params=_params(3), name="ssd",
    )(proj, proj, proj, proj, dt_t, cw2, cb2, cw2, cb2, cw2, cb2,
      per_head(dt_bias), per_head(a_log), drow, norm_g.reshape(1, d_inner), e_mat)


def _post_kernel(*refs, n_mix, final_norm, ff_chunk):
    mix_refs = refs[:n_mix]
    wmix_refs = refs[n_mix:2 * n_mix]
    (x_ref, gq_ref, wq_ref, k_ref, v_ref, wo_ref, gf_ref, w1_ref, w2_ref, gfin_ref,
     o_ref) = refs[2 * n_mix:]
    d = x_ref.shape[1]
    hd = d // X_HEADS

    x = x_ref[...]
    for m_ref, w_ref in zip(mix_refs, wmix_refs):
        x = x + _dot(m_ref[...], w_ref[...])

    q = _dot(_rms(x, gq_ref[...]).astype(BF16), wq_ref[...]).astype(BF16)
    heads = []
    for h in range(X_HEADS):
        sc = lax.dot_general(q[:, h * hd:(h + 1) * hd], k_ref[0, :, h * hd:(h + 1) * hd], NT_DIMS,
                             preferred_element_type=F32) * (hd ** -0.5)
        p = jnp.exp(sc - jnp.max(sc, axis=-1, keepdims=True))
        p = p / jnp.sum(p, axis=-1, keepdims=True)
        heads.append(_dot(p.astype(BF16), v_ref[0, :, h * hd:(h + 1) * hd]))
    x = x + _dot(jnp.concatenate(heads, axis=-1).astype(BF16), wo_ref[...])

    hn = _rms(x, gf_ref[...]).astype(BF16)
    y = x
    for c in range(0, w1_ref.shape[1], ff_chunk):
        hid = jnp.square(jnp.maximum(_dot(hn, w1_ref[:, c:c + ff_chunk]), 0.0))
        y = y + _dot(hid.astype(BF16), w2_ref[c:c + ff_chunk, :])
    if final_norm:
        y = _rms(y, gfin_ref[...])
    o_ref[...] = y


def _post(mixes, wmixes, x2d, gq, wq, kv, wo, gf, w1, w2, gfin, final_norm, bsz, seq,
          tm=TOKEN_TILE, ff_chunk=1024):
    t, d = x2d.shape
    nt = seq // tm
    mem_len = kv.shape[1]
    tile = lambda w: pl.BlockSpec((tm, w), lambda b, i: (b * nt + i, 0))
    in_specs = [tile(m.shape[1]) for m in mixes] + [_resident(w.shape) for w in wmixes]
    in_specs += [tile(d), _resident((1, d)), _resident(wq.shape),
                 pl.BlockSpec((1, mem_len, d), lambda b, i: (b, 0, 0)),
                 pl.BlockSpec((1, mem_len, d), lambda b, i: (b, 0, 1)),
                 _resident(wo.shape), _resident((1, d)), _resident(w1.shape),
                 _resident(w2.shape), _resident((1, d))]
    return pl.pallas_call(
        functools.partial(_post_kernel, n_mix=len(mixes), final_norm=final_norm,
                          ff_chunk=ff_chunk),
        grid=(bsz, nt), in_specs=in_specs, out_specs=tile(d),
        out_shape=jax.ShapeDtypeStruct((t, d), F32),
        compiler_params=_params(2), name="post",
    )(*mixes, *wmixes, x2d, gq.reshape(1, d), wq, kv, kv, wo, gf.reshape(1, d), w1, w2,
      gfin.reshape(1, d))


def kernel(x, mem, norm_mix_g, norm_mem_q_g, norm_mem_kv_g, norm_ffn_g, final_norm_g, e_w_in, a_conv_w, a_conv_b, a_gate_r_w, a_gate_r_b, a_gate_i_w, a_gate_i_b, a_lambda, b_lb_logits, b_norm_g, e_w_out, o_w_in, m_conv_w, m_conv_b, m_dt_bias, m_a_log, m_d, m_norm_g, o_w_out, xq_w, xk_w, xv_w, xo_w, ffn_w1, ffn_w2):
    bsz, seq, d = x.shape
    mem_len = mem.shape[1]
    depth = norm_mix_g.shape[0]
    t = bsz * seq
    bf = lambda w: w.astype(BF16)
    xs = x.reshape(t, d)
    mem2d = mem.reshape(bsz * mem_len, d)

    for l in range(depth):
        (kv,) = _inproj(mem2d, norm_mem_kv_g[l],
                        [bf(jnp.concatenate([xk_w[l], xv_w[l]], axis=1))], ["nn"], [BF16])
        kv = kv.reshape(bsz, mem_len, 2 * d)
        if l % 2 == 0:
            e = l // 2
            w = e_w_in[e]
            wa = d
            sec = lambda k: w[:, k * wa:(k + 1) * wa]
            w_main = bf(jnp.concatenate([sec(0), sec(1), sec(2), sec(4), sec(5)], axis=1))
            proj, fproj = _inproj(xs, norm_mix_g[l], [w_main, bf(sec(3))], ["nn", "nn"],
                                  [BF16, F32])
            wg = bf(0.5 * jnp.concatenate([a_gate_r_w[e], a_gate_i_w[e]], axis=-1))
            ya = _rglru(proj, a_conv_w[e], a_conv_b[e], wg, 0.5 * a_gate_r_b[e],
                        0.5 * a_gate_i_b[e], a_lambda[e], bsz, seq)
            blk = wa // LANES
            yb = _hgrn2(proj, fproj, b_lb_logits, b_norm_g[e], l, bsz, seq,
                        (2 * blk, 3 * blk, 4 * blk))
            mixes = [ya, yb]
            wmixes = [bf(e_w_out[e][:wa]), bf(e_w_out[e][wa:])]
        else:
            o = l // 2
            w = o_w_in[o]
            n_main = w.shape[1] - M_GROUPS * M_HPG
            w_dt = w[:, n_main:].T.reshape(M_GROUPS, M_HPG, d)
            w_dt = jnp.pad(w_dt, ((0, 0), (0, SUBLANES - M_HPG), (0, 0))).reshape(-1, d)
            proj, dt_t = _inproj(xs, norm_mix_g[l], [bf(w[:, :n_main]), bf(w_dt)], ["nn", "nt"],
                                 [BF16, F32])
            dt_t = dt_t.reshape(M_GROUPS, SUBLANES, t // CHUNK, CHUNK).transpose(0, 2, 1, 3)
            ym = _ssd(proj, dt_t, m_conv_w[o], m_conv_b[o], m_dt_bias[o], m_a_log[o], m_d[o],
                      m_norm_g[o], bsz, seq)
            mixes = [ym]
            wmixes = [bf(o_w_out[o])]
        xs = _post(mixes, wmixes, xs, norm_mem_q_g[l], bf(xq_w[l]), kv, bf(xo_w[l]),
                   norm_ffn_g[l], bf(ffn_w1[l]), bf(ffn_w2[l]), final_norm_g,
                   l == depth - 1, bsz, seq)
    return xs.reshape(bsz, seq, d)
```

```python
import functools

import jax
import jax.numpy as jnp
import numpy as np
from jax import lax
from jax.experimental import pallas as pl
from jax.experimental.pallas import tpu as pltpu

F32 = jnp.float32
BF16 = jnp.bfloat16

NORM_EPS = 1e-6
LOG2_E = 1.4426950408889634
CONV_W = 4
LANES = 128
SUBLANES = 8
VMEM_LIMIT_BYTES = 56 * 1024 * 1024

A_HEADS = 8
LRU_C = 8.0
B_HEADS = 8
M_HEAD_DIM = 64
M_GROUPS = 8
M_HPG = 4
D_STATE = 128
X_HEADS = 4
CHUNK = 128

TOKEN_TILE = 512
LRU_TILE = 256
MIX_TILE = 1024
HGRN2_HEADS_PER_STEP = 4
SSD_GROUPS_PER_STEP = 4

NT_DIMS = (((1,), (1,)), ((), ()))
TN_DIMS = (((0,), (0,)), ((), ()))


def _params(n_axes):
    return pltpu.CompilerParams(
        dimension_semantics=("arbitrary",) * n_axes, vmem_limit_bytes=VMEM_LIMIT_BYTES)


def _resident(shape):
    nd = len(shape)
    return pl.BlockSpec(shape, lambda *_: (0,) * nd, pipeline_mode=pl.Buffered(1))


def _rms(x, g):
    return x * lax.rsqrt(jnp.mean(x * x, axis=-1, keepdims=True) + NORM_EPS) * g


def _sigmoid(x):
    return 0.5 * jnp.tanh(0.5 * x) + 0.5


def _silu(x):
    h = 0.5 * x
    return h * jnp.tanh(h) + h


def _softplus(x):
    return jnp.maximum(x, 0.0) + jnp.log1p(jnp.exp(-jnp.abs(x)))


def _dot(a, b):
    return jnp.dot(a, b, preferred_element_type=F32)


def _dot_f32(a, b):
    return jnp.dot(a, b, preferred_element_type=F32, precision=lax.Precision.HIGHEST)


def _split3(x):
    hi = x.astype(BF16)
    r = x - hi.astype(F32)
    mid = r.astype(BF16)
    lo = (r - mid.astype(F32)).astype(BF16)
    return hi, mid, lo


def _inproj_kernel(x_ref, g_ref, *refs, kinds, col_chunk):
    n = len(kinds)
    w_refs, o_refs = refs[:n], refs[n:]
    xn = _rms(x_ref[...], g_ref[...]).astype(BF16)
    for kind, w_ref, o_ref in zip(kinds, w_refs, o_refs):
        if kind == "nn":
            for c in range(0, w_ref.shape[1], col_chunk):
                o_ref[:, c:c + col_chunk] = _dot(xn, w_ref[:, c:c + col_chunk]).astype(o_ref.dtype)
        else:
            o_ref[...] = lax.dot_general(w_ref[...], xn, NT_DIMS,
                                         preferred_element_type=F32).astype(o_ref.dtype)


def _inproj(x2d, g, weights, kinds, out_dtypes, tm=TOKEN_TILE, col_chunk=1024):
    t, d = x2d.shape
    tm = min(tm, t)
    assert t % tm == 0
    in_specs = [pl.BlockSpec((tm, d), lambda i: (i, 0)), _resident((1, d))]
    out_specs, out_shapes = [], []
    for w, kind, dt in zip(weights, kinds, out_dtypes):
        in_specs.append(_resident(w.shape))
        if kind == "nn":
            out_specs.append(pl.BlockSpec((tm, w.shape[1]), lambda i: (i, 0)))
            out_shapes.append(jax.ShapeDtypeStruct((t, w.shape[1]), dt))
        else:
            out_specs.append(pl.BlockSpec((w.shape[0], tm), lambda i: (0, i)))
            out_shapes.append(jax.ShapeDtypeStruct((w.shape[0], t), dt))
    return pl.pallas_call(
        functools.partial(_inproj_kernel, kinds=tuple(kinds), col_chunk=col_chunk),
        grid=(t // tm,), in_specs=in_specs, out_specs=out_specs, out_shape=out_shapes,
        compiler_params=_params(1), name="inproj",
    )(x2d, g.reshape(1, d), *weights)


def _causal_conv(x, buf, w_ref, b_ref):
    ts = x.shape[0]
    buf[SUBLANES:SUBLANES + ts, :] = x
    y = b_ref[...] + w_ref[CONV_W - 1:CONV_W, :] * x
    for k in range(CONV_W - 1):
        off = SUBLANES - (CONV_W - 1) + k
        y = y + w_ref[k:k + 1, :] * buf[off:off + ts, :]
    buf[0:SUBLANES, :] = x[ts - SUBLANES:ts, :]
    return y


def _conv_silu(src_ref, buf, w_ref, b_ref, dst_ref, rows=CHUNK):
    ts = src_ref.shape[0]
    buf[SUBLANES:SUBLANES + ts, :] = src_ref[...].astype(F32)
    for r0 in range(0, ts, rows):
        y = b_ref[...] + w_ref[CONV_W - 1:CONV_W, :] * buf[SUBLANES + r0:SUBLANES + r0 + rows, :]
        for k in range(CONV_W - 1):
            off = SUBLANES - (CONV_W - 1) + k + r0
            y = y + w_ref[k:k + 1, :] * buf[off:off + rows, :]
        dst_ref[r0:r0 + rows, :] = _silu(y).astype(dst_ref.dtype)
    buf[0:SUBLANES, :] = buf[ts:ts + SUBLANES, :]


def _segment_permutation(ts):
    seg = ts // SUBLANES
    p = np.zeros((ts, ts), np.float32)
    for j in range(seg):
        for s in range(SUBLANES):
            p[SUBLANES * j + s, seg * s + j] = 1.0
    return p


def _rglru_kernel(xa_ref, ga_ref, perm_ref, cw_ref, cb_ref, wg_ref, br_ref, bi_ref, lam_ref, o_ref,
                  xe_s, a_s, u_s, h_s, ap_s, tail, hc):
    ts, width = xa_ref.shape
    hd = width // A_HEADS
    seg = ts // SUBLANES
    halo = (CONV_W - 1) * SUBLANES

    @pl.when(pl.program_id(1) == 0)
    def _():
        tail[...] = jnp.zeros_like(tail)
        hc[...] = jnp.zeros_like(hc)

    perm = perm_ref[...]
    xa = _dot(perm, xa_ref[...])
    ga = _dot(perm, ga_ref[...])

    first_sublane = lax.broadcasted_iota(jnp.int32, (SUBLANES, width), 0) == 0
    for i in range(CONV_W - 1):
        blk = xa[ts - halo + i * SUBLANES:ts - halo + (i + 1) * SUBLANES, :]
        xe_s[i * SUBLANES:(i + 1) * SUBLANES, :] = jnp.where(
            first_sublane, tail[i:i + 1, :], pltpu.roll(blk, 1, 0))
        tail[i:i + 1, :] = blk[SUBLANES - 1:SUBLANES, :]
    xe_s[halo:halo + ts, :] = xa
    conv = cb_ref[...] + cw_ref[CONV_W - 1:CONV_W, :] * xa
    for k in range(CONV_W - 1):
        conv = conv + cw_ref[k:k + 1, :] * xe_s[k * SUBLANES:k * SUBLANES + ts, :]

    convb = conv.astype(BF16)
    r_pre, i_pre = [], []
    for h in range(A_HEADS):
        gate = _dot(convb[:, h * hd:(h + 1) * hd], wg_ref[h])
        r_pre.append(gate[:, :hd])
        i_pre.append(gate[:, hd:])
    tr = jnp.tanh(jnp.concatenate(r_pre, axis=-1) + br_ref[...])
    ti = jnp.tanh(jnp.concatenate(i_pre, axis=-1) + bi_ref[...])
    half_c = (-0.5 * LRU_C) * _softplus(-lam_ref[...])
    a = jnp.exp(half_c * tr + half_c)
    a_s[...] = a
    v = 1.0 - a * a
    half_conv = 0.5 * conv
    u_s[...] = (v * lax.rsqrt(jnp.maximum(v, 1e-30))) * (half_conv * ti + half_conv)

    h = jnp.zeros((SUBLANES, width), F32)
    ap = jnp.ones((SUBLANES, width), F32)
    for j in range(seg):
        rows = slice(j * SUBLANES, (j + 1) * SUBLANES)
        aj = a_s[rows, :]
        h = aj * h + u_s[rows, :]
        ap = aj * ap
        h_s[rows, :] = h
        ap_s[rows, :] = ap

    sub = lax.broadcasted_iota(jnp.int32, (SUBLANES, width), 0)
    for sh in (1, 2, 4):
        keep = sub >= sh
        h = jnp.where(keep, ap * pltpu.roll(h, sh, 0) + h, h)
        ap = jnp.where(keep, ap * pltpu.roll(ap, sh, 0), ap)
    leave = h + ap * hc[0:1, :]
    enter = jnp.where(sub == 0, hc[0:1, :], pltpu.roll(leave, 1, 0))
    hc[0:1, :] = leave[SUBLANES - 1:SUBLANES, :]

    enter_t = jnp.concatenate([enter] * seg, axis=0)
    y = ((h_s[...] + ap_s[...] * enter_t) * jax.nn.gelu(ga)).astype(BF16)
    o_ref[...] = lax.dot_general(perm, y, TN_DIMS, preferred_element_type=F32).astype(o_ref.dtype)


def _rglru(proj, conv_w, conv_b, wg, b_r, b_i, lam, bsz, seq, ts=LRU_TILE):
    width = conv_w.shape[1]
    nt = seq // ts
    row = lambda b, t: b * nt + t
    vec = lambda: _resident((1, width))
    perm = jnp.asarray(_segment_permutation(ts), BF16)
    tile = lambda: pltpu.VMEM((ts, width), F32)
    return pl.pallas_call(
        _rglru_kernel,
        grid=(bsz, nt),
        in_specs=[pl.BlockSpec((ts, width), lambda b, t: (row(b, t), 0)),
                  pl.BlockSpec((ts, width), lambda b, t: (row(b, t), 1)),
                  _resident(perm.shape), _resident(conv_w.shape), vec(), _resident(wg.shape),
                  vec(), vec(), vec()],
        out_specs=pl.BlockSpec((ts, width), lambda b, t: (row(b, t), 0)),
        out_shape=jax.ShapeDtypeStruct((bsz * seq, width), BF16),
        scratch_shapes=[pltpu.VMEM((ts + (CONV_W - 1) * SUBLANES, width), F32),
                        tile(), tile(), tile(), tile(),
                        pltpu.VMEM((SUBLANES, width), F32), pltpu.VMEM((SUBLANES, width), F32)],
        compiler_params=_params(2), name="rglru",
    )(proj, proj, perm, conv_w, conv_b.reshape(1, width), wg, b_r.reshape(1, width),
      b_i.reshape(1, width), lam.reshape(1, width))


def _pair_level(n):
    t = lax.broadcasted_iota(jnp.int32, (n, n), 0)
    s = lax.broadcasted_iota(jnp.int32, (n, n), 1)
    x = t ^ s
    lvl = jnp.zeros((n, n), jnp.int32)
    w = 1
    while w < n:
        lvl = lvl + (x >= w).astype(jnp.int32)
        w *= 2
    return jnp.where(s > t, -1, lvl)


def _hgrn2_kernel(q_ref, f_ref, v_ref, g_ref, lbl_ref, ng_ref, o_ref, st, bc_s, *, layer):
    ts, dk = q_ref.shape[0], LANES
    c = CHUNK
    nck = ts // c
    heads = q_ref.shape[1] // dk

    @pl.when(pl.program_id(2) == 0)
    def _():
        st[...] = jnp.zeros_like(st)

    lg = lbl_ref[...]
    e = jnp.exp(lg - jnp.max(lg, axis=0, keepdims=True))
    lb = jnp.sum(e[0:layer + 1, :], axis=0, keepdims=True) / jnp.sum(e, axis=0, keepdims=True)

    row = lax.broadcasted_iota(jnp.int32, (c, dk), 0)
    tt = lax.broadcasted_iota(jnp.int32, (c, c), 0)
    ss = lax.broadcasted_iota(jnp.int32, (c, c), 1)
    tri = (ss <= tt).astype(BF16)
    lvl = _pair_level(c)
    n_lvl = c.bit_length()
    lvl_b = lvl.astype(F32).astype(BF16)
    right_b = [None] + [((row >> (p - 1)) & 1).astype(F32).astype(BF16) for p in range(1, n_lvl)]
    side = [None] + [jnp.where((row & (1 << (p - 1))) != 0, 1.0, -1.0) for p in range(1, n_lvl)]
    r4 = row & 3

    for ci, hh in [(ci, hh) for ci in range(nck) for hh in range(heads)]:
        r0 = ci * c
        cols = slice(hh * dk, (hh + 1) * dk)
        lbh = lb[:, cols]
        qf = _silu(q_ref[r0:r0 + c, cols].astype(F32))
        f = lbh + (1.0 - lbh) * _sigmoid(f_ref[r0:r0 + c, cols])
        kk = 1.0 - f
        qb, kb = qf.astype(BF16), kk.astype(BF16)
        bc = LOG2_E * sum(_dot(tri, part) for part in _split3(jnp.log(f)))
        bc_s[hh * nck + ci] = bc
        vb = v_ref[r0:r0 + c, cols]

        scores = jnp.where(lvl_b == 0,
                           lax.dot_general(qb, kb, NT_DIMS,
                                           preferred_element_type=F32).astype(BF16), 0)
        for p in range(1, n_lvl):
            w = 1 << (p - 1)
            if w == 1:
                dec = jnp.where(right_b[p] > 0, f.astype(BF16), 1)
            elif w == 2:
                dec = jnp.where(r4 == 0, pltpu.roll(f, c - 1, 0),
                                jnp.where(r4 == 1, 1.0,
                                          jnp.where(r4 == 2, f, f * pltpu.roll(f, 1, 0)))
                                ).astype(BF16)
            else:
                gat = jnp.concatenate(
                    [jnp.broadcast_to(bc_s[hh * nck + ci, b * 2 * w + w - 1:b * 2 * w + w, :],
                                      (2 * w, dk)) for b in range(c // (2 * w))], axis=0)
                dec = jnp.exp2((bc - gat) * side[p]).astype(BF16)
            z = jnp.where(right_b[p] > 0, qb, kb) * dec
            scores = jnp.where(lvl_b == p,
                               lax.dot_general(z, z, NT_DIMS,
                                               preferred_element_type=F32).astype(BF16), scores)
        o = _dot(scores, vb)

        b_last = bc[c - 1:c, :]
        o = o + lax.dot_general((qf * jnp.exp2(bc)).astype(BF16), st[hh].astype(BF16), NT_DIMS,
                                preferred_element_type=F32)
        kd = (kk * jnp.exp2(b_last - bc)).astype(BF16)
        st[hh] = st[hh] * jnp.exp2(b_last) + lax.dot_general(vb, kd, TN_DIMS,
                                                             preferred_element_type=F32)

        o = o * lax.rsqrt(jnp.mean(o * o, axis=-1, keepdims=True) + NORM_EPS)
        gv = g_ref[r0:r0 + c, cols].astype(F32)
        o_ref[r0:r0 + c, cols] = (o * ng_ref[:, cols] * _silu(gv)).astype(o_ref.dtype)


def _hgrn2(proj, fproj, lb_logits, norm_g, layer, bsz, seq, col0, ts=MIX_TILE,
           heads=HGRN2_HEADS_PER_STEP):
    dk = LANES
    wd = heads * dk
    width = fproj.shape[1]
    nt = seq // ts
    q0, v0, g0 = (o // heads for o in col0)
    row = lambda b, t: b * nt + t
    n_layers = lb_logits.shape[0]
    return pl.pallas_call(
        functools.partial(_hgrn2_kernel, layer=layer),
        grid=(bsz, B_HEADS // heads, nt),
        in_specs=[pl.BlockSpec((ts, wd), lambda b, h, t: (row(b, t), q0 + h)),
                  pl.BlockSpec((ts, wd), lambda b, h, t: (row(b, t), h)),
                  pl.BlockSpec((ts, wd), lambda b, h, t: (row(b, t), v0 + h)),
                  pl.BlockSpec((ts, wd), lambda b, h, t: (row(b, t), g0 + h)),
                  pl.BlockSpec((n_layers, wd), lambda b, h, t: (0, h)),
                  pl.BlockSpec((1, wd), lambda b, h, t: (0, h))],
        out_specs=pl.BlockSpec((ts, wd), lambda b, h, t: (row(b, t), h)),
        out_shape=jax.ShapeDtypeStruct((bsz * seq, width), BF16),
        scratch_shapes=[pltpu.VMEM((heads, dk, dk), F32),
                        pltpu.VMEM((heads * (ts // CHUNK), CHUNK, dk), F32)],
        compiler_params=_params(3), name="hgrn2",
    )(proj, fproj, proj, proj, lb_logits, norm_g.reshape(1, width))


def _ssd_expand_matrices(nck):
    e = np.zeros((nck, LANES, M_HPG * LANES + M_HPG * M_HEAD_DIM), np.float32)
    for c in range(nck):
        for r in range(M_HPG):
            e[c, SUBLANES * c + r, r * LANES:(r + 1) * LANES] = 1.0
            e[c, SUBLANES * (nck + c) + r,
              M_HPG * LANES + r * M_HEAD_DIM:M_HPG * LANES + (r + 1) * M_HEAD_DIM] = 1.0
    return e


def _ssd_kernel(*refs, groups):
    n_in = 16
    (z, x, b, c, dt, cwx, cbx, cwb, cbb, cwc, cbc, bias, alog, drow, ng, e_ref) = refs[:n_in]
    o_ref, scratch = refs[n_in], refs[n_in + 1:]
    gw, n = x.shape[1] // groups, b.shape[1] // groups
    for gi in range(groups):
        wide = (slice(None), slice(gi * gw, (gi + 1) * gw))
        narrow = (slice(None), slice(gi * n, (gi + 1) * n))
        _ssd_group(z.at[wide], x.at[wide], b.at[narrow], c.at[narrow], dt.at[gi], cwx.at[wide],
                   cbx.at[wide], cwb.at[narrow], cbb.at[narrow], cwc.at[narrow], cbc.at[narrow],
                   bias.at[gi], alog.at[gi], drow.at[wide], ng.at[wide], e_ref, o_ref.at[wide],
                   *[sc.at[gi] for sc in scratch])


def _ssd_group(z_ref, x_ref, b_ref, c_ref, dt_ref, cwx_ref, cbx_ref, cwb_ref, cbb_ref,
               cwc_ref, cbc_ref, bias_ref, alog_ref, drow_ref, ng_ref, e_ref, o_ref,
               xbuf, bbuf, cbuf, xc_s, bc_s, cc_s, lhs_s, xw_s, sb_s, dec_s, st):
    ts, gw = x_ref.shape
    n = D_STATE
    c = CHUNK
    nck = ts // c
    nb = M_HPG * LANES

    @pl.when(pl.program_id(2) == 0)
    def _():
        xbuf[0:SUBLANES, :] = jnp.zeros((SUBLANES, gw), F32)
        bbuf[0:SUBLANES, :] = jnp.zeros((SUBLANES, n), F32)
        cbuf[0:SUBLANES, :] = jnp.zeros((SUBLANES, n), F32)
        st[...] = jnp.zeros_like(st)

    _conv_silu(x_ref, xbuf, cwx_ref, cbx_ref, xc_s)
    _conv_silu(b_ref, bbuf, cwb_ref, cbb_ref, bc_s)
    _conv_silu(c_ref, cbuf, cwc_ref, cbc_ref, cc_s)

    tt = lax.broadcasted_iota(jnp.int32, (c, c), 0)
    ss = lax.broadcasted_iota(jnp.int32, (c, c), 1)
    causal = ss <= tt
    upper = (tt <= ss).astype(F32)
    low_half = lax.broadcasted_iota(jnp.int32, (1, LANES), 1) < M_HEAD_DIM
    low_lanes = lax.broadcasted_iota(jnp.int32, (c + D_STATE, LANES), 1) < M_HEAD_DIM

    bias = jnp.concatenate([bias_ref[...]] * nck, axis=0)
    neg_a = jnp.concatenate([-jnp.exp(alog_ref[...])] * nck, axis=0)
    dt = _softplus(dt_ref[...].reshape(nck * SUBLANES, c) + bias)
    a_row = LOG2_E * _dot_f32(dt * neg_a, upper)
    dtw = dt * jnp.exp2(a_row[:, c - 1:c] - a_row)
    src_row = a_row - LOG2_E * jnp.log(dt)
    rows = [a_row, dtw]
    if 2 * nck * SUBLANES < c:
        rows.append(jnp.zeros((c - 2 * nck * SUBLANES, c), F32))
    hi, mid, lo = _split3(jnp.concatenate(rows, axis=0).T)

    for ci in range(nck):
        r0 = ci * c
        e_c = e_ref[ci]
        ex = _dot(hi, e_c)
        a_col = ex[:, :nb] + _dot(mid, e_c[:, :nb]) + _dot(lo, e_c[:, :nb])
        xw_s[r0:r0 + c, :] = (xc_s[r0:r0 + c, :] * ex[:, nb:]).astype(BF16)
        cm = cc_s[r0:r0 + c, :]
        cmf = cm.astype(F32)
        cb = lax.dot_general(cm, bc_s[r0:r0 + c, :], NT_DIMS, preferred_element_type=F32)
        for r in range(M_HPG):
            a_t = a_col[:, r * LANES:(r + 1) * LANES]
            src = src_row[ci * SUBLANES + r:ci * SUBLANES + r + 1, :]
            lmat = jnp.exp2(jnp.where(causal, a_t - src, -1e30))
            lhs_s[ci * M_HPG + r] = jnp.concatenate(
                [(cb * lmat).astype(BF16), (cmf * jnp.exp2(a_t)).astype(BF16)], axis=1)
        e_last = [jnp.exp2(a_col[c - 1:c, r * LANES:(r + 1) * LANES]) for r in range(M_HPG)]
        dec_s[ci:ci + 1, :] = jnp.concatenate([jnp.where(low_half, e_last[0], e_last[1]),
                                               jnp.where(low_half, e_last[2], e_last[3])], axis=1)

    for ci in range(nck):
        r0 = ci * c
        sb_s[ci] = st[...].astype(BF16)
        st[...] = st[...] * dec_s[ci:ci + 1, :] + lax.dot_general(
            bc_s[r0:r0 + c, :], xw_s[r0:r0 + c, :], TN_DIMS, preferred_element_type=F32)

    for ci in range(nck):
        r0 = ci * c
        xc = xc_s[r0:r0 + c, :]
        rhs = jnp.concatenate([xc.astype(BF16), sb_s[ci]], axis=0)
        tiles = []
        for j in range(gw // LANES):
            rt = rhs[:, j * LANES:(j + 1) * LANES]
            tiles.append(_dot(lhs_s[ci * M_HPG + 2 * j], jnp.where(low_lanes, rt, 0))
                         + _dot(lhs_s[ci * M_HPG + 2 * j + 1], jnp.where(low_lanes, 0, rt)))
        y = drow_ref[...] * xc + jnp.concatenate(tiles, axis=1)
        y = y * _silu(z_ref[r0:r0 + c, :].astype(F32))
        y = y * lax.rsqrt(jnp.mean(y * y, axis=-1, keepdims=True) + NORM_EPS)
        o_ref[r0:r0 + c, :] = (y * ng_ref[...]).astype(o_ref.dtype)


def _ssd(proj, dt_t, conv_w, conv_b, dt_bias, a_log, d_skip, norm_g, bsz, seq, ts=MIX_TILE,
         groups=SSD_GROUPS_PER_STEP):
    gw = M_HPG * M_HEAD_DIM
    n = D_STATE
    d_inner = M_GROUPS * gw
    nt = seq // ts
    nck = ts // CHUNK
    row = lambda b, t: b * nt + t
    gp = groups
    zb, xb = 0, d_inner // (gw * gp)
    bb, cb = 2 * d_inner // (n * gp), (2 * d_inner // n + M_GROUPS) // gp
    wbb, wcb = d_inner // (n * gp), (d_inner // n + M_GROUPS) // gp
    cw2 = conv_w.reshape(CONV_W, -1)
    cb2 = conv_b.reshape(1, -1)

    def per_head(v):
        v = jnp.pad(v.reshape(M_GROUPS, M_HPG), ((0, 0), (0, SUBLANES - M_HPG)))
        return jnp.broadcast_to(v[:, :, None], (M_GROUPS, SUBLANES, LANES)).astype(F32)

    drow = jnp.repeat(d_skip.astype(F32), M_HEAD_DIM).reshape(1, d_inner)
    e_mat = jnp.asarray(_ssd_expand_matrices(nck), BF16)
    g3 = lambda b, g, t: (g, 0, 0)
    return pl.pallas_call(
        functools.partial(_ssd_kernel, groups=gp),
        grid=(bsz, M_GROUPS // gp, nt),
        in_specs=[pl.BlockSpec((ts, gp * gw), lambda b, g, t: (row(b, t), zb + g)),
                  pl.BlockSpec((ts, gp * gw), lambda b, g, t: (row(b, t), xb + g)),
                  pl.BlockSpec((ts, gp * n), lambda b, g, t: (row(b, t), bb + g)),
                  pl.BlockSpec((ts, gp * n), lambda b, g, t: (row(b, t), cb + g)),
                  pl.BlockSpec((gp, nck, SUBLANES, LANES), lambda b, g, t: (g, row(b, t), 0, 0)),
                  pl.BlockSpec((CONV_W, gp * gw), lambda b, g, t: (0, g)),
                  pl.BlockSpec((1, gp * gw), lambda b, g, t: (0, g)),
                  pl.BlockSpec((CONV_W, gp * n), lambda b, g, t: (0, wbb + g)),
                  pl.BlockSpec((1, gp * n), lambda b, g, t: (0, wbb + g)),
                  pl.BlockSpec((CONV_W, gp * n), lambda b, g, t: (0, wcb + g)),
                  pl.BlockSpec((1, gp * n), lambda b, g, t: (0, wcb + g)),
                  pl.BlockSpec((gp, SUBLANES, LANES), g3),
                  pl.BlockSpec((gp, SUBLANES, LANES), g3),
                  pl.BlockSpec((1, gp * gw), lambda b, g, t: (0, g)),
                  pl.BlockSpec((1, gp * gw), lambda b, g, t: (0, g)),
                  _resident(e_mat.shape)],
        out_specs=pl.BlockSpec((ts, gp * gw), lambda b, g, t: (row(b, t), g)),
        out_shape=jax.ShapeDtypeStruct((bsz * seq, d_inner), BF16),
        scratch_shapes=[pltpu.VMEM((gp,) + shape, dt) for shape, dt in [
            ((ts + SUBLANES, gw), F32), ((ts + SUBLANES, n), F32), ((ts + SUBLANES, n), F32),
            ((ts, gw), F32), ((ts, n), BF16), ((ts, n), BF16),
            ((nck * M_HPG, CHUNK, CHUNK + n), BF16), ((ts, gw), BF16), ((nck, n, gw), BF16),
            ((max(nck, SUBLANES), gw), F32), ((n, gw), F32)]],
        compiler_params=_params(3), name="ssd",
    )(proj, proj, proj, proj, dt_t, cw2, cb2, cw2, cb2, cw2, cb2,
      per_head(dt_bias), per_head(a_log), drow, norm_g.reshape(1, d_inner), e_mat)


def _post_kernel(*refs, n_mix, final_norm, ff_chunk):
    mix_refs = refs[:n_mix]
    wmix_refs = refs[n_mix:2 * n_mix]
    (x_ref, gq_ref, wq_ref, k_ref, v_ref, wo_ref, gf_ref, w1_ref, w2_ref, gfin_ref,
     o_ref) = refs[2 * n_mix:]
    d = x_ref.shape[1]
    hd = d // X_HEADS

    x = x_ref[...]
    for m_ref, w_ref in zip(mix_refs, wmix_refs):
        x = x + _dot(m_ref[...], w_ref[...])

    q = _dot(_rms(x, gq_ref[...]).astype(BF16), wq_ref[...]).astype(BF16)
    heads = []
    for h in range(X_HEADS):
        sc = lax.dot_general(q[:, h * hd:(h + 1) * hd], k_ref[0, :, h * hd:(h + 1) * hd], NT_DIMS,
                             preferred_element_type=F32) * (hd ** -0.5)
        p = jnp.exp(sc - jnp.max(sc, axis=-1, keepdims=True))
        p = p / jnp.sum(p, axis=-1, keepdims=True)
        heads.append(_dot(p.astype(BF16), v_ref[0, :, h * hd:(h + 1) * hd]))
    x = x + _dot(jnp.concatenate(heads, axis=-1).astype(BF16), wo_ref[...])

    hn = _rms(x, gf_ref[...]).astype(BF16)
    y = x
    for c in range(0, w1_ref.shape[1], ff_chunk):
        hid = jnp.square(jnp.maximum(_dot(hn, w1_ref[:, c:c + ff_chunk]), 0.0))
        y = y + _dot(hid.astype(BF16), w2_ref[c:c + ff_chunk, :])
    if final_norm:
        y = _rms(y, gfin_ref[...])
    o_ref[...] = y


def _post(mixes, wmixes, x2d, gq, wq, kv, wo, gf, w1, w2, gfin, final_norm, bsz, seq,
          tm=TOKEN_TILE, ff_chunk=1024):
    t, d = x2d.shape
    nt = seq // tm
    mem_len = kv.shape[1]
    tile = lambda w: pl.BlockSpec((tm, w), lambda b, i: (b * nt + i, 0))
    in_specs = [tile(m.shape[1]) for m in mixes] + [_resident(w.shape) for w in wmixes]
    in_specs += [tile(d), _resident((1, d)), _resident(wq.shape),
                 pl.BlockSpec((1, mem_len, d), lambda b, i: (b, 0, 0)),
                 pl.BlockSpec((1, mem_len, d), lambda b, i: (b, 0, 1)),
                 _resident(wo.shape), _resident((1, d)), _resident(w1.shape),
                 _resident(w2.shape), _resident((1, d))]
    return pl.pallas_call(
        functools.partial(_post_kernel, n_mix=len(mixes), final_norm=final_norm,
                          ff_chunk=ff_chunk),
        grid=(bsz, nt), in_specs=in_specs, out_specs=tile(d),
        out_shape=jax.ShapeDtypeStruct((t, d), F32),
        compiler_params=_params(2), name="post",
    )(*mixes, *wmixes, x2d, gq.reshape(1, d), wq, kv, kv, wo, gf.reshape(1, d), w1, w2,
      gfin.reshape(1, d))


def kernel(x, mem, norm_mix_g, norm_mem_q_g, norm_mem_kv_g, norm_ffn_g, final_norm_g, e_w_in, a_conv_w, a_conv_b, a_gate_r_w, a_gate_r_b, a_gate_i_w, a_gate_i_b, a_lambda, b_lb_logits, b_norm_g, e_w_out, o_w_in, m_conv_w, m_conv_b, m_dt_bias, m_a_log, m_d, m_norm_g, o_w_out, xq_w, xk_w, xv_w, xo_w, ffn_w1, ffn_w2):
    bsz, seq, d = x.shape
    mem_len = mem.shape[1]
    depth = norm_mix_g.shape[0]
    t = bsz * seq
    bf = lambda w: w.astype(BF16)
    xs = x.reshape(t, d)
    mem2d = mem.reshape(bsz * mem_len, d)

    for l in range(depth):
        (kv,) = _inproj(mem2d, norm_mem_kv_g[l],
                        [bf(jnp.concatenate([xk_w[l], xv_w[l]], axis=1))], ["nn"], [BF16])
        kv = kv.reshape(bsz, mem_len, 2 * d)
        if l % 2 == 0:
            e = l // 2
            w = e_w_in[e]
            wa = d
            sec = lambda k: w[:, k * wa:(k + 1) * wa]
            w_main = bf(jnp.concatenate([sec(0), sec(1), sec(2), sec(4), sec(5)], axis=1))
            proj, fproj = _inproj(xs, norm_mix_g[l], [w_main, bf(sec(3))], ["nn", "nn"],
                                  [BF16, F32])
            wg = bf(0.5 * jnp.concatenate([a_gate_r_w[e], a_gate_i_w[e]], axis=-1))
            ya = _rglru(proj, a_conv_w[e], a_conv_b[e], wg, 0.5 * a_gate_r_b[e],
                        0.5 * a_gate_i_b[e], a_lambda[e], bsz, seq)
            blk = wa // LANES
            yb = _hgrn2(proj, fproj, b_lb_logits, b_norm_g[e], l, bsz, seq,
                        (2 * blk, 3 * blk, 4 * blk))
            mixes = [ya, yb]
            wmixes = [bf(e_w_out[e][:wa]), bf(e_w_out[e][wa:])]
        else:
            o = l // 2
            w = o_w_in[o]
            n_main = w.shape[1] - M_GROUPS * M_HPG
            w_dt = w[:, n_main:].T.reshape(M_GROUPS, M_HPG, d)
            w_dt = jnp.pad(w_dt, ((0, 0), (0, SUBLANES - M_HPG), (0, 0))).reshape(-1, d)
            proj, dt_t = _inproj(xs, norm_mix_g[l], [bf(w[:, :n_main]), bf(w_dt)], ["nn", "nt"],
                                 [BF16, F32])
            dt_t = dt_t.reshape(M_GROUPS, SUBLANES, t // CHUNK, CHUNK).transpose(0, 2, 1, 3)
            ym = _ssd(proj, dt_t, m_conv_w[o], m_conv_b[o], m_dt_bias[o], m_a_log[o], m_d[o],
                      m_norm_g[o], bsz, seq)
            mixes = [ym]
            wmixes = [bf(o_w_out[o])]
        xs = _post(mixes, wmixes, xs, norm_mem_q_g[l], bf(xq_w[l]), kv, bf(xo_w[l]),
                   norm_ffn_g[l], bf(ffn_w1[l]), bf(ffn_w2[l]), final_norm_g,
                   l == depth - 1, bsz, seq)
    return xs.reshape(bsz, seq, d)
```

```python
import functools

import jax
import jax.numpy as jnp
import numpy as np
from jax import lax
from jax.experimental import pallas as pl
from jax.experimental.pallas import tpu as pltpu

F32 = jnp.float32
BF16 = jnp.bfloat16

NORM_EPS = 1e-6
LOG2_E = 1.4426950408889634
CONV_W = 4
LANES = 128
SUBLANES = 8
VMEM_LIMIT_BYTES = 56 * 1024 * 1024

A_HEADS = 8
LRU_C = 8.0
B_HEADS = 8
M_HEAD_DIM = 64
M_GROUPS = 8
M_HPG = 4
D_STATE = 128
X_HEADS = 4
CHUNK = 128

TOKEN_TILE = 512
LRU_TILE = 256
MIX_TILE = 1024
HGRN2_HEADS_PER_STEP = 4
SSD_GROUPS_PER_STEP = 4

NT_DIMS = (((1,), (1,)), ((), ()))
TN_DIMS = (((0,), (0,)), ((), ()))


def _params(n_axes):
    return pltpu.CompilerParams(
        dimension_semantics=("arbitrary",) * n_axes, vmem_limit_bytes=VMEM_LIMIT_BYTES)


def _resident(shape):
    nd = len(shape)
    return pl.BlockSpec(shape, lambda *_: (0,) * nd, pipeline_mode=pl.Buffered(1))


def _rms(x, g):
    return x * lax.rsqrt(jnp.mean(x * x, axis=-1, keepdims=True) + NORM_EPS) * g


def _sigmoid(x):
    return 0.5 * jnp.tanh(0.5 * x) + 0.5


def _silu(x):
    h = 0.5 * x
    return h * jnp.tanh(h) + h


def _softplus(x):
    return jnp.maximum(x, 0.0) + jnp.log1p(jnp.exp(-jnp.abs(x)))


def _dot(a, b):
    return jnp.dot(a, b, preferred_element_type=F32)


def _dot_f32(a, b):
    return jnp.dot(a, b, preferred_element_type=F32, precision=lax.Precision.HIGHEST)


def _split3(x):
    hi = x.astype(BF16)
    r = x - hi.astype(F32)
    mid = r.astype(BF16)
    lo = (r - mid.astype(F32)).astype(BF16)
    return hi, mid, lo


def _inproj_kernel(x_ref, g_ref, *refs, kinds, col_chunk):
    n = len(kinds)
    w_refs, o_refs = refs[:n], refs[n:]
    xn = _rms(x_ref[...], g_ref[...]).astype(BF16)
    for kind, w_ref, o_ref in zip(kinds, w_refs, o_refs):
        if kind == "nn":
            for c in range(0, w_ref.shape[1], col_chunk):
                o_ref[:, c:c + col_chunk] = _dot(xn, w_ref[:, c:c + col_chunk]).astype(o_ref.dtype)
        else:
            y = lax.dot_general(w_ref[...], xn, NT_DIMS, preferred_element_type=F32)
            for gi in range(y.shape[0] // SUBLANES):
                for ci in range(y.shape[1] // LANES):
                    o_ref[gi, ci] = y[gi * SUBLANES:(gi + 1) * SUBLANES,
                                      ci * LANES:(ci + 1) * LANES].astype(o_ref.dtype)


def _inproj(x2d, g, weights, kinds, out_dtypes, tm=TOKEN_TILE, col_chunk=1024):
    t, d = x2d.shape
    tm = min(tm, t)
    assert t % tm == 0
    in_specs = [pl.BlockSpec((tm, d), lambda i: (i, 0)), _resident((1, d))]
    out_specs, out_shapes = [], []
    for w, kind, dt in zip(weights, kinds, out_dtypes):
        in_specs.append(_resident(w.shape))
        if kind == "nn":
            out_specs.append(pl.BlockSpec((tm, w.shape[1]), lambda i: (i, 0)))
            out_shapes.append(jax.ShapeDtypeStruct((t, w.shape[1]), dt))
        else:
            rb = w.shape[0] // SUBLANES
            out_specs.append(pl.BlockSpec((rb, tm // LANES, SUBLANES, LANES),
                                          lambda i: (0, i, 0, 0)))
            out_shapes.append(jax.ShapeDtypeStruct((rb, t // LANES, SUBLANES, LANES), dt))
    return pl.pallas_call(
        functools.partial(_inproj_kernel, kinds=tuple(kinds), col_chunk=col_chunk),
        grid=(t // tm,), in_specs=in_specs, out_specs=out_specs, out_shape=out_shapes,
        compiler_params=_params(1), name="inproj",
    )(x2d, g.reshape(1, d), *weights)


def _conv_silu(src_ref, buf, w_ref, b_ref, dst_ref, rows=CHUNK):
    ts = src_ref.shape[0]
    buf[SUBLANES:SUBLANES + ts, :] = src_ref[...].astype(F32)
    for r0 in range(0, ts, rows):
        y = b_ref[...] + w_ref[CONV_W - 1:CONV_W, :] * buf[SUBLANES + r0:SUBLANES + r0 + rows, :]
        for k in range(CONV_W - 1):
            off = SUBLANES - (CONV_W - 1) + k + r0
            y = y + w_ref[k:k + 1, :] * buf[off:off + rows, :]
        dst_ref[r0:r0 + rows, :] = _silu(y).astype(dst_ref.dtype)
    buf[0:SUBLANES, :] = buf[ts:ts + SUBLANES, :]


def _segment_permutation(ts):
    seg = ts // SUBLANES
    p = np.zeros((ts, ts), np.float32)
    for j in range(seg):
        for s in range(SUBLANES):
            p[SUBLANES * j + s, seg * s + j] = 1.0
    return p


def _segment_conv(xa, xe_s, tail, w, b):
    ts, width = xa.shape
    halo = (CONV_W - 1) * SUBLANES
    first_sublane = lax.broadcasted_iota(jnp.int32, (SUBLANES, width), 0) == 0
    for i in range(CONV_W - 1):
        blk = xa[ts - halo + i * SUBLANES:ts - halo + (i + 1) * SUBLANES, :]
        xe_s[i * SUBLANES:(i + 1) * SUBLANES, :] = jnp.where(
            first_sublane, tail[i:i + 1, :], pltpu.roll(blk, 1, 0))
        tail[i:i + 1, :] = blk[SUBLANES - 1:SUBLANES, :]
    xe_s[halo:halo + ts, :] = xa
    conv = b[...] + w[CONV_W - 1:CONV_W, :] * xa
    for k in range(CONV_W - 1):
        conv = conv + w[k:k + 1, :] * xe_s[k * SUBLANES:k * SUBLANES + ts, :]
    return conv


def _rglru_kernel(xa_ref, ga_ref, perm_ref, cw_ref, cb_ref, wg_ref, br_ref, bi_ref, lam_ref, o_ref,
                  xe_s, a_s, u_s, h_s, ap_s, tail, hc):
    ts, width = xa_ref.shape
    hd = width // A_HEADS
    seg = ts // SUBLANES

    @pl.when(pl.program_id(1) == 0)
    def _():
        tail[...] = jnp.zeros_like(tail)
        hc[...] = jnp.zeros_like(hc)

    perm = perm_ref[...]
    xa = _dot(perm, xa_ref[...])
    ga = _dot(perm, ga_ref[...])
    conv = _segment_conv(xa, xe_s, tail, cw_ref, cb_ref)

    convb = conv.astype(BF16)
    r_pre, i_pre = [], []
    for h in range(A_HEADS):
        gate = _dot(convb[:, h * hd:(h + 1) * hd], wg_ref[h])
        r_pre.append(gate[:, :hd])
        i_pre.append(gate[:, hd:])
    tr = jnp.tanh(jnp.concatenate(r_pre, axis=-1) + br_ref[...])
    ti = jnp.tanh(jnp.concatenate(i_pre, axis=-1) + bi_ref[...])
    half_c = (-0.5 * LRU_C) * _softplus(-lam_ref[...])
    a = jnp.exp(half_c * tr + half_c)
    a_s[...] = a
    v = 1.0 - a * a
    half_conv = 0.5 * conv
    u_s[...] = (v * lax.rsqrt(jnp.maximum(v, 1e-30))) * (half_conv * ti + half_conv)

    h = jnp.zeros((SUBLANES, width), F32)
    ap = jnp.ones((SUBLANES, width), F32)
    for j in range(seg):
        rows = slice(j * SUBLANES, (j + 1) * SUBLANES)
        aj = a_s[rows, :]
        h = aj * h + u_s[rows, :]
        ap = aj * ap
        h_s[rows, :] = h
        ap_s[rows, :] = ap

    sub = lax.broadcasted_iota(jnp.int32, (SUBLANES, width), 0)
    for sh in (1, 2, 4):
        keep = sub >= sh
        h = jnp.where(keep, ap * pltpu.roll(h, sh, 0) + h, h)
        ap = jnp.where(keep, ap * pltpu.roll(ap, sh, 0), ap)
    leave = h + ap * hc[0:1, :]
    enter = jnp.where(sub == 0, hc[0:1, :], pltpu.roll(leave, 1, 0))
    hc[0:1, :] = leave[SUBLANES - 1:SUBLANES, :]

    enter_t = jnp.concatenate([enter] * seg, axis=0)
    y = ((h_s[...] + ap_s[...] * enter_t) * jax.nn.gelu(ga)).astype(BF16)
    o_ref[...] = lax.dot_general(perm, y, TN_DIMS, preferred_element_type=F32).astype(o_ref.dtype)


def _rglru(proj, conv_w, conv_b, wg, b_r, b_i, lam, bsz, seq, ts=LRU_TILE):
    width = conv_w.shape[1]
    nt = seq // ts
    row = lambda b, t: b * nt + t
    vec = lambda: _resident((1, width))
    perm = jnp.asarray(_segment_permutation(ts), BF16)
    tile = lambda: pltpu.VMEM((ts, width), F32)
    return pl.pallas_call(
        _rglru_kernel,
        grid=(bsz, nt),
        in_specs=[pl.BlockSpec((ts, width), lambda b, t: (row(b, t), 0)),
                  pl.BlockSpec((ts, width), lambda b, t: (row(b, t), 1)),
                  _resident(perm.shape), _resident(conv_w.shape), vec(), _resident(wg.shape),
                  vec(), vec(), vec()],
        out_specs=pl.BlockSpec((ts, width), lambda b, t: (row(b, t), 0)),
        out_shape=jax.ShapeDtypeStruct((bsz * seq, width), BF16),
        scratch_shapes=[pltpu.VMEM((ts + (CONV_W - 1) * SUBLANES, width), F32),
                        tile(), tile(), tile(), tile(),
                        pltpu.VMEM((SUBLANES, width), F32), pltpu.VMEM((SUBLANES, width), F32)],
        compiler_params=_params(2), name="rglru",
    )(proj, proj, perm, conv_w, conv_b.reshape(1, width), wg, b_r.reshape(1, width),
      b_i.reshape(1, width), lam.reshape(1, width))


def _pair_level(n):
    t = lax.broadcasted_iota(jnp.int32, (n, n), 0)
    s = lax.broadcasted_iota(jnp.int32, (n, n), 1)
    x = t ^ s
    lvl = jnp.zeros((n, n), jnp.int32)
    w = 1
    while w < n:
        lvl = lvl + (x >= w).astype(jnp.int32)
        w *= 2
    return jnp.where(s > t, -1, lvl)


def _hgrn2_kernel(q_ref, f_ref, v_ref, g_ref, lbl_ref, ng_ref, o_ref, st, bc_s, *, layer):
    ts, dk = q_ref.shape[0], LANES
    c = CHUNK
    nck = ts // c
    heads = q_ref.shape[1] // dk

    @pl.when(pl.program_id(2) == 0)
    def _():
        st[...] = jnp.zeros_like(st)

    lg = lbl_ref[...]
    e = jnp.exp(lg - jnp.max(lg, axis=0, keepdims=True))
    lb = jnp.sum(e[0:layer + 1, :], axis=0, keepdims=True) / jnp.sum(e, axis=0, keepdims=True)

    row = lax.broadcasted_iota(jnp.int32, (c, dk), 0)
    tt = lax.broadcasted_iota(jnp.int32, (c, c), 0)
    ss = lax.broadcasted_iota(jnp.int32, (c, c), 1)
    tri = (ss <= tt).astype(BF16)
    lvl = _pair_level(c)
    n_lvl = c.bit_length()
    lvl_b = lvl.astype(F32).astype(BF16)
    right_b = [None] + [((row >> (p - 1)) & 1).astype(F32).astype(BF16) for p in range(1, n_lvl)]
    side = [None] + [jnp.where((row & (1 << (p - 1))) != 0, 1.0, -1.0) for p in range(1, n_lvl)]
    r4 = row & 3

    for ci, hh in [(ci, hh) for ci in range(nck) for hh in range(heads)]:
        r0 = ci * c
        cols = slice(hh * dk, (hh + 1) * dk)
        lbh = lb[:, cols]
        qf = _silu(q_ref[r0:r0 + c, cols].astype(F32))
        f = lbh + (1.0 - lbh) * _sigmoid(f_ref[r0:r0 + c, cols])
        kk = 1.0 - f
        qb, kb = qf.astype(BF16), kk.astype(BF16)
        bc = LOG2_E * sum(_dot(tri, part) for part in _split3(jnp.log(f)))
        bc_s[hh * nck + ci] = bc
        vb = v_ref[r0:r0 + c, cols]

        scores = jnp.where(lvl_b == 0,
                           lax.dot_general(qb, kb, NT_DIMS,
                                           preferred_element_type=F32).astype(BF16), 0)
        for p in range(1, n_lvl):
            w = 1 << (p - 1)
            if w == 1:
                dec = jnp.where(right_b[p] > 0, f.astype(BF16), 1)
            elif w == 2:
                dec = jnp.where(r4 == 0, pltpu.roll(f, c - 1, 0),
                                jnp.where(r4 == 1, 1.0,
                                          jnp.where(r4 == 2, f, f * pltpu.roll(f, 1, 0)))
                                ).astype(BF16)
            else:
                gat = jnp.concatenate(
                    [jnp.broadcast_to(bc_s[hh * nck + ci, b * 2 * w + w - 1:b * 2 * w + w, :],
                                      (2 * w, dk)) for b in range(c // (2 * w))], axis=0)
                dec = jnp.exp2((bc - gat) * side[p]).astype(BF16)
            z = jnp.where(right_b[p] > 0, qb, kb) * dec
            scores = jnp.where(lvl_b == p,
                               lax.dot_general(z, z, NT_DIMS,
                                               preferred_element_type=F32).astype(BF16), scores)
        o = _dot(scores, vb)

        b_last = bc[c - 1:c, :]
        o = o + lax.dot_general((qf * jnp.exp2(bc)).astype(BF16), st[hh].astype(BF16), NT_DIMS,
                                preferred_element_type=F32)
        kd = (kk * jnp.exp2(b_last - bc)).astype(BF16)
        st[hh] = st[hh] * jnp.exp2(b_last) + lax.dot_general(vb, kd, TN_DIMS,
                                                             preferred_element_type=F32)

        o = o * lax.rsqrt(jnp.mean(o * o, axis=-1, keepdims=True) + NORM_EPS)
        gv = g_ref[r0:r0 + c, cols].astype(F32)
        o_ref[r0:r0 + c, cols] = (o * ng_ref[:, cols] * _silu(gv)).astype(o_ref.dtype)


def _hgrn2(proj, fproj, lb_logits, norm_g, layer, bsz, seq, col0, ts=MIX_TILE,
           heads=HGRN2_HEADS_PER_STEP):
    dk = LANES
    wd = heads * dk
    width = fproj.shape[1]
    nt = seq // ts
    q0, v0, g0 = (o // heads for o in col0)
    row = lambda b, t: b * nt + t
    n_layers = lb_logits.shape[0]
    return pl.pallas_call(
        functools.partial(_hgrn2_kernel, layer=layer),
        grid=(bsz, B_HEADS // heads, nt),
        in_specs=[pl.BlockSpec((ts, wd), lambda b, h, t: (row(b, t), q0 + h)),
                  pl.BlockSpec((ts, wd), lambda b, h, t: (row(b, t), h)),
                  pl.BlockSpec((ts, wd), lambda b, h, t: (row(b, t), v0 + h)),
                  pl.BlockSpec((ts, wd), lambda b, h, t: (row(b, t), g0 + h)),
                  pl.BlockSpec((n_layers, wd), lambda b, h, t: (0, h)),
                  pl.BlockSpec((1, wd), lambda b, h, t: (0, h))],
        out_specs=pl.BlockSpec((ts, wd), lambda b, h, t: (row(b, t), h)),
        out_shape=jax.ShapeDtypeStruct((bsz * seq, width), BF16),
        scratch_shapes=[pltpu.VMEM((heads, dk, dk), F32),
                        pltpu.VMEM((heads * (ts // CHUNK), CHUNK, dk), F32)],
        compiler_params=_params(3), name="hgrn2",
    )(proj, fproj, proj, proj, lb_logits, norm_g.reshape(1, width))


def _ssd_expand_matrices(nck):
    e = np.zeros((nck, LANES, M_HPG * LANES + M_HPG * M_HEAD_DIM), np.float32)
    for c in range(nck):
        for r in range(M_HPG):
            e[c, SUBLANES * c + r, r * LANES:(r + 1) * LANES] = 1.0
            e[c, SUBLANES * (nck + c) + r,
              M_HPG * LANES + r * M_HEAD_DIM:M_HPG * LANES + (r + 1) * M_HEAD_DIM] = 1.0
    return e


def _ssd_kernel(*refs, groups):
    n_in = 16
    (z, x, b, c, dt, cwx, cbx, cwb, cbb, cwc, cbc, bias, alog, drow, ng, e_ref) = refs[:n_in]
    o_ref, scratch = refs[n_in], refs[n_in + 1:]
    gw, n = x.shape[1] // groups, b.shape[1] // groups
    for gi in range(groups):
        wide = (slice(None), slice(gi * gw, (gi + 1) * gw))
        narrow = (slice(None), slice(gi * n, (gi + 1) * n))
        _ssd_group(z.at[wide], x.at[wide], b.at[narrow], c.at[narrow], dt.at[gi], cwx.at[wide],
                   cbx.at[wide], cwb.at[narrow], cbb.at[narrow], cwc.at[narrow], cbc.at[narrow],
                   bias.at[gi], alog.at[gi], drow.at[wide], ng.at[wide], e_ref, o_ref.at[wide],
                   *[sc.at[gi] for sc in scratch])


def _ssd_group(z_ref, x_ref, b_ref, c_ref, dt_ref, cwx_ref, cbx_ref, cwb_ref, cbb_ref,
               cwc_ref, cbc_ref, bias_ref, alog_ref, drow_ref, ng_ref, e_ref, o_ref,
               xbuf, bbuf, cbuf, xc_s, bc_s, cc_s, lhs_s, xw_s, sb_s, dec_s, st):
    ts, gw = x_ref.shape
    n = D_STATE
    c = CHUNK
    nck = ts // c
    nb = M_HPG * LANES

    @pl.when(pl.program_id(2) == 0)
    def _():
        xbuf[0:SUBLANES, :] = jnp.zeros((SUBLANES, gw), F32)
        bbuf[0:SUBLANES, :] = jnp.zeros((SUBLANES, n), F32)
        cbuf[0:SUBLANES, :] = jnp.zeros((SUBLANES, n), F32)
        st[...] = jnp.zeros_like(st)

    _conv_silu(x_ref, xbuf, cwx_ref, cbx_ref, xc_s)
    _conv_silu(b_ref, bbuf, cwb_ref, cbb_ref, bc_s)
    _conv_silu(c_ref, cbuf, cwc_ref, cbc_ref, cc_s)

    tt = lax.broadcasted_iota(jnp.int32, (c, c), 0)
    ss = lax.broadcasted_iota(jnp.int32, (c, c), 1)
    causal = ss <= tt
    upper = (tt <= ss).astype(F32)
    low_half = lax.broadcasted_iota(jnp.int32, (1, LANES), 1) < M_HEAD_DIM
    low_lanes = lax.broadcasted_iota(jnp.int32, (c + D_STATE, LANES), 1) < M_HEAD_DIM

    bias = jnp.concatenate([bias_ref[...]] * nck, axis=0)
    neg_a = jnp.concatenate([-jnp.exp(alog_ref[...])] * nck, axis=0)
    dt = _softplus(dt_ref[...].reshape(nck * SUBLANES, c) + bias)
    a_row = LOG2_E * _dot_f32(dt * neg_a, upper)
    dtw = dt * jnp.exp2(a_row[:, c - 1:c] - a_row)
    src_row = a_row - LOG2_E * jnp.log(dt)
    rows = [a_row, dtw]
    if 2 * nck * SUBLANES < c:
        rows.append(jnp.zeros((c - 2 * nck * SUBLANES, c), F32))
    hi, mid, lo = _split3(jnp.concatenate(rows, axis=0).T)

    for ci in range(nck):
        r0 = ci * c
        e_c = e_ref[ci]
        ex = _dot(hi, e_c)
        a_col = ex[:, :nb] + _dot(mid, e_c[:, :nb]) + _dot(lo, e_c[:, :nb])
        xw_s[r0:r0 + c, :] = (xc_s[r0:r0 + c, :] * ex[:, nb:]).astype(BF16)
        cm = cc_s[r0:r0 + c, :]
        cmf = cm.astype(F32)
        cb = lax.dot_general(cm, bc_s[r0:r0 + c, :], NT_DIMS, preferred_element_type=F32)
        for r in range(M_HPG):
            a_t = a_col[:, r * LANES:(r + 1) * LANES]
            src = src_row[ci * SUBLANES + r:ci * SUBLANES + r + 1, :]
            lmat = jnp.exp2(jnp.where(causal, a_t - src, -1e30))
            lhs_s[ci * M_HPG + r] = jnp.concatenate(
                [(cb * lmat).astype(BF16), (cmf * jnp.exp2(a_t)).astype(BF16)], axis=1)
        e_last = [jnp.exp2(a_col[c - 1:c, r * LANES:(r + 1) * LANES]) for r in range(M_HPG)]
        dec_s[ci:ci + 1, :] = jnp.concatenate([jnp.where(low_half, e_last[0], e_last[1]),
                                               jnp.where(low_half, e_last[2], e_last[3])], axis=1)

    for ci in range(nck):
        r0 = ci * c
        sb_s[ci] = st[...].astype(BF16)
        st[...] = st[...] * dec_s[ci:ci + 1, :] + lax.dot_general(
            bc_s[r0:r0 + c, :], xw_s[r0:r0 + c, :], TN_DIMS, preferred_element_type=F32)

    for ci in range(nck):
        r0 = ci * c
        xc = xc_s[r0:r0 + c, :]
        rhs = jnp.concatenate([xc.astype(BF16), sb_s[ci]], axis=0)
        tiles = []
        for j in range(gw // LANES):
            rt = rhs[:, j * LANES:(j + 1) * LANES]
            tiles.append(_dot(lhs_s[ci * M_HPG + 2 * j], jnp.where(low_lanes, rt, 0))
                         + _dot(lhs_s[ci * M_HPG + 2 * j + 1], jnp.where(low_lanes, 0, rt)))
        y = drow_ref[...] * xc + jnp.concatenate(tiles, axis=1)
        y = y * _silu(z_ref[r0:r0 + c, :].astype(F32))
        y = y * lax.rsqrt(jnp.mean(y * y, axis=-1, keepdims=True) + NORM_EPS)
        o_ref[r0:r0 + c, :] = (y * ng_ref[...]).astype(o_ref.dtype)


def _ssd(proj, dt_t, conv_w, conv_b, dt_bias, a_log, d_skip, norm_g, bsz, seq, ts=MIX_TILE,
         groups=SSD_GROUPS_PER_STEP):
    gw = M_HPG * M_HEAD_DIM
    n = D_STATE
    d_inner = M_GROUPS * gw
    nt = seq // ts
    nck = ts // CHUNK
    row = lambda b, t: b * nt + t
    gp = groups
    zb, xb = 0, d_inner // (gw * gp)
    bb, cb = 2 * d_inner // (n * gp), (2 * d_inner // n + M_GROUPS) // gp
    wbb, wcb = d_inner // (n * gp), (d_inner // n + M_GROUPS) // gp
    cw2 = conv_w.reshape(CONV_W, -1)
    cb2 = conv_b.reshape(1, -1)

    def per_head(v):
        v = jnp.pad(v.reshape(M_GROUPS, M_HPG), ((0, 0), (0, SUBLANES - M_HPG)))
        return jnp.broadcast_to(v[:, :, None], (M_GROUPS, SUBLANES, LANES)).astype(F32)

    drow = jnp.repeat(d_skip.astype(F32), M_HEAD_DIM).reshape(1, d_inner)
    e_mat = jnp.asarray(_ssd_expand_matrices(nck), BF16)
    g3 = lambda b, g, t: (g, 0, 0)
    return pl.pallas_call(
        functools.partial(_ssd_kernel, groups=gp),
        grid=(bsz, M_GROUPS // gp, nt),
        in_specs=[pl.BlockSpec((ts, gp * gw), lambda b, g, t: (row(b, t), zb + g)),
                  pl.BlockSpec((ts, gp * gw), lambda b, g, t: (row(b, t), xb + g)),
                  pl.BlockSpec((ts, gp * n), lambda b, g, t: (row(b, t), bb + g)),
                  pl.BlockSpec((ts, gp * n), lambda b, g, t: (row(b, t), cb + g)),
                  pl.BlockSpec((gp, nck, SUBLANES, LANES), lambda b, g, t: (g, row(b, t), 0, 0)),
                  pl.BlockSpec((CONV_W, gp * gw), lambda b, g, t: (0, g)),
                  pl.BlockSpec((1, gp * gw), lambda b, g, t: (0, g)),
                  pl.BlockSpec((CONV_W, gp * n), lambda b, g, t: (0, wbb + g)),
                  pl.BlockSpec((1, gp * n), lambda b, g, t: (0, wbb + g)),
                  pl.BlockSpec((CONV_W, gp * n), lambda b, g, t: (0, wcb + g)),
                  pl.BlockSpec((1, gp * n), lambda b, g, t: (0, wcb + g)),
                  pl.BlockSpec((gp, SUBLANES, LANES), g3),
                  pl.BlockSpec((gp, SUBLANES, LANES), g3),
                  pl.BlockSpec((1, gp * gw), lambda b, g, t: (0, g)),
                  pl.BlockSpec((1, gp * gw), lambda b, g, t: (0, g)),
                  _resident(e_mat.shape)],
        out_specs=pl.BlockSpec((ts, gp * gw), lambda b, g, t: (row(b, t), g)),
        out_shape=jax.ShapeDtypeStruct((bsz * seq, d_inner), BF16),
        scratch_shapes=[pltpu.VMEM((gp,) + shape, dt) for shape, dt in [
            ((ts + SUBLANES, gw), F32), ((ts + SUBLANES, n), F32), ((ts + SUBLANES, n), F32),
            ((ts, gw), F32), ((ts, n), BF16), ((ts, n), BF16),
            ((nck * M_HPG, CHUNK, CHUNK + n), BF16), ((ts, gw), BF16), ((nck, n, gw), BF16),
            ((max(nck, SUBLANES), gw), F32), ((n, gw), F32)]],
        compiler_params=_params(3), name="ssd",
    )(proj, proj, proj, proj, dt_t, cw2, cb2, cw2, cb2, cw2, cb2,
      per_head(dt_bias), per_head(a_log), drow, norm_g.reshape(1, d_inner), e_mat)


def _post_kernel(*refs, n_mix, final_norm, ff_chunk):
    mix_refs = refs[:n_mix]
    wmix_refs = refs[n_mix:2 * n_mix]
    (x_ref, gq_ref, wq_ref, k_ref, v_ref, wo_ref, gf_ref, w1_ref, w2_ref, gfin_ref,
     o_ref) = refs[2 * n_mix:]
    d = x_ref.shape[1]
    hd = d // X_HEADS

    x = x_ref[...]
    for m_ref, w_ref in zip(mix_refs, wmix_refs):
        x = x + _dot(m_ref[...], w_ref[...])

    q = _dot(_rms(x, gq_ref[...]).astype(BF16), wq_ref[...]).astype(BF16)
    heads = []
    for h in range(X_HEADS):
        sc = lax.dot_general(q[:, h * hd:(h + 1) * hd], k_ref[0, :, h * hd:(h + 1) * hd], NT_DIMS,
                             preferred_element_type=F32) * (hd ** -0.5)
        p = jnp.exp(sc - jnp.max(sc, axis=-1, keepdims=True))
        p = p / jnp.sum(p, axis=-1, keepdims=True)
        heads.append(_dot(p.astype(BF16), v_ref[0, :, h * hd:(h + 1) * hd]))
    x = x + _dot(jnp.concatenate(heads, axis=-1).astype(BF16), wo_ref[...])

    hn = _rms(x, gf_ref[...]).astype(BF16)
    y = x
    for c in range(0, w1_ref.shape[1], ff_chunk):
        hid = jnp.square(jnp.maximum(_dot(hn, w1_ref[:, c:c + ff_chunk]), 0.0))
        y = y + _dot(hid.astype(BF16), w2_ref[c:c + ff_chunk, :])
    if final_norm:
        y = _rms(y, gfin_ref[...])
    o_ref[...] = y


def _post(mixes, wmixes, x2d, gq, wq, kv, wo, gf, w1, w2, gfin, final_norm, bsz, seq,
          tm=TOKEN_TILE, ff_chunk=1024):
    t, d = x2d.shape
    nt = seq // tm
    mem_len = kv.shape[1]
    tile = lambda w: pl.BlockSpec((tm, w), lambda b, i: (b * nt + i, 0))
    in_specs = [tile(m.shape[1]) for m in mixes] + [_resident(w.shape) for w in wmixes]
    in_specs += [tile(d), _resident((1, d)), _resident(wq.shape),
                 pl.BlockSpec((1, mem_len, d), lambda b, i: (b, 0, 0)),
                 pl.BlockSpec((1, mem_len, d), lambda b, i: (b, 0, 1)),
                 _resident(wo.shape), _resident((1, d)), _resident(w1.shape),
                 _resident(w2.shape), _resident((1, d))]
    return pl.pallas_call(
        functools.partial(_post_kernel, n_mix=len(mixes), final_norm=final_norm,
                          ff_chunk=ff_chunk),
        grid=(bsz, nt), in_specs=in_specs, out_specs=tile(d),
        out_shape=jax.ShapeDtypeStruct((t, d), F32),
        compiler_params=_params(2), name="post",
    )(*mixes, *wmixes, x2d, gq.reshape(1, d), wq, kv, kv, wo, gf.reshape(1, d), w1, w2,
      gfin.reshape(1, d))


def kernel(x, mem, norm_mix_g, norm_mem_q_g, norm_mem_kv_g, norm_ffn_g, final_norm_g, e_w_in, a_conv_w, a_conv_b, a_gate_r_w, a_gate_r_b, a_gate_i_w, a_gate_i_b, a_lambda, b_lb_logits, b_norm_g, e_w_out, o_w_in, m_conv_w, m_conv_b, m_dt_bias, m_a_log, m_d, m_norm_g, o_w_out, xq_w, xk_w, xv_w, xo_w, ffn_w1, ffn_w2):
    bsz, seq, d = x.shape
    mem_len = mem.shape[1]
    depth = norm_mix_g.shape[0]
    t = bsz * seq
    bf = lambda w: w.astype(BF16)
    xs = x.reshape(t, d)
    mem2d = mem.reshape(bsz * mem_len, d)

    for l in range(depth):
        (kv,) = _inproj(mem2d, norm_mem_kv_g[l],
                        [bf(jnp.concatenate([xk_w[l], xv_w[l]], axis=1))], ["nn"], [BF16])
        kv = kv.reshape(bsz, mem_len, 2 * d)
        if l % 2 == 0:
            e = l // 2
            w = e_w_in[e]
            wa = d
            sec = lambda k: w[:, k * wa:(k + 1) * wa]
            w_main = bf(jnp.concatenate([sec(0), sec(1), sec(2), sec(4), sec(5)], axis=1))
            proj, fproj = _inproj(xs, norm_mix_g[l], [w_main, bf(sec(3))], ["nn", "nn"],
                                  [BF16, F32])
            wg = bf(0.5 * jnp.concatenate([a_gate_r_w[e], a_gate_i_w[e]], axis=-1))
            ya = _rglru(proj, a_conv_w[e], a_conv_b[e], wg, 0.5 * a_gate_r_b[e],
                        0.5 * a_gate_i_b[e], a_lambda[e], bsz, seq)
            blk = wa // LANES
            yb = _hgrn2(proj, fproj, b_lb_logits, b_norm_g[e], l, bsz, seq,
                        (2 * blk, 3 * blk, 4 * blk))
            mixes = [ya, yb]
            wmixes = [bf(e_w_out[e][:wa]), bf(e_w_out[e][wa:])]
        else:
            o = l // 2
            w = o_w_in[o]
            n_main = w.shape[1] - M_GROUPS * M_HPG
            w_dt = w[:, n_main:].T.reshape(M_GROUPS, M_HPG, d)
            w_dt = jnp.pad(w_dt, ((0, 0), (0, SUBLANES - M_HPG), (0, 0))).reshape(-1, d)
            proj, dt_t = _inproj(xs, norm_mix_g[l], [bf(w[:, :n_main]), bf(w_dt)], ["nn", "nt"],
                                 [BF16, F32])
            ym = _ssd(proj, dt_t, m_conv_w[o], m_conv_b[o], m_dt_bias[o], m_a_log[o], m_d[o],
                      m_norm_g[o], bsz, seq)
            mixes = [ym]
            wmixes = [bf(o_w_out[o])]
        xs = _post(mixes, wmixes, xs, norm_mem_q_g[l], bf(xq_w[l]), kv, bf(xo_w[l]),
                   norm_ffn_g[l], bf(ffn_w1[l]), bf(ffn_w2[l]), final_norm_g,
                   l == depth - 1, bsz, seq)
    return xs.reshape(bsz, seq, d)
```

```python
import functools

import jax
import jax.numpy as jnp
import numpy as np
from jax import lax
from jax.experimental import pallas as pl
from jax.experimental.pallas import tpu as pltpu

F32 = jnp.float32
BF16 = jnp.bfloat16

NORM_EPS = 1e-6
LOG2_E = 1.4426950408889634
CONV_W = 4
LANES = 128
SUBLANES = 8
VMEM_LIMIT_BYTES = 56 * 1024 * 1024

A_HEADS = 8
LRU_C = 8.0
B_HEADS = 8
M_HEAD_DIM = 64
M_GROUPS = 8
M_HPG = 4
D_STATE = 128
X_HEADS = 4
CHUNK = 128

TOKEN_TILE = 512
LRU_TILE = 256
MIX_TILE = 1024
HGRN2_HEADS_PER_STEP = 4
SSD_GROUPS_PER_STEP = 4

NT_DIMS = (((1,), (1,)), ((), ()))
TN_DIMS = (((0,), (0,)), ((), ()))


def _params(n_axes):
    return pltpu.CompilerParams(
        dimension_semantics=("arbitrary",) * n_axes, vmem_limit_bytes=VMEM_LIMIT_BYTES)


def _resident(shape):
    nd = len(shape)
    return pl.BlockSpec(shape, lambda *_: (0,) * nd, pipeline_mode=pl.Buffered(1))


def _rms(x, g):
    return x * lax.rsqrt(jnp.mean(x * x, axis=-1, keepdims=True) + NORM_EPS) * g


def _sigmoid(x):
    return 0.5 * jnp.tanh(0.5 * x) + 0.5


def _silu(x):
    h = 0.5 * x
    return h * jnp.tanh(h) + h


def _softplus(x):
    return jnp.maximum(x, 0.0) + jnp.log1p(jnp.exp(-jnp.abs(x)))


def _dot(a, b):
    return jnp.dot(a, b, preferred_element_type=F32)


def _dot_f32(a, b):
    return jnp.dot(a, b, preferred_element_type=F32, precision=lax.Precision.HIGHEST)


def _split3(x):
    hi = x.astype(BF16)
    r = x - hi.astype(F32)
    mid = r.astype(BF16)
    lo = (r - mid.astype(F32)).astype(BF16)
    return hi, mid, lo


def _layer_resident(w, layer, rows=None):
    r0, nr = (0, w.shape[1]) if rows is None else rows
    return pl.BlockSpec((None, nr, w.shape[2]), lambda *_: (layer, r0, 0),
                        pipeline_mode=pl.Buffered(1))


def _inproj_kernel(x_ref, g_ref, *refs, n_w, outs, col_chunk):
    w_refs, o_refs = refs[:n_w], refs[n_w:]
    xn = _rms(x_ref[...], g_ref[...]).astype(BF16)
    for (kind, wi, cols), o_ref in zip(outs, o_refs):
        w_ref = w_refs[wi]
        if kind == "nn":
            for j, c in enumerate(cols):
                o_ref[:, j * col_chunk:(j + 1) * col_chunk] = _dot(
                    xn, w_ref[:, c:c + col_chunk]).astype(o_ref.dtype)
        else:
            y = lax.dot_general(w_ref[...], xn, NT_DIMS, preferred_element_type=F32)
            for gi in range(y.shape[0] // SUBLANES):
                for ci in range(y.shape[1] // LANES):
                    o_ref[gi, ci] = y[gi * SUBLANES:(gi + 1) * SUBLANES,
                                      ci * LANES:(ci + 1) * LANES].astype(o_ref.dtype)


def _inproj(x2d, g, weights, outs, out_dtypes, tm=TOKEN_TILE, col_chunk=1024):
    t, d = x2d.shape
    tm = min(tm, t)
    assert t % tm == 0
    in_specs = [pl.BlockSpec((tm, d), lambda i: (i, 0)), _resident((1, d))]
    in_specs += [_layer_resident(w, l) for w, l in weights]
    out_specs, out_shapes = [], []
    for (kind, wi, cols), dt in zip(outs, out_dtypes):
        if kind == "nn":
            n = len(cols) * col_chunk
            out_specs.append(pl.BlockSpec((tm, n), lambda i: (i, 0)))
            out_shapes.append(jax.ShapeDtypeStruct((t, n), dt))
        else:
            rb = weights[wi][0].shape[1] // SUBLANES
            out_specs.append(pl.BlockSpec((rb, tm // LANES, SUBLANES, LANES),
                                          lambda i: (0, i, 0, 0)))
            out_shapes.append(jax.ShapeDtypeStruct((rb, t // LANES, SUBLANES, LANES), dt))
    return pl.pallas_call(
        functools.partial(_inproj_kernel, n_w=len(weights), outs=tuple(outs),
                          col_chunk=col_chunk),
        grid=(t // tm,), in_specs=in_specs, out_specs=out_specs, out_shape=out_shapes,
        compiler_params=_params(1), name="inproj",
    )(x2d, g.reshape(1, d), *[w for w, _ in weights])


def _conv_silu(src_ref, buf, w_ref, b_ref, dst_ref, rows=CHUNK):
    ts = src_ref.shape[0]
    buf[SUBLANES:SUBLANES + ts, :] = src_ref[...].astype(F32)
    for r0 in range(0, ts, rows):
        y = b_ref[...] + w_ref[CONV_W - 1:CONV_W, :] * buf[SUBLANES + r0:SUBLANES + r0 + rows, :]
        for k in range(CONV_W - 1):
            off = SUBLANES - (CONV_W - 1) + k + r0
            y = y + w_ref[k:k + 1, :] * buf[off:off + rows, :]
        dst_ref[r0:r0 + rows, :] = _silu(y).astype(dst_ref.dtype)
    buf[0:SUBLANES, :] = buf[ts:ts + SUBLANES, :]


def _segment_permutation(ts):
    seg = ts // SUBLANES
    p = np.zeros((ts, ts), np.float32)
    for j in range(seg):
        for s in range(SUBLANES):
            p[SUBLANES * j + s, seg * s + j] = 1.0
    return p


def _segment_conv(xa, xe_s, tail, w, b):
    ts, width = xa.shape
    halo = (CONV_W - 1) * SUBLANES
    first_sublane = lax.broadcasted_iota(jnp.int32, (SUBLANES, width), 0) == 0
    for i in range(CONV_W - 1):
        blk = xa[ts - halo + i * SUBLANES:ts - halo + (i + 1) * SUBLANES, :]
        xe_s[i * SUBLANES:(i + 1) * SUBLANES, :] = jnp.where(
            first_sublane, tail[i:i + 1, :], pltpu.roll(blk, 1, 0))
        tail[i:i + 1, :] = blk[SUBLANES - 1:SUBLANES, :]
    xe_s[halo:halo + ts, :] = xa
    conv = b[...] + w[CONV_W - 1:CONV_W, :] * xa
    for k in range(CONV_W - 1):
        conv = conv + w[k:k + 1, :] * xe_s[k * SUBLANES:k * SUBLANES + ts, :]
    return conv


def _rglru_kernel(xa_ref, ga_ref, perm_ref, cw_ref, cb_ref, wg_ref, br_ref, bi_ref, lam_ref, o_ref,
                  xe_s, a_s, u_s, h_s, ap_s, tail, hc):
    ts, width = xa_ref.shape
    hd = width // A_HEADS
    seg = ts // SUBLANES

    @pl.when(pl.program_id(1) == 0)
    def _():
        tail[...] = jnp.zeros_like(tail)
        hc[...] = jnp.zeros_like(hc)

    perm = perm_ref[...]
    xa = _dot(perm, xa_ref[...])
    ga = _dot(perm, ga_ref[...])
    conv = _segment_conv(xa, xe_s, tail, cw_ref, cb_ref)

    convb = conv.astype(BF16)
    r_pre, i_pre = [], []
    for h in range(A_HEADS):
        gate = _dot(convb[:, h * hd:(h + 1) * hd], wg_ref[h])
        r_pre.append(gate[:, :hd])
        i_pre.append(gate[:, hd:])
    tr = jnp.tanh(jnp.concatenate(r_pre, axis=-1) + br_ref[...])
    ti = jnp.tanh(jnp.concatenate(i_pre, axis=-1) + bi_ref[...])
    half_c = (-0.5 * LRU_C) * _softplus(-lam_ref[...])
    a = jnp.exp(half_c * tr + half_c)
    a_s[...] = a
    v = 1.0 - a * a
    half_conv = 0.5 * conv
    u_s[...] = (v * lax.rsqrt(jnp.maximum(v, 1e-30))) * (half_conv * ti + half_conv)

    h = jnp.zeros((SUBLANES, width), F32)
    ap = jnp.ones((SUBLANES, width), F32)
    for j in range(seg):
        rows = slice(j * SUBLANES, (j + 1) * SUBLANES)
        aj = a_s[rows, :]
        h = aj * h + u_s[rows, :]
        ap = aj * ap
        h_s[rows, :] = h
        ap_s[rows, :] = ap

    sub = lax.broadcasted_iota(jnp.int32, (SUBLANES, width), 0)
    for sh in (1, 2, 4):
        keep = sub >= sh
        h = jnp.where(keep, ap * pltpu.roll(h, sh, 0) + h, h)
        ap = jnp.where(keep, ap * pltpu.roll(ap, sh, 0), ap)
    leave = h + ap * hc[0:1, :]
    enter = jnp.where(sub == 0, hc[0:1, :], pltpu.roll(leave, 1, 0))
    hc[0:1, :] = leave[SUBLANES - 1:SUBLANES, :]

    enter_t = jnp.concatenate([enter] * seg, axis=0)
    y = ((h_s[...] + ap_s[...] * enter_t) * jax.nn.gelu(ga)).astype(BF16)
    o_ref[...] = lax.dot_general(perm, y, TN_DIMS, preferred_element_type=F32).astype(o_ref.dtype)


def _rglru(proj, conv_w, conv_b, wg, b_r, b_i, lam, bsz, seq, ts=LRU_TILE):
    width = conv_w.shape[1]
    nt = seq // ts
    row = lambda b, t: b * nt + t
    vec = lambda: _resident((1, width))
    perm = jnp.asarray(_segment_permutation(ts), BF16)
    tile = lambda: pltpu.VMEM((ts, width), F32)
    return pl.pallas_call(
        _rglru_kernel,
        grid=(bsz, nt),
        in_specs=[pl.BlockSpec((ts, width), lambda b, t: (row(b, t), 0)),
                  pl.BlockSpec((ts, width), lambda b, t: (row(b, t), 1)),
                  _resident(perm.shape), _resident(conv_w.shape), vec(), _resident(wg.shape),
                  vec(), vec(), vec()],
        out_specs=pl.BlockSpec((ts, width), lambda b, t: (row(b, t), 0)),
        out_shape=jax.ShapeDtypeStruct((bsz * seq, width), BF16),
        scratch_shapes=[pltpu.VMEM((ts + (CONV_W - 1) * SUBLANES, width), F32),
                        tile(), tile(), tile(), tile(),
                        pltpu.VMEM((SUBLANES, width), F32), pltpu.VMEM((SUBLANES, width), F32)],
        compiler_params=_params(2), name="rglru",
    )(proj, proj, perm, conv_w, conv_b.reshape(1, width), wg, b_r.reshape(1, width),
      b_i.reshape(1, width), lam.reshape(1, width))


def _pair_level(n):
    t = lax.broadcasted_iota(jnp.int32, (n, n), 0)
    s = lax.broadcasted_iota(jnp.int32, (n, n), 1)
    x = t ^ s
    lvl = jnp.zeros((n, n), jnp.int32)
    w = 1
    while w < n:
        lvl = lvl + (x >= w).astype(jnp.int32)
        w *= 2
    return jnp.where(s > t, -1, lvl)


def _hgrn2_kernel(q_ref, f_ref, v_ref, g_ref, lbl_ref, ng_ref, o_ref, st, bc_s, *, layer):
    ts, dk = q_ref.shape[0], LANES
    c = CHUNK
    nck = ts // c
    heads = q_ref.shape[1] // dk

    @pl.when(pl.program_id(2) == 0)
    def _():
        st[...] = jnp.zeros_like(st)

    lg = lbl_ref[...]
    e = jnp.exp(lg - jnp.max(lg, axis=0, keepdims=True))
    lb = jnp.sum(e[0:layer + 1, :], axis=0, keepdims=True) / jnp.sum(e, axis=0, keepdims=True)

    row = lax.broadcasted_iota(jnp.int32, (c, dk), 0)
    tt = lax.broadcasted_iota(jnp.int32, (c, c), 0)
    ss = lax.broadcasted_iota(jnp.int32, (c, c), 1)
    tri = (ss <= tt).astype(BF16)
    lvl = _pair_level(c)
    n_lvl = c.bit_length()
    lvl_b = lvl.astype(F32).astype(BF16)
    right_b = [None] + [((row >> (p - 1)) & 1).astype(F32).astype(BF16) for p in range(1, n_lvl)]
    side = [None] + [jnp.where((row & (1 << (p - 1))) != 0, 1.0, -1.0) for p in range(1, n_lvl)]
    r4 = row & 3

    for ci, hh in [(ci, hh) for ci in range(nck) for hh in range(heads)]:
        r0 = ci * c
        cols = slice(hh * dk, (hh + 1) * dk)
        lbh = lb[:, cols]
        qf = _silu(q_ref[r0:r0 + c, cols].astype(F32))
        f = lbh + (1.0 - lbh) * _sigmoid(f_ref[r0:r0 + c, cols])
        kk = 1.0 - f
        qb, kb = qf.astype(BF16), kk.astype(BF16)
        bc = LOG2_E * sum(_dot(tri, part) for part in _split3(jnp.log(f)))
        bc_s[hh * nck + ci] = bc
        vb = v_ref[r0:r0 + c, cols]

        scores = jnp.where(lvl_b == 0,
                           lax.dot_general(qb, kb, NT_DIMS,
                                           preferred_element_type=F32).astype(BF16), 0)
        for p in range(1, n_lvl):
            w = 1 << (p - 1)
            if w == 1:
                dec = jnp.where(right_b[p] > 0, f.astype(BF16), 1)
            elif w == 2:
                dec = jnp.where(r4 == 0, pltpu.roll(f, c - 1, 0),
                                jnp.where(r4 == 1, 1.0,
                                          jnp.where(r4 == 2, f, f * pltpu.roll(f, 1, 0)))
                                ).astype(BF16)
            else:
                gat = jnp.concatenate(
                    [jnp.broadcast_to(bc_s[hh * nck + ci, b * 2 * w + w - 1:b * 2 * w + w, :],
                                      (2 * w, dk)) for b in range(c // (2 * w))], axis=0)
                dec = jnp.exp2((bc - gat) * side[p]).astype(BF16)
            z = jnp.where(right_b[p] > 0, qb, kb) * dec
            scores = jnp.where(lvl_b == p,
                               lax.dot_general(z, z, NT_DIMS,
                                               preferred_element_type=F32).astype(BF16), scores)
        o = _dot(scores, vb)

        b_last = bc[c - 1:c, :]
        o = o + lax.dot_general((qf * jnp.exp2(bc)).astype(BF16), st[hh].astype(BF16), NT_DIMS,
                                preferred_element_type=F32)
        kd = (kk * jnp.exp2(b_last - bc)).astype(BF16)
        st[hh] = st[hh] * jnp.exp2(b_last) + lax.dot_general(vb, kd, TN_DIMS,
                                                             preferred_element_type=F32)

        o = o * lax.rsqrt(jnp.mean(o * o, axis=-1, keepdims=True) + NORM_EPS)
        gv = g_ref[r0:r0 + c, cols].astype(F32)
        o_ref[r0:r0 + c, cols] = (o * ng_ref[:, cols] * _silu(gv)).astype(o_ref.dtype)


def _hgrn2(proj, fproj, lb_logits, norm_g, layer, bsz, seq, col0, ts=MIX_TILE,
           heads=HGRN2_HEADS_PER_STEP):
    dk = LANES
    wd = heads * dk
    width = fproj.shape[1]
    nt = seq // ts
    q0, v0, g0 = (o // heads for o in col0)
    row = lambda b, t: b * nt + t
    n_layers = lb_logits.shape[0]
    return pl.pallas_call(
        functools.partial(_hgrn2_kernel, layer=layer),
        grid=(bsz, B_HEADS // heads, nt),
        in_specs=[pl.BlockSpec((ts, wd), lambda b, h, t: (row(b, t), q0 + h)),
                  pl.BlockSpec((ts, wd), lambda b, h, t: (row(b, t), h)),
                  pl.BlockSpec((ts, wd), lambda b, h, t: (row(b, t), v0 + h)),
                  pl.BlockSpec((ts, wd), lambda b, h, t: (row(b, t), g0 + h)),
                  pl.BlockSpec((n_layers, wd), lambda b, h, t: (0, h)),
                  pl.BlockSpec((1, wd), lambda b, h, t: (0, h))],
        out_specs=pl.BlockSpec((ts, wd), lambda b, h, t: (row(b, t), h)),
        out_shape=jax.ShapeDtypeStruct((bsz * seq, width), BF16),
        scratch_shapes=[pltpu.VMEM((heads, dk, dk), F32),
                        pltpu.VMEM((heads * (ts // CHUNK), CHUNK, dk), F32)],
        compiler_params=_params(3), name="hgrn2",
    )(proj, fproj, proj, proj, lb_logits, norm_g.reshape(1, width))


def _ssd_expand_matrices(nck):
    e = np.zeros((nck, LANES, M_HPG * LANES + M_HPG * M_HEAD_DIM), np.float32)
    for c in range(nck):
        for r in range(M_HPG):
            e[c, SUBLANES * c + r, r * LANES:(r + 1) * LANES] = 1.0
            e[c, SUBLANES * (nck + c) + r,
              M_HPG * LANES + r * M_HEAD_DIM:M_HPG * LANES + (r + 1) * M_HEAD_DIM] = 1.0
    return e


def _ssd_kernel(*refs, groups):
    n_in = 16
    (z, x, b, c, dt, cwx, cbx, cwb, cbb, cwc, cbc, bias, alog, drow, ng, e_ref) = refs[:n_in]
    o_ref, scratch = refs[n_in], refs[n_in + 1:]
    gw, n = x.shape[1] // groups, b.shape[1] // groups
    for gi in range(groups):
        wide = (slice(None), slice(gi * gw, (gi + 1) * gw))
        narrow = (slice(None), slice(gi * n, (gi + 1) * n))
        _ssd_group(z.at[wide], x.at[wide], b.at[narrow], c.at[narrow], dt.at[gi], cwx.at[wide],
                   cbx.at[wide], cwb.at[narrow], cbb.at[narrow], cwc.at[narrow], cbc.at[narrow],
                   bias.at[gi], alog.at[gi], drow.at[wide], ng.at[wide], e_ref, o_ref.at[wide],
                   *[sc.at[gi] for sc in scratch])


def _ssd_group(z_ref, x_ref, b_ref, c_ref, dt_ref, cwx_ref, cbx_ref, cwb_ref, cbb_ref,
               cwc_ref, cbc_ref, bias_ref, alog_ref, drow_ref, ng_ref, e_ref, o_ref,
               xbuf, bbuf, cbuf, xc_s, bc_s, cc_s, lhs_s, xw_s, sb_s, dec_s, st):
    ts, gw = x_ref.shape
    n = D_STATE
    c = CHUNK
    nck = ts // c
    nb = M_HPG * LANES

    @pl.when(pl.program_id(2) == 0)
    def _():
        xbuf[0:SUBLANES, :] = jnp.zeros((SUBLANES, gw), F32)
        bbuf[0:SUBLANES, :] = jnp.zeros((SUBLANES, n), F32)
        cbuf[0:SUBLANES, :] = jnp.zeros((SUBLANES, n), F32)
        st[...] = jnp.zeros_like(st)

    _conv_silu(x_ref, xbuf, cwx_ref, cbx_ref, xc_s)
    _conv_silu(b_ref, bbuf, cwb_ref, cbb_ref, bc_s)
    _conv_silu(c_ref, cbuf, cwc_ref, cbc_ref, cc_s)

    tt = lax.broadcasted_iota(jnp.int32, (c, c), 0)
    ss = lax.broadcasted_iota(jnp.int32, (c, c), 1)
    causal = ss <= tt
    upper = (tt <= ss).astype(F32)
    low_half = lax.broadcasted_iota(jnp.int32, (1, LANES), 1) < M_HEAD_DIM
    low_lanes = lax.broadcasted_iota(jnp.int32, (c + D_STATE, LANES), 1) < M_HEAD_DIM

    bias = jnp.concatenate([bias_ref[...]] * nck, axis=0)
    neg_a = jnp.concatenate([-jnp.exp(alog_ref[...])] * nck, axis=0)
    dt = _softplus(dt_ref[...].reshape(nck * SUBLANES, c) + bias)
    a_row = LOG2_E * _dot_f32(dt * neg_a, upper)
    dtw = dt * jnp.exp2(a_row[:, c - 1:c] - a_row)
    src_row = a_row - LOG2_E * jnp.log(dt)
    rows = [a_row, dtw]
    if 2 * nck * SUBLANES < c:
        rows.append(jnp.zeros((c - 2 * nck * SUBLANES, c), F32))
    hi, mid, lo = _split3(jnp.concatenate(rows, axis=0).T)

    for ci in range(nck):
        r0 = ci * c
        e_c = e_ref[ci]
        ex = _dot(hi, e_c)
        a_col = ex[:, :nb] + _dot(mid, e_c[:, :nb]) + _dot(lo, e_c[:, :nb])
        xw_s[r0:r0 + c, :] = (xc_s[r0:r0 + c, :] * ex[:, nb:]).astype(BF16)
        cm = cc_s[r0:r0 + c, :]
        cmf = cm.astype(F32)
        cb = lax.dot_general(cm, bc_s[r0:r0 + c, :], NT_DIMS, preferred_element_type=F32)
        for r in range(M_HPG):
            a_t = a_col[:, r * LANES:(r + 1) * LANES]
            src = src_row[ci * SUBLANES + r:ci * SUBLANES + r + 1, :]
            lmat = jnp.exp2(jnp.where(causal, a_t - src, -1e30))
            lhs_s[ci * M_HPG + r] = jnp.concatenate(
                [(cb * lmat).astype(BF16), (cmf * jnp.exp2(a_t)).astype(BF16)], axis=1)
        e_last = [jnp.exp2(a_col[c - 1:c, r * LANES:(r + 1) * LANES]) for r in range(M_HPG)]
        dec_s[ci:ci + 1, :] = jnp.concatenate([jnp.where(low_half, e_last[0], e_last[1]),
                                               jnp.where(low_half, e_last[2], e_last[3])], axis=1)

    for ci in range(nck):
        r0 = ci * c
        sb_s[ci] = st[...].astype(BF16)
        st[...] = st[...] * dec_s[ci:ci + 1, :] + lax.dot_general(
            bc_s[r0:r0 + c, :], xw_s[r0:r0 + c, :], TN_DIMS, preferred_element_type=F32)

    for ci in range(nck):
        r0 = ci * c
        xc = xc_s[r0:r0 + c, :]
        rhs = jnp.concatenate([xc.astype(BF16), sb_s[ci]], axis=0)
        tiles = []
        for j in range(gw // LANES):
            rt = rhs[:, j * LANES:(j + 1) * LANES]
            tiles.append(_dot(lhs_s[ci * M_HPG + 2 * j], jnp.where(low_lanes, rt, 0))
                         + _dot(lhs_s[ci * M_HPG + 2 * j + 1], jnp.where(low_lanes, 0, rt)))
        y = drow_ref[...] * xc + jnp.concatenate(tiles, axis=1)
        y = y * _silu(z_ref[r0:r0 + c, :].astype(F32))
        y = y * lax.rsqrt(jnp.mean(y * y, axis=-1, keepdims=True) + NORM_EPS)
        o_ref[r0:r0 + c, :] = (y * ng_ref[...]).astype(o_ref.dtype)


def _ssd(proj, dt_t, conv_w, conv_b, dt_bias, a_log, d_skip, norm_g, bsz, seq, ts=MIX_TILE,
         groups=SSD_GROUPS_PER_STEP):
    gw = M_HPG * M_HEAD_DIM
    n = D_STATE
    d_inner = M_GROUPS * gw
    nt = seq // ts
    nck = ts // CHUNK
    row = lambda b, t: b * nt + t
    gp = groups
    zb, xb = 0, d_inner // (gw * gp)
    bb, cb = 2 * d_inner // (n * gp), (2 * d_inner // n + M_GROUPS) // gp
    wbb, wcb = d_inner // (n * gp), (d_inner // n + M_GROUPS) // gp
    cw2 = conv_w.reshape(CONV_W, -1)
    cb2 = conv_b.reshape(1, -1)

    def per_head(v):
        v = jnp.pad(v.reshape(M_GROUPS, M_HPG), ((0, 0), (0, SUBLANES - M_HPG)))
        return jnp.broadcast_to(v[:, :, None], (M_GROUPS, SUBLANES, LANES)).astype(F32)

    drow = jnp.repeat(d_skip.astype(F32), M_HEAD_DIM).reshape(1, d_inner)
    e_mat = jnp.asarray(_ssd_expand_matrices(nck), BF16)
    g3 = lambda b, g, t: (g, 0, 0)
    return pl.pallas_call(
        functools.partial(_ssd_kernel, groups=gp),
        grid=(bsz, M_GROUPS // gp, nt),
        in_specs=[pl.BlockSpec((ts, gp * gw), lambda b, g, t: (row(b, t), zb + g)),
                  pl.BlockSpec((ts, gp * gw), lambda b, g, t: (row(b, t), xb + g)),
                  pl.BlockSpec((ts, gp * n), lambda b, g, t: (row(b, t), bb + g)),
                  pl.BlockSpec((ts, gp * n), lambda b, g, t: (row(b, t), cb + g)),
                  pl.BlockSpec((gp, nck, SUBLANES, LANES), lambda b, g, t: (g, row(b, t), 0, 0)),
                  pl.BlockSpec((CONV_W, gp * gw), lambda b, g, t: (0, g)),
                  pl.BlockSpec((1, gp * gw), lambda b, g, t: (0, g)),
                  pl.BlockSpec((CONV_W, gp * n), lambda b, g, t: (0, wbb + g)),
                  pl.BlockSpec((1, gp * n), lambda b, g, t: (0, wbb + g)),
                  pl.BlockSpec((CONV_W, gp * n), lambda b, g, t: (0, wcb + g)),
                  pl.BlockSpec((1, gp * n), lambda b, g, t: (0, wcb + g)),
                  pl.BlockSpec((gp, SUBLANES, LANES), g3),
                  pl.BlockSpec((gp, SUBLANES, LANES), g3),
                  pl.BlockSpec((1, gp * gw), lambda b, g, t: (0, g)),
                  pl.BlockSpec((1, gp * gw), lambda b, g, t: (0, g)),
                  _resident(e_mat.shape)],
        out_specs=pl.BlockSpec((ts, gp * gw), lambda b, g, t: (row(b, t), g)),
        out_shape=jax.ShapeDtypeStruct((bsz * seq, d_inner), BF16),
        scratch_shapes=[pltpu.VMEM((gp,) + shape, dt) for shape, dt in [
            ((ts + SUBLANES, gw), F32), ((ts + SUBLANES, n), F32), ((ts + SUBLANES, n), F32),
            ((ts, gw), F32), ((ts, n), BF16), ((ts, n), BF16),
            ((nck * M_HPG, CHUNK, CHUNK + n), BF16), ((ts, gw), BF16), ((nck, n, gw), BF16),
            ((max(nck, SUBLANES), gw), F32), ((n, gw), F32)]],
        compiler_params=_params(3), name="ssd",
    )(proj, proj, proj, proj, dt_t, cw2, cb2, cw2, cb2, cw2, cb2,
      per_head(dt_bias), per_head(a_log), drow, norm_g.reshape(1, d_inner), e_mat)


def _post_kernel(*refs, n_mix, final_norm, ff_chunk):
    mix_refs = refs[:n_mix]
    wmix_refs = refs[n_mix:2 * n_mix]
    (x_ref, gq_ref, wq_ref, k_ref, v_ref, wo_ref, gf_ref, w1_ref, w2_ref, gfin_ref,
     o_ref) = refs[2 * n_mix:]
    d = x_ref.shape[1]
    hd = d // X_HEADS

    x = x_ref[...]
    for m_ref, w_ref in zip(mix_refs, wmix_refs):
        x = x + _dot(m_ref[...], w_ref[...])

    q = _dot(_rms(x, gq_ref[...]).astype(BF16), wq_ref[...]).astype(BF16)
    heads = []
    for h in range(X_HEADS):
        sc = lax.dot_general(q[:, h * hd:(h + 1) * hd], k_ref[0, :, h * hd:(h + 1) * hd], NT_DIMS,
                             preferred_element_type=F32) * (hd ** -0.5)
        p = jnp.exp(sc - jnp.max(sc, axis=-1, keepdims=True))
        p = p / jnp.sum(p, axis=-1, keepdims=True)
        heads.append(_dot(p.astype(BF16), v_ref[0, :, h * hd:(h + 1) * hd]))
    x = x + _dot(jnp.concatenate(heads, axis=-1).astype(BF16), wo_ref[...])

    hn = _rms(x, gf_ref[...]).astype(BF16)
    y = x
    for c in range(0, w1_ref.shape[1], ff_chunk):
        hid = jnp.square(jnp.maximum(_dot(hn, w1_ref[:, c:c + ff_chunk]), 0.0))
        y = y + _dot(hid.astype(BF16), w2_ref[c:c + ff_chunk, :])
    if final_norm:
        y = _rms(y, gfin_ref[...])
    o_ref[...] = y


def _post(mixes, wmix_specs, wmix_arrays, x2d, gq, k, v, gf, gfin, attn_ffn, layer, final_norm,
          bsz, seq, tm=TOKEN_TILE, ff_chunk=1024):
    t, d = x2d.shape
    nt = seq // tm
    mem_len = k.shape[1]
    wq, wo, w1, w2 = attn_ffn
    tile = lambda w: pl.BlockSpec((tm, w), lambda b, i: (b * nt + i, 0))
    mem = lambda: pl.BlockSpec((1, mem_len, d), lambda b, i: (b, 0, 0))
    in_specs = [tile(m.shape[1]) for m in mixes] + list(wmix_specs)
    in_specs += [tile(d), _resident((1, d)), _layer_resident(wq, layer), mem(), mem(),
                 _layer_resident(wo, layer), _resident((1, d)), _layer_resident(w1, layer),
                 _layer_resident(w2, layer), _resident((1, d))]
    return pl.pallas_call(
        functools.partial(_post_kernel, n_mix=len(mixes), final_norm=final_norm,
                          ff_chunk=ff_chunk),
        grid=(bsz, nt), in_specs=in_specs, out_specs=tile(d),
        out_shape=jax.ShapeDtypeStruct((t, d), F32),
        compiler_params=_params(2), name="post",
    )(*mixes, *wmix_arrays, x2d, gq.reshape(1, d), wq, k, v, wo, gf.reshape(1, d), w1, w2,
      gfin.reshape(1, d))


def kernel(x, mem, norm_mix_g, norm_mem_q_g, norm_mem_kv_g, norm_ffn_g, final_norm_g, e_w_in, a_conv_w, a_conv_b, a_gate_r_w, a_gate_r_b, a_gate_i_w, a_gate_i_b, a_lambda, b_lb_logits, b_norm_g, e_w_out, o_w_in, m_conv_w, m_conv_b, m_dt_bias, m_a_log, m_d, m_norm_g, o_w_out, xq_w, xk_w, xv_w, xo_w, ffn_w1, ffn_w2):
    bsz, seq, d = x.shape
    mem_len = mem.shape[1]
    depth = norm_mix_g.shape[0]
    t = bsz * seq
    bf = lambda w: w.astype(BF16)
    xs = x.reshape(t, d)
    mem2d = mem.reshape(bsz * mem_len, d)

    attn_ffn = (bf(xq_w), bf(xo_w), bf(ffn_w1), bf(ffn_w2))
    xk, xv = bf(xk_w), bf(xv_w)
    all_cols = lambda n: tuple(range(0, n, 1024))
    for l in range(depth):
        k, v = _inproj(mem2d, norm_mem_kv_g[l], [(xk, l), (xv, l)],
                       [("nn", 0, all_cols(d)), ("nn", 1, all_cols(d))], [BF16, BF16])
        k, v = k.reshape(bsz, mem_len, d), v.reshape(bsz, mem_len, d)
        if l % 2 == 0:
            e = l // 2
            wa = d
            proj, fproj = _inproj(xs, norm_mix_g[l], [(bf(e_w_in), e)],
                                  [("nn", 0, (0, wa, 2 * wa, 4 * wa, 5 * wa)), ("nn", 0, (3 * wa,))],
                                  [BF16, F32])
            wg = bf(0.5 * jnp.concatenate([a_gate_r_w[e], a_gate_i_w[e]], axis=-1))
            ya = _rglru(proj, a_conv_w[e], a_conv_b[e], wg, 0.5 * a_gate_r_b[e],
                        0.5 * a_gate_i_b[e], a_lambda[e], bsz, seq)
            blk = wa // LANES
            yb = _hgrn2(proj, fproj, b_lb_logits, b_norm_g[e], l, bsz, seq,
                        (2 * blk, 3 * blk, 4 * blk))
            mixes = [ya, yb]
            w_out = bf(e_w_out)
            wmix_specs = [_layer_resident(w_out, e, (0, wa)), _layer_resident(w_out, e, (1, wa))]
            wmix_arrays = [w_out, w_out]
        else:
            o = l // 2
            n_main = o_w_in.shape[-1] - M_GROUPS * M_HPG
            w_dt = o_w_in[o][:, n_main:].T.reshape(M_GROUPS, M_HPG, d)
            w_dt = jnp.pad(w_dt, ((0, 0), (0, SUBLANES - M_HPG), (0, 0))).reshape(1, -1, d)
            proj, dt_t = _inproj(xs, norm_mix_g[l], [(bf(o_w_in), o), (bf(w_dt), 0)],
                                 [("nn", 0, all_cols(n_main)), ("nt", 1, ())], [BF16, F32])
            ym = _ssd(proj, dt_t, m_conv_w[o], m_conv_b[o], m_dt_bias[o], m_a_log[o], m_d[o],
                      m_norm_g[o], bsz, seq)
            mixes = [ym]
            w_out = bf(o_w_out)
            wmix_specs = [_layer_resident(w_out, o)]
            wmix_arrays = [w_out]
        xs = _post(mixes, wmix_specs, wmix_arrays, xs, norm_mem_q_g[l], k, v, norm_ffn_g[l],
                   final_norm_g, attn_ffn, l, l == depth - 1, bsz, seq)
    return xs.reshape(bsz, seq, d)
```

```python
import functools

import jax
import jax.numpy as jnp
import numpy as np
from jax import lax
from jax.experimental import pallas as pl
from jax.experimental.pallas import tpu as pltpu

F32 = jnp.float32
BF16 = jnp.bfloat16

NORM_EPS = 1e-6
LOG2_E = 1.4426950408889634
CONV_W = 4
LANES = 128
SUBLANES = 8
VMEM_LIMIT_BYTES = 56 * 1024 * 1024

A_HEADS = 8
LRU_C = 8.0
B_HEADS = 8
M_HEAD_DIM = 64
M_GROUPS = 8
M_HPG = 4
D_STATE = 128
X_HEADS = 4
CHUNK = 128

TOKEN_TILE = 512
LRU_TILE = 256
MIX_TILE = 1024
HGRN2_HEADS_PER_STEP = 4
SSD_GROUPS_PER_STEP = 4

NT_DIMS = (((1,), (1,)), ((), ()))
TN_DIMS = (((0,), (0,)), ((), ()))


def _params(n_axes):
    return pltpu.CompilerParams(
        dimension_semantics=("arbitrary",) * n_axes, vmem_limit_bytes=VMEM_LIMIT_BYTES)


def _resident(shape):
    nd = len(shape)
    return pl.BlockSpec(shape, lambda *_: (0,) * nd, pipeline_mode=pl.Buffered(1))


def _rms(x, g):
    return x * lax.rsqrt(jnp.mean(x * x, axis=-1, keepdims=True) + NORM_EPS) * g


def _sigmoid(x):
    return 0.5 * jnp.tanh(0.5 * x) + 0.5


def _silu(x):
    h = 0.5 * x
    return h * jnp.tanh(h) + h


def _softplus(x):
    return jnp.maximum(x, 0.0) + jnp.log1p(jnp.exp(-jnp.abs(x)))


def _dot(a, b):
    return jnp.dot(a, b, preferred_element_type=F32)


def _dot_f32(a, b):
    return jnp.dot(a, b, preferred_element_type=F32, precision=lax.Precision.HIGHEST)


def _split3(x):
    hi = x.astype(BF16)
    r = x - hi.astype(F32)
    mid = r.astype(BF16)
    lo = (r - mid.astype(F32)).astype(BF16)
    return hi, mid, lo


def _layer_resident(w, layer, rows=None):
    r0, nr = (0, w.shape[1]) if rows is None else rows
    return pl.BlockSpec((None, nr, w.shape[2]), lambda *_: (layer, r0, 0),
                        pipeline_mode=pl.Buffered(1))


def _inproj_kernel(x_ref, g_ref, *refs, n_w, outs, col_chunk):
    w_refs, o_refs = refs[:n_w], refs[n_w:]
    xn = _rms(x_ref[...], g_ref[...]).astype(BF16)
    for (kind, wi, cols), o_ref in zip(outs, o_refs):
        w_ref = w_refs[wi]
        if kind == "nn":
            for j, c in enumerate(cols):
                o_ref[:, j * col_chunk:(j + 1) * col_chunk] = _dot(
                    xn, w_ref[:, c:c + col_chunk]).astype(o_ref.dtype)
        else:
            y = lax.dot_general(w_ref[...], xn, NT_DIMS, preferred_element_type=F32)
            for gi in range(y.shape[0] // SUBLANES):
                for ci in range(y.shape[1] // LANES):
                    o_ref[gi, ci] = y[gi * SUBLANES:(gi + 1) * SUBLANES,
                                      ci * LANES:(ci + 1) * LANES].astype(o_ref.dtype)


def _inproj(x2d, g, weights, outs, out_dtypes, tm=TOKEN_TILE, col_chunk=1024):
    t, d = x2d.shape
    tm = min(tm, t)
    assert t % tm == 0
    in_specs = [pl.BlockSpec((tm, d), lambda i: (i, 0)), _resident((1, d))]
    in_specs += [_layer_resident(w, l) for w, l in weights]
    out_specs, out_shapes = [], []
    for (kind, wi, cols), dt in zip(outs, out_dtypes):
        if kind == "nn":
            n = len(cols) * col_chunk
            out_specs.append(pl.BlockSpec((tm, n), lambda i: (i, 0)))
            out_shapes.append(jax.ShapeDtypeStruct((t, n), dt))
        else:
            rb = weights[wi][0].shape[1] // SUBLANES
            out_specs.append(pl.BlockSpec((rb, tm // LANES, SUBLANES, LANES),
                                          lambda i: (0, i, 0, 0)))
            out_shapes.append(jax.ShapeDtypeStruct((rb, t // LANES, SUBLANES, LANES), dt))
    return pl.pallas_call(
        functools.partial(_inproj_kernel, n_w=len(weights), outs=tuple(outs),
                          col_chunk=col_chunk),
        grid=(t // tm,), in_specs=in_specs, out_specs=out_specs, out_shape=out_shapes,
        compiler_params=_params(1), name="inproj",
    )(x2d, g.reshape(1, d), *[w for w, _ in weights])


def _conv_silu(src_ref, buf, w_ref, b_ref, dst_ref, rows=CHUNK):
    ts = src_ref.shape[0]
    buf[SUBLANES:SUBLANES + ts, :] = src_ref[...].astype(F32)
    for r0 in range(0, ts, rows):
        y = b_ref[...] + w_ref[CONV_W - 1:CONV_W, :] * buf[SUBLANES + r0:SUBLANES + r0 + rows, :]
        for k in range(CONV_W - 1):
            off = SUBLANES - (CONV_W - 1) + k + r0
            y = y + w_ref[k:k + 1, :] * buf[off:off + rows, :]
        dst_ref[r0:r0 + rows, :] = _silu(y).astype(dst_ref.dtype)
    buf[0:SUBLANES, :] = buf[ts:ts + SUBLANES, :]


def _segment_permutation(ts):
    seg = ts // SUBLANES
    p = np.zeros((ts, ts), np.float32)
    for j in range(seg):
        for s in range(SUBLANES):
            p[SUBLANES * j + s, seg * s + j] = 1.0
    return p


def _segment_conv(xa, xe_s, tail, w, b):
    ts, width = xa.shape
    halo = (CONV_W - 1) * SUBLANES
    first_sublane = lax.broadcasted_iota(jnp.int32, (SUBLANES, width), 0) == 0
    for i in range(CONV_W - 1):
        blk = xa[ts - halo + i * SUBLANES:ts - halo + (i + 1) * SUBLANES, :]
        xe_s[i * SUBLANES:(i + 1) * SUBLANES, :] = jnp.where(
            first_sublane, tail[i:i + 1, :], pltpu.roll(blk, 1, 0))
        tail[i:i + 1, :] = blk[SUBLANES - 1:SUBLANES, :]
    xe_s[halo:halo + ts, :] = xa
    conv = b[...] + w[CONV_W - 1:CONV_W, :] * xa
    for k in range(CONV_W - 1):
        conv = conv + w[k:k + 1, :] * xe_s[k * SUBLANES:k * SUBLANES + ts, :]
    return conv


def _rglru_kernel(xa_ref, ga_ref, perm_ref, cw_ref, cb_ref, wg_ref, br_ref, bi_ref, lam_ref, o_ref,
                  xe_s, a_s, u_s, h_s, ap_s, tail, hc):
    ts, width = xa_ref.shape
    hd = width // A_HEADS
    seg = ts // SUBLANES

    @pl.when(pl.program_id(1) == 0)
    def _():
        tail[...] = jnp.zeros_like(tail)
        hc[...] = jnp.zeros_like(hc)

    perm = perm_ref[...]
    xa = _dot(perm, xa_ref[...])
    ga = _dot(perm, ga_ref[...])
    conv = _segment_conv(xa, xe_s, tail, cw_ref, cb_ref)

    convb = conv.astype(BF16)
    r_pre, i_pre = [], []
    for h in range(A_HEADS):
        gate = _dot(convb[:, h * hd:(h + 1) * hd], wg_ref[h])
        r_pre.append(gate[:, :hd])
        i_pre.append(gate[:, hd:])
    tr = jnp.tanh(jnp.concatenate(r_pre, axis=-1) + br_ref[...])
    ti = jnp.tanh(jnp.concatenate(i_pre, axis=-1) + bi_ref[...])
    half_c = (-0.5 * LRU_C) * _softplus(-lam_ref[...])
    a = jnp.exp(half_c * tr + half_c)
    a_s[...] = a
    v = 1.0 - a * a
    half_conv = 0.5 * conv
    u_s[...] = (v * lax.rsqrt(jnp.maximum(v, 1e-30))) * (half_conv * ti + half_conv)

    h = jnp.zeros((SUBLANES, width), F32)
    ap = jnp.ones((SUBLANES, width), F32)
    for j in range(seg):
        rows = slice(j * SUBLANES, (j + 1) * SUBLANES)
        aj = a_s[rows, :]
        h = aj * h + u_s[rows, :]
        ap = aj * ap
        h_s[rows, :] = h
        ap_s[rows, :] = ap

    sub = lax.broadcasted_iota(jnp.int32, (SUBLANES, width), 0)
    for sh in (1, 2, 4):
        keep = sub >= sh
        h = jnp.where(keep, ap * pltpu.roll(h, sh, 0) + h, h)
        ap = jnp.where(keep, ap * pltpu.roll(ap, sh, 0), ap)
    leave = h + ap * hc[0:1, :]
    enter = jnp.where(sub == 0, hc[0:1, :], pltpu.roll(leave, 1, 0))
    hc[0:1, :] = leave[SUBLANES - 1:SUBLANES, :]

    enter_t = jnp.concatenate([enter] * seg, axis=0)
    y = ((h_s[...] + ap_s[...] * enter_t) * jax.nn.gelu(ga)).astype(BF16)
    o_ref[...] = lax.dot_general(perm, y, TN_DIMS, preferred_element_type=F32).astype(o_ref.dtype)


def _rglru(proj, conv_w, conv_b, wg, b_r, b_i, lam, bsz, seq, ts=LRU_TILE):
    width = conv_w.shape[1]
    nt = seq // ts
    row = lambda b, t: b * nt + t
    vec = lambda: _resident((1, width))
    perm = jnp.asarray(_segment_permutation(ts), BF16)
    tile = lambda: pltpu.VMEM((ts, width), F32)
    return pl.pallas_call(
        _rglru_kernel,
        grid=(bsz, nt),
        in_specs=[pl.BlockSpec((ts, width), lambda b, t: (row(b, t), 0)),
                  pl.BlockSpec((ts, width), lambda b, t: (row(b, t), 1)),
                  _resident(perm.shape), _resident(conv_w.shape), vec(), _resident(wg.shape),
                  vec(), vec(), vec()],
        out_specs=pl.BlockSpec((ts, width), lambda b, t: (row(b, t), 0)),
        out_shape=jax.ShapeDtypeStruct((bsz * seq, width), BF16),
        scratch_shapes=[pltpu.VMEM((ts + (CONV_W - 1) * SUBLANES, width), F32),
                        tile(), tile(), tile(), tile(),
                        pltpu.VMEM((SUBLANES, width), F32), pltpu.VMEM((SUBLANES, width), F32)],
        compiler_params=_params(2), name="rglru",
    )(proj, proj, perm, conv_w, conv_b.reshape(1, width), wg, b_r.reshape(1, width),
      b_i.reshape(1, width), lam.reshape(1, width))


def _pair_level(n):
    t = lax.broadcasted_iota(jnp.int32, (n, n), 0)
    s = lax.broadcasted_iota(jnp.int32, (n, n), 1)
    x = t ^ s
    lvl = jnp.zeros((n, n), jnp.int32)
    w = 1
    while w < n:
        lvl = lvl + (x >= w).astype(jnp.int32)
        w *= 2
    return jnp.where(s > t, -1, lvl)


def _hgrn2_kernel(q_ref, f_ref, v_ref, g_ref, lbl_ref, ng_ref, o_ref, st, bc_s, f_s, *, layer):
    ts, dk = q_ref.shape[0], LANES
    c = CHUNK
    nck = ts // c
    heads = q_ref.shape[1] // dk

    @pl.when(pl.program_id(2) == 0)
    def _():
        st[...] = jnp.zeros_like(st)

    lg = lbl_ref[...]
    e = jnp.exp(lg - jnp.max(lg, axis=0, keepdims=True))
    lb = jnp.sum(e[0:layer + 1, :], axis=0, keepdims=True) / jnp.sum(e, axis=0, keepdims=True)

    row = lax.broadcasted_iota(jnp.int32, (c, dk), 0)
    tt = lax.broadcasted_iota(jnp.int32, (c, c), 0)
    ss = lax.broadcasted_iota(jnp.int32, (c, c), 1)
    tri = (ss <= tt).astype(BF16)
    lvl = _pair_level(c)
    n_lvl = c.bit_length()
    lvl_b = lvl.astype(F32).astype(BF16)
    right_b = [None] + [((row >> (p - 1)) & 1).astype(F32).astype(BF16) for p in range(1, n_lvl)]
    side = [None] + [jnp.where((row & (1 << (p - 1))) != 0, 1.0, -1.0) for p in range(1, n_lvl)]
    r4 = row & 3

    bodies = [(ci, hh) for ci in range(nck) for hh in range(heads)]
    for ci, hh in bodies:
        cols = slice(hh * dk, (hh + 1) * dk)
        lbh = lb[:, cols]
        f = lbh + (1.0 - lbh) * _sigmoid(f_ref[ci * c:(ci + 1) * c, cols])
        f_s[hh * nck + ci] = f
        bc_s[hh * nck + ci] = LOG2_E * sum(_dot(tri, part) for part in _split3(jnp.log(f)))

    for ci, hh in bodies:
        r0 = ci * c
        cols = slice(hh * dk, (hh + 1) * dk)
        qf = _silu(q_ref[r0:r0 + c, cols].astype(F32))
        f = f_s[hh * nck + ci]
        kk = 1.0 - f
        qb, kb = qf.astype(BF16), kk.astype(BF16)
        bc = bc_s[hh * nck + ci]
        vb = v_ref[r0:r0 + c, cols]

        scores = jnp.where(lvl_b == 0,
                           lax.dot_general(qb, kb, NT_DIMS,
                                           preferred_element_type=F32).astype(BF16), 0)
        for p in range(1, n_lvl):
            w = 1 << (p - 1)
            if w == 1:
                dec = jnp.where(right_b[p] > 0, f.astype(BF16), 1)
            elif w == 2:
                dec = jnp.where(r4 == 0, pltpu.roll(f, c - 1, 0),
                                jnp.where(r4 == 1, 1.0,
                                          jnp.where(r4 == 2, f, f * pltpu.roll(f, 1, 0)))
                                ).astype(BF16)
            else:
                gat = jnp.concatenate(
                    [jnp.broadcast_to(bc_s[hh * nck + ci, b * 2 * w + w - 1:b * 2 * w + w, :],
                                      (2 * w, dk)) for b in range(c // (2 * w))], axis=0)
                dec = jnp.exp2((bc - gat) * side[p]).astype(BF16)
            z = jnp.where(right_b[p] > 0, qb, kb) * dec
            scores = jnp.where(lvl_b == p,
                               lax.dot_general(z, z, NT_DIMS,
                                               preferred_element_type=F32).astype(BF16), scores)
        o = _dot(scores, vb)

        b_last = bc[c - 1:c, :]
        o = o + lax.dot_general((qf * jnp.exp2(bc)).astype(BF16), st[hh].astype(BF16), NT_DIMS,
                                preferred_element_type=F32)
        kd = (kk * jnp.exp2(b_last - bc)).astype(BF16)
        st[hh] = st[hh] * jnp.exp2(b_last) + lax.dot_general(vb, kd, TN_DIMS,
                                                             preferred_element_type=F32)

        o = o * lax.rsqrt(jnp.mean(o * o, axis=-1, keepdims=True) + NORM_EPS)
        gv = g_ref[r0:r0 + c, cols].astype(F32)
        o_ref[r0:r0 + c, cols] = (o * ng_ref[:, cols] * _silu(gv)).astype(o_ref.dtype)


def _hgrn2(proj, fproj, lb_logits, norm_g, layer, bsz, seq, col0, ts=MIX_TILE,
           heads=HGRN2_HEADS_PER_STEP):
    dk = LANES
    wd = heads * dk
    width = fproj.shape[1]
    nt = seq // ts
    q0, v0, g0 = (o // heads for o in col0)
    row = lambda b, t: b * nt + t
    n_layers = lb_logits.shape[0]
    return pl.pallas_call(
        functools.partial(_hgrn2_kernel, layer=layer),
        grid=(bsz, B_HEADS // heads, nt),
        in_specs=[pl.BlockSpec((ts, wd), lambda b, h, t: (row(b, t), q0 + h)),
                  pl.BlockSpec((ts, wd), lambda b, h, t: (row(b, t), h)),
                  pl.BlockSpec((ts, wd), lambda b, h, t: (row(b, t), v0 + h)),
                  pl.BlockSpec((ts, wd), lambda b, h, t: (row(b, t), g0 + h)),
                  pl.BlockSpec((n_layers, wd), lambda b, h, t: (0, h)),
                  pl.BlockSpec((1, wd), lambda b, h, t: (0, h))],
        out_specs=pl.BlockSpec((ts, wd), lambda b, h, t: (row(b, t), h)),
        out_shape=jax.ShapeDtypeStruct((bsz * seq, width), BF16),
        scratch_shapes=[pltpu.VMEM((heads, dk, dk), F32),
                        pltpu.VMEM((heads * (ts // CHUNK), CHUNK, dk), F32),
                        pltpu.VMEM((heads * (ts // CHUNK), CHUNK, dk), F32)],
        compiler_params=_params(3), name="hgrn2",
    )(proj, fproj, proj, proj, lb_logits, norm_g.reshape(1, width))


def _ssd_expand_matrices(nck):
    e = np.zeros((nck, LANES, M_HPG * LANES + M_HPG * M_HEAD_DIM), np.float32)
    for c in range(nck):
        for r in range(M_HPG):
            e[c, SUBLANES * c + r, r * LANES:(r + 1) * LANES] = 1.0
            e[c, SUBLANES * (nck + c) + r,
              M_HPG * LANES + r * M_HEAD_DIM:M_HPG * LANES + (r + 1) * M_HEAD_DIM] = 1.0
    return e


def _ssd_kernel(*refs, groups):
    n_in = 16
    (z, x, b, c, dt, cwx, cbx, cwb, cbb, cwc, cbc, bias, alog, drow, ng, e_ref) = refs[:n_in]
    o_ref, scratch = refs[n_in], refs[n_in + 1:]
    gw, n = x.shape[1] // groups, b.shape[1] // groups
    row_forms = [_ssd_row_forms(dt.at[gi], bias.at[gi], alog.at[gi]) for gi in range(groups)]
    for gi in range(groups):
        wide = (slice(None), slice(gi * gw, (gi + 1) * gw))
        narrow = (slice(None), slice(gi * n, (gi + 1) * n))
        _ssd_group(z.at[wide], x.at[wide], b.at[narrow], c.at[narrow], row_forms[gi],
                   cwx.at[wide], cbx.at[wide], cwb.at[narrow], cbb.at[narrow], cwc.at[narrow],
                   cbc.at[narrow], drow.at[wide], ng.at[wide], e_ref, o_ref.at[wide],
                   *[sc.at[gi] for sc in scratch])


def _ssd_row_forms(dt_ref, bias_ref, alog_ref):
    nck, _, c = dt_ref.shape
    tt = lax.broadcasted_iota(jnp.int32, (c, c), 0)
    ss = lax.broadcasted_iota(jnp.int32, (c, c), 1)
    upper = (tt <= ss).astype(F32)
    bias = jnp.concatenate([bias_ref[...]] * nck, axis=0)
    neg_a = jnp.concatenate([-jnp.exp(alog_ref[...])] * nck, axis=0)
    dt = _softplus(dt_ref[...].reshape(nck * SUBLANES, c) + bias)
    a_row = LOG2_E * _dot_f32(dt * neg_a, upper)
    dtw = dt * jnp.exp2(a_row[:, c - 1:c] - a_row)
    src_row = a_row - LOG2_E * jnp.log(dt)
    rows = [a_row, dtw]
    if 2 * nck * SUBLANES < c:
        rows.append(jnp.zeros((c - 2 * nck * SUBLANES, c), F32))
    return _split3(jnp.concatenate(rows, axis=0).T) + (src_row,)


def _ssd_group(z_ref, x_ref, b_ref, c_ref, row_forms, cwx_ref, cbx_ref, cwb_ref, cbb_ref,
               cwc_ref, cbc_ref, drow_ref, ng_ref, e_ref, o_ref,
               xbuf, bbuf, cbuf, xc_s, bc_s, cc_s, lhs_s, xw_s, sb_s, dec_s, st):
    ts, gw = x_ref.shape
    n = D_STATE
    c = CHUNK
    nck = ts // c
    nb = M_HPG * LANES

    @pl.when(pl.program_id(2) == 0)
    def _():
        xbuf[0:SUBLANES, :] = jnp.zeros((SUBLANES, gw), F32)
        bbuf[0:SUBLANES, :] = jnp.zeros((SUBLANES, n), F32)
        cbuf[0:SUBLANES, :] = jnp.zeros((SUBLANES, n), F32)
        st[...] = jnp.zeros_like(st)

    _conv_silu(x_ref, xbuf, cwx_ref, cbx_ref, xc_s)
    _conv_silu(b_ref, bbuf, cwb_ref, cbb_ref, bc_s)
    _conv_silu(c_ref, cbuf, cwc_ref, cbc_ref, cc_s)

    tt = lax.broadcasted_iota(jnp.int32, (c, c), 0)
    ss = lax.broadcasted_iota(jnp.int32, (c, c), 1)
    causal = ss <= tt
    low_half = lax.broadcasted_iota(jnp.int32, (1, LANES), 1) < M_HEAD_DIM
    low_lanes = lax.broadcasted_iota(jnp.int32, (c + D_STATE, LANES), 1) < M_HEAD_DIM
    hi, mid, lo, src_row = row_forms

    for ci in range(nck):
        r0 = ci * c
        e_c = e_ref[ci]
        ex = _dot(hi, e_c)
        a_col = ex[:, :nb] + _dot(mid, e_c[:, :nb]) + _dot(lo, e_c[:, :nb])
        xw_s[r0:r0 + c, :] = (xc_s[r0:r0 + c, :] * ex[:, nb:]).astype(BF16)
        cm = cc_s[r0:r0 + c, :]
        cmf = cm.astype(F32)
        cb = lax.dot_general(cm, bc_s[r0:r0 + c, :], NT_DIMS, preferred_element_type=F32)
        for r in range(M_HPG):
            a_t = a_col[:, r * LANES:(r + 1) * LANES]
            src = src_row[ci * SUBLANES + r:ci * SUBLANES + r + 1, :]
            lmat = jnp.exp2(jnp.where(causal, a_t - src, -1e30))
            lhs_s[ci * M_HPG + r] = jnp.concatenate(
                [(cb * lmat).astype(BF16), (cmf * jnp.exp2(a_t)).astype(BF16)], axis=1)
        e_last = [jnp.exp2(a_col[c - 1:c, r * LANES:(r + 1) * LANES]) for r in range(M_HPG)]
        dec_s[ci:ci + 1, :] = jnp.concatenate([jnp.where(low_half, e_last[0], e_last[1]),
                                               jnp.where(low_half, e_last[2], e_last[3])], axis=1)

    for ci in range(nck):
        r0 = ci * c
        sb_s[ci] = st[...].astype(BF16)
        st[...] = st[...] * dec_s[ci:ci + 1, :] + lax.dot_general(
            bc_s[r0:r0 + c, :], xw_s[r0:r0 + c, :], TN_DIMS, preferred_element_type=F32)

    for ci in range(nck):
        r0 = ci * c
        xc = xc_s[r0:r0 + c, :]
        rhs = jnp.concatenate([xc.astype(BF16), sb_s[ci]], axis=0)
        tiles = []
        for j in range(gw // LANES):
            rt = rhs[:, j * LANES:(j + 1) * LANES]
            tiles.append(_dot(lhs_s[ci * M_HPG + 2 * j], jnp.where(low_lanes, rt, 0))
                         + _dot(lhs_s[ci * M_HPG + 2 * j + 1], jnp.where(low_lanes, 0, rt)))
        y = drow_ref[...] * xc + jnp.concatenate(tiles, axis=1)
        y = y * _silu(z_ref[r0:r0 + c, :].astype(F32))
        y = y * lax.rsqrt(jnp.mean(y * y, axis=-1, keepdims=True) + NORM_EPS)
        o_ref[r0:r0 + c, :] = (y * ng_ref[...]).astype(o_ref.dtype)


def _ssd(proj, dt_t, conv_w, conv_b, dt_bias, a_log, d_skip, norm_g, bsz, seq, ts=MIX_TILE,
         groups=SSD_GROUPS_PER_STEP):
    gw = M_HPG * M_HEAD_DIM
    n = D_STATE
    d_inner = M_GROUPS * gw
    nt = seq // ts
    nck = ts // CHUNK
    row = lambda b, t: b * nt + t
    gp = groups
    zb, xb = 0, d_inner // (gw * gp)
    bb, cb = 2 * d_inner // (n * gp), (2 * d_inner // n + M_GROUPS) // gp
    wbb, wcb = d_inner // (n * gp), (d_inner // n + M_GROUPS) // gp
    cw2 = conv_w.reshape(CONV_W, -1)
    cb2 = conv_b.reshape(1, -1)

    def per_head(v):
        v = jnp.pad(v.reshape(M_GROUPS, M_HPG), ((0, 0), (0, SUBLANES - M_HPG)))
        return jnp.broadcast_to(v[:, :, None], (M_GROUPS, SUBLANES, LANES)).astype(F32)

    drow = jnp.repeat(d_skip.astype(F32), M_HEAD_DIM).reshape(1, d_inner)
    e_mat = jnp.asarray(_ssd_expand_matrices(nck), BF16)
    g3 = lambda b, g, t: (g, 0, 0)
    return pl.pallas_call(
        functools.partial(_ssd_kernel, groups=gp),
        grid=(bsz, M_GROUPS // gp, nt),
        in_specs=[pl.BlockSpec((ts, gp * gw), lambda b, g, t: (row(b, t), zb + g)),
                  pl.BlockSpec((ts, gp * gw), lambda b, g, t: (row(b, t), xb + g)),
                  pl.BlockSpec((ts, gp * n), lambda b, g, t: (row(b, t), bb + g)),
                  pl.BlockSpec((ts, gp * n), lambda b, g, t: (row(b, t), cb + g)),
                  pl.BlockSpec((gp, nck, SUBLANES, LANES), lambda b, g, t: (g, row(b, t), 0, 0)),
                  pl.BlockSpec((CONV_W, gp * gw), lambda b, g, t: (0, g)),
                  pl.BlockSpec((1, gp * gw), lambda b, g, t: (0, g)),
                  pl.BlockSpec((CONV_W, gp * n), lambda b, g, t: (0, wbb + g)),
                  pl.BlockSpec((1, gp * n), lambda b, g, t: (0, wbb + g)),
                  pl.BlockSpec((CONV_W, gp * n), lambda b, g, t: (0, wcb + g)),
                  pl.BlockSpec((1, gp * n), lambda b, g, t: (0, wcb + g)),
                  pl.BlockSpec((gp, SUBLANES, LANES), g3),
                  pl.BlockSpec((gp, SUBLANES, LANES), g3),
                  pl.BlockSpec((1, gp * gw), lambda b, g, t: (0, g)),
                  pl.BlockSpec((1, gp * gw), lambda b, g, t: (0, g)),
                  _resident(e_mat.shape)],
        out_specs=pl.BlockSpec((ts, gp * gw), lambda b, g, t: (row(b, t), g)),
        out_shape=jax.ShapeDtypeStruct((bsz * seq, d_inner), BF16),
        scratch_shapes=[pltpu.VMEM((gp,) + shape, dt) for shape, dt in [
            ((ts + SUBLANES, gw), F32), ((ts + SUBLANES, n), F32), ((ts + SUBLANES, n), F32),
            ((ts, gw), F32), ((ts, n), BF16), ((ts, n), BF16),
            ((nck * M_HPG, CHUNK, CHUNK + n), BF16), ((ts, gw), BF16), ((nck, n, gw), BF16),
            ((max(nck, SUBLANES), gw), F32), ((n, gw), F32)]],
        compiler_params=_params(3), name="ssd",
    )(proj, proj, proj, proj, dt_t, cw2, cb2, cw2, cb2, cw2, cb2,
      per_head(dt_bias), per_head(a_log), drow, norm_g.reshape(1, d_inner), e_mat)


def _post_kernel(*refs, n_mix, final_norm, ff_chunk):
    mix_refs = refs[:n_mix]
    wmix_refs = refs[n_mix:2 * n_mix]
    (x_ref, gq_ref, wq_ref, k_ref, v_ref, wo_ref, gf_ref, w1_ref, w2_ref, gfin_ref,
     o_ref) = refs[2 * n_mix:]
    d = x_ref.shape[1]
    hd = d // X_HEADS

    x = x_ref[...]
    for m_ref, w_ref in zip(mix_refs, wmix_refs):
        x = x + _dot(m_ref[...], w_ref[...])

    q = _dot(_rms(x, gq_ref[...]).astype(BF16), wq_ref[...]).astype(BF16)
    heads = []
    for h in range(X_HEADS):
        sc = lax.dot_general(q[:, h * hd:(h + 1) * hd], k_ref[0, :, h * hd:(h + 1) * hd], NT_DIMS,
                             preferred_element_type=F32) * (hd ** -0.5)
        p = jnp.exp(sc - jnp.max(sc, axis=-1, keepdims=True))
        p = p / jnp.sum(p, axis=-1, keepdims=True)
        heads.append(_dot(p.astype(BF16), v_ref[0, :, h * hd:(h + 1) * hd]))
    x = x + _dot(jnp.concatenate(heads, axis=-1).astype(BF16), wo_ref[...])

    hn = _rms(x, gf_ref[...]).astype(BF16)
    y = x
    for c in range(0, w1_ref.shape[1], ff_chunk):
        hid = jnp.square(jnp.maximum(_dot(hn, w1_ref[:, c:c + ff_chunk]), 0.0))
        y = y + _dot(hid.astype(BF16), w2_ref[c:c + ff_chunk, :])
    if final_norm:
        y = _rms(y, gfin_ref[...])
    o_ref[...] = y


def _post(mixes, wmix_specs, wmix_arrays, x2d, gq, k, v, gf, gfin, attn_ffn, layer, final_norm,
          bsz, seq, tm=TOKEN_TILE, ff_chunk=1024):
    t, d = x2d.shape
    nt = seq // tm
    mem_len = k.shape[1]
    wq, wo, w1, w2 = attn_ffn
    tile = lambda w: pl.BlockSpec((tm, w), lambda b, i: (b * nt + i, 0))
    mem = lambda: pl.BlockSpec((1, mem_len, d), lambda b, i: (b, 0, 0))
    in_specs = [tile(m.shape[1]) for m in mixes] + list(wmix_specs)
    in_specs += [tile(d), _resident((1, d)), _layer_resident(wq, layer), mem(), mem(),
                 _layer_resident(wo, layer), _resident((1, d)), _layer_resident(w1, layer),
                 _layer_resident(w2, layer), _resident((1, d))]
    return pl.pallas_call(
        functools.partial(_post_kernel, n_mix=len(mixes), final_norm=final_norm,
                          ff_chunk=ff_chunk),
        grid=(bsz, nt), in_specs=in_specs, out_specs=tile(d),
        out_shape=jax.ShapeDtypeStruct((t, d), F32),
        compiler_params=_params(2), name="post",
    )(*mixes, *wmix_arrays, x2d, gq.reshape(1, d), wq, k, v, wo, gf.reshape(1, d), w1, w2,
      gfin.reshape(1, d))


def kernel(x, mem, norm_mix_g, norm_mem_q_g, norm_mem_kv_g, norm_ffn_g, final_norm_g, e_w_in, a_conv_w, a_conv_b, a_gate_r_w, a_gate_r_b, a_gate_i_w, a_gate_i_b, a_lambda, b_lb_logits, b_norm_g, e_w_out, o_w_in, m_conv_w, m_conv_b, m_dt_bias, m_a_log, m_d, m_norm_g, o_w_out, xq_w, xk_w, xv_w, xo_w, ffn_w1, ffn_w2):
    bsz, seq, d = x.shape
    mem_len = mem.shape[1]
    depth = norm_mix_g.shape[0]
    t = bsz * seq
    bf = lambda w: w.astype(BF16)
    xs = x.reshape(t, d)
    mem2d = mem.reshape(bsz * mem_len, d)

    attn_ffn = (bf(xq_w), bf(xo_w), bf(ffn_w1), bf(ffn_w2))
    xk, xv = bf(xk_w), bf(xv_w)
    all_cols = lambda n: tuple(range(0, n, 1024))
    for l in range(depth):
        k, v = _inproj(mem2d, norm_mem_kv_g[l], [(xk, l), (xv, l)],
                       [("nn", 0, all_cols(d)), ("nn", 1, all_cols(d))], [BF16, BF16])
        k, v = k.reshape(bsz, mem_len, d), v.reshape(bsz, mem_len, d)
        if l % 2 == 0:
            e = l // 2
            wa = d
            proj, fproj = _inproj(xs, norm_mix_g[l], [(bf(e_w_in), e)],
                                  [("nn", 0, (0, wa, 2 * wa, 4 * wa, 5 * wa)), ("nn", 0, (3 * wa,))],
                                  [BF16, F32])
            wg = bf(0.5 * jnp.concatenate([a_gate_r_w[e], a_gate_i_w[e]], axis=-1))
            ya = _rglru(proj, a_conv_w[e], a_conv_b[e], wg, 0.5 * a_gate_r_b[e],
                        0.5 * a_gate_i_b[e], a_lambda[e], bsz, seq)
            blk = wa // LANES
            yb = _hgrn2(proj, fproj, b_lb_logits, b_norm_g[e], l, bsz, seq,
                        (2 * blk, 3 * blk, 4 * blk))
            mixes = [ya, yb]
            w_out = bf(e_w_out)
            wmix_specs = [_layer_resident(w_out, e, (0, wa)), _layer_resident(w_out, e, (1, wa))]
            wmix_arrays = [w_out, w_out]
        else:
            o = l // 2
            n_main = o_w_in.shape[-1] - M_GROUPS * M_HPG
            w_dt = o_w_in[o][:, n_main:].T.reshape(M_GROUPS, M_HPG, d)
            w_dt = jnp.pad(w_dt, ((0, 0), (0, SUBLANES - M_HPG), (0, 0))).reshape(1, -1, d)
            proj, dt_t = _inproj(xs, norm_mix_g[l], [(bf(o_w_in), o), (bf(w_dt), 0)],
                                 [("nn", 0, all_cols(n_main)), ("nt", 1, ())], [BF16, F32])
            ym = _ssd(proj, dt_t, m_conv_w[o], m_conv_b[o], m_dt_bias[o], m_a_log[o], m_d[o],
                      m_norm_g[o], bsz, seq)
            mixes = [ym]
            w_out = bf(o_w_out)
            wmix_specs = [_layer_resident(w_out, o)]
            wmix_arrays = [w_out]
        xs = _post(mixes, wmix_specs, wmix_arrays, xs, norm_mem_q_g[l], k, v, norm_ffn_g[l],
                   final_norm_g, attn_ffn, l, l == depth - 1, bsz, seq)
    return xs.reshape(bsz, seq, d)
```

```python
import functools

import jax
import jax.numpy as jnp
import numpy as np
from jax import lax
from jax.experimental import pallas as pl
from jax.experimental.pallas import tpu as pltpu

F32 = jnp.float32
BF16 = jnp.bfloat16

NORM_EPS = 1e-6
LOG2_E = 1.4426950408889634
CONV_W = 4
LANES = 128
SUBLANES = 8
VMEM_LIMIT_BYTES = 56 * 1024 * 1024

A_HEADS = 8
LRU_C = 8.0
B_HEADS = 8
M_HEAD_DIM = 64
M_GROUPS = 8
M_HPG = 4
D_STATE = 128
X_HEADS = 4
CHUNK = 128

TOKEN_TILE = 512
LRU_TILE = 256
LRU_SEQS_PER_STEP = 4
MIX_TILE = 1024
HGRN2_HEADS_PER_STEP = 4
SSD_GROUPS_PER_STEP = 4

NT_DIMS = (((1,), (1,)), ((), ()))
TN_DIMS = (((0,), (0,)), ((), ()))


def _params(n_axes):
    return pltpu.CompilerParams(
        dimension_semantics=("arbitrary",) * n_axes, vmem_limit_bytes=VMEM_LIMIT_BYTES)


def _resident(shape):
    nd = len(shape)
    return pl.BlockSpec(shape, lambda *_: (0,) * nd, pipeline_mode=pl.Buffered(1))


def _rms(x, g):
    return x * lax.rsqrt(jnp.mean(x * x, axis=-1, keepdims=True) + NORM_EPS) * g


def _sigmoid(x):
    return 0.5 * jnp.tanh(0.5 * x) + 0.5


def _silu(x):
    h = 0.5 * x
    return h * jnp.tanh(h) + h


def _softplus(x):
    return jnp.maximum(x, 0.0) + jnp.log1p(jnp.exp(-jnp.abs(x)))


def _dot(a, b):
    return jnp.dot(a, b, preferred_element_type=F32)


def _dot_f32(a, b):
    return jnp.dot(a, b, preferred_element_type=F32, precision=lax.Precision.HIGHEST)


def _split3(x):
    hi = x.astype(BF16)
    r = x - hi.astype(F32)
    mid = r.astype(BF16)
    lo = (r - mid.astype(F32)).astype(BF16)
    return hi, mid, lo


def _layer_resident(w, layer, rows=None):
    r0, nr = (0, w.shape[1]) if rows is None else rows
    return pl.BlockSpec((None, nr, w.shape[2]), lambda *_: (layer, r0, 0),
                        pipeline_mode=pl.Buffered(1))


def _inproj_kernel(x_ref, g_ref, *refs, n_w, outs, col_chunk):
    w_refs, o_refs = refs[:n_w], refs[n_w:]
    xn = _rms(x_ref[...], g_ref[...]).astype(BF16)
    for (kind, wi, cols), o_ref in zip(outs, o_refs):
        w_ref = w_refs[wi]
        if kind == "nn":
            for j, c in enumerate(cols):
                o_ref[:, j * col_chunk:(j + 1) * col_chunk] = _dot(
                    xn, w_ref[:, c:c + col_chunk]).astype(o_ref.dtype)
        else:
            y = lax.dot_general(w_ref[...], xn, NT_DIMS, preferred_element_type=F32)
            for gi in range(y.shape[0] // SUBLANES):
                for ci in range(y.shape[1] // LANES):
                    o_ref[gi, ci] = y[gi * SUBLANES:(gi + 1) * SUBLANES,
                                      ci * LANES:(ci + 1) * LANES].astype(o_ref.dtype)


def _inproj(x2d, g, weights, outs, out_dtypes, tm=TOKEN_TILE, col_chunk=1024):
    t, d = x2d.shape
    tm = min(tm, t)
    assert t % tm == 0
    in_specs = [pl.BlockSpec((tm, d), lambda i: (i, 0)), _resident((1, d))]
    in_specs += [_layer_resident(w, l) for w, l in weights]
    out_specs, out_shapes = [], []
    for (kind, wi, cols), dt in zip(outs, out_dtypes):
        if kind == "nn":
            n = len(cols) * col_chunk
            out_specs.append(pl.BlockSpec((tm, n), lambda i: (i, 0)))
            out_shapes.append(jax.ShapeDtypeStruct((t, n), dt))
        else:
            rb = weights[wi][0].shape[1] // SUBLANES
            out_specs.append(pl.BlockSpec((rb, tm // LANES, SUBLANES, LANES),
                                          lambda i: (0, i, 0, 0)))
            out_shapes.append(jax.ShapeDtypeStruct((rb, t // LANES, SUBLANES, LANES), dt))
    return pl.pallas_call(
        functools.partial(_inproj_kernel, n_w=len(weights), outs=tuple(outs),
                          col_chunk=col_chunk),
        grid=(t // tm,), in_specs=in_specs, out_specs=out_specs, out_shape=out_shapes,
        compiler_params=_params(1), name="inproj",
    )(x2d, g.reshape(1, d), *[w for w, _ in weights])


def _conv_silu(src_ref, buf, w_ref, b_ref, dst_ref, rows=CHUNK):
    ts = src_ref.shape[0]
    buf[SUBLANES:SUBLANES + ts, :] = src_ref[...].astype(F32)
    for r0 in range(0, ts, rows):
        y = b_ref[...] + w_ref[CONV_W - 1:CONV_W, :] * buf[SUBLANES + r0:SUBLANES + r0 + rows, :]
        for k in range(CONV_W - 1):
            off = SUBLANES - (CONV_W - 1) + k + r0
            y = y + w_ref[k:k + 1, :] * buf[off:off + rows, :]
        dst_ref[r0:r0 + rows, :] = _silu(y).astype(dst_ref.dtype)
    buf[0:SUBLANES, :] = buf[ts:ts + SUBLANES, :]


def _segment_permutation(ts):
    seg = ts // SUBLANES
    p = np.zeros((ts, ts), np.float32)
    for j in range(seg):
        for s in range(SUBLANES):
            p[SUBLANES * j + s, seg * s + j] = 1.0
    return p


def _segment_conv(xa, xe_s, tail, w, b):
    ts, width = xa.shape
    halo = (CONV_W - 1) * SUBLANES
    first_sublane = lax.broadcasted_iota(jnp.int32, (SUBLANES, width), 0) == 0
    for i in range(CONV_W - 1):
        blk = xa[ts - halo + i * SUBLANES:ts - halo + (i + 1) * SUBLANES, :]
        xe_s[i * SUBLANES:(i + 1) * SUBLANES, :] = jnp.where(
            first_sublane, tail[i:i + 1, :], pltpu.roll(blk, 1, 0))
        tail[i:i + 1, :] = blk[SUBLANES - 1:SUBLANES, :]
    xe_s[halo:halo + ts, :] = xa
    conv = b[...] + w[CONV_W - 1:CONV_W, :] * xa
    for k in range(CONV_W - 1):
        conv = conv + w[k:k + 1, :] * xe_s[k * SUBLANES:k * SUBLANES + ts, :]
    return conv


def _rglru_kernel(xa_ref, ga_ref, *refs):
    n_par = 7
    params, o_ref, scratch = refs[:n_par], refs[n_par], refs[n_par + 1:]
    tiles = [_rglru_tile(xa_ref.at[bi], ga_ref.at[bi], *params, o_ref.at[bi],
                         *[sc.at[bi] for sc in scratch]) for bi in range(xa_ref.shape[0])]
    for _ in zip(*tiles):
        pass


def _rglru_tile(xa_ref, ga_ref, perm_ref, cw_ref, cb_ref, wg_ref, br_ref, bi_ref, lam_ref, o_ref,
                xe_s, a_s, u_s, h_s, ap_s, tail, hc):
    ts, width = xa_ref.shape
    hd = width // A_HEADS
    seg = ts // SUBLANES

    @pl.when(pl.program_id(1) == 0)
    def _():
        tail[...] = jnp.zeros_like(tail)
        hc[...] = jnp.zeros_like(hc)

    perm = perm_ref[...]
    xa = _dot(perm, xa_ref[...])
    ga = _dot(perm, ga_ref[...])
    conv = _segment_conv(xa, xe_s, tail, cw_ref, cb_ref)

    yield
    convb = conv.astype(BF16)
    r_pre, i_pre = [], []
    for h in range(A_HEADS):
        gate = _dot(convb[:, h * hd:(h + 1) * hd], wg_ref[h])
        r_pre.append(gate[:, :hd])
        i_pre.append(gate[:, hd:])
    tr = jnp.tanh(jnp.concatenate(r_pre, axis=-1) + br_ref[...])
    ti = jnp.tanh(jnp.concatenate(i_pre, axis=-1) + bi_ref[...])
    half_c = (-0.5 * LRU_C) * _softplus(-lam_ref[...])
    a = jnp.exp(half_c * tr + half_c)
    a_s[...] = a
    v = 1.0 - a * a
    half_conv = 0.5 * conv
    u_s[...] = (v * lax.rsqrt(jnp.maximum(v, 1e-30))) * (half_conv * ti + half_conv)

    yield
    h = jnp.zeros((SUBLANES, width), F32)
    ap = jnp.ones((SUBLANES, width), F32)
    for j in range(seg):
        rows = slice(j * SUBLANES, (j + 1) * SUBLANES)
        aj = a_s[rows, :]
        h = aj * h + u_s[rows, :]
        ap = aj * ap
        h_s[rows, :] = h
        ap_s[rows, :] = ap

    yield
    sub = lax.broadcasted_iota(jnp.int32, (SUBLANES, width), 0)
    for sh in (1, 2, 4):
        keep = sub >= sh
        h = jnp.where(keep, ap * pltpu.roll(h, sh, 0) + h, h)
        ap = jnp.where(keep, ap * pltpu.roll(ap, sh, 0), ap)
    leave = h + ap * hc[0:1, :]
    enter = jnp.where(sub == 0, hc[0:1, :], pltpu.roll(leave, 1, 0))
    hc[0:1, :] = leave[SUBLANES - 1:SUBLANES, :]

    enter_t = jnp.concatenate([enter] * seg, axis=0)
    y = ((h_s[...] + ap_s[...] * enter_t) * jax.nn.gelu(ga)).astype(BF16)
    o_ref[...] = lax.dot_general(perm, y, TN_DIMS, preferred_element_type=F32).astype(o_ref.dtype)
    yield


def _rglru(proj, conv_w, conv_b, wg, b_r, b_i, lam, bsz, seq, ts=LRU_TILE, nb=LRU_SEQS_PER_STEP):
    width = conv_w.shape[1]
    nt = seq // ts
    vec = lambda: _resident((1, width))
    perm = jnp.asarray(_segment_permutation(ts), BF16)
    assert bsz % nb == 0
    proj3 = proj.reshape(bsz, seq, proj.shape[1])
    return pl.pallas_call(
        _rglru_kernel,
        grid=(bsz // nb, nt),
        in_specs=[pl.BlockSpec((nb, ts, width), lambda b, t: (b, t, 0)),
                  pl.BlockSpec((nb, ts, width), lambda b, t: (b, t, 1)),
                  _resident(perm.shape), _resident(conv_w.shape), vec(), _resident(wg.shape),
                  vec(), vec(), vec()],
        out_specs=pl.BlockSpec((nb, ts, width), lambda b, t: (b, t, 0)),
        out_shape=jax.ShapeDtypeStruct((bsz, seq, width), BF16),
        scratch_shapes=[pltpu.VMEM((nb,) + shape, F32) for shape in [
            (ts + (CONV_W - 1) * SUBLANES, width), (ts, width), (ts, width), (ts, width),
            (ts, width), (SUBLANES, width), (SUBLANES, width)]],
        compiler_params=_params(2), name="rglru",
    )(proj3, proj3, perm, conv_w, conv_b.reshape(1, width), wg, b_r.reshape(1, width),
      b_i.reshape(1, width), lam.reshape(1, width)).reshape(bsz * seq, width)


def _pair_level(n):
    t = lax.broadcasted_iota(jnp.int32, (n, n), 0)
    s = lax.broadcasted_iota(jnp.int32, (n, n), 1)
    x = t ^ s
    lvl = jnp.zeros((n, n), jnp.int32)
    w = 1
    while w < n:
        lvl = lvl + (x >= w).astype(jnp.int32)
        w *= 2
    return jnp.where(s > t, -1, lvl)


def _hgrn2_kernel(q_ref, f_ref, v_ref, g_ref, lbl_ref, ng_ref, o_ref, st, bc_s, f_s, *, layer):
    ts, dk = q_ref.shape[0], LANES
    c = CHUNK
    nck = ts // c
    heads = q_ref.shape[1] // dk

    @pl.when(pl.program_id(2) == 0)
    def _():
        st[...] = jnp.zeros_like(st)

    lg = lbl_ref[...]
    e = jnp.exp(lg - jnp.max(lg, axis=0, keepdims=True))
    lb = jnp.sum(e[0:layer + 1, :], axis=0, keepdims=True) / jnp.sum(e, axis=0, keepdims=True)

    row = lax.broadcasted_iota(jnp.int32, (c, dk), 0)
    tt = lax.broadcasted_iota(jnp.int32, (c, c), 0)
    ss = lax.broadcasted_iota(jnp.int32, (c, c), 1)
    tri = (ss <= tt).astype(BF16)
    lvl = _pair_level(c)
    n_lvl = c.bit_length()
    lvl_b = lvl.astype(F32).astype(BF16)
    right_b = [None] + [((row >> (p - 1)) & 1).astype(F32).astype(BF16) for p in range(1, n_lvl)]
    side = [None] + [jnp.where((row & (1 << (p - 1))) != 0, 1.0, -1.0) for p in range(1, n_lvl)]
    r4 = row & 3

    bodies = [(ci, hh) for ci in range(nck) for hh in range(heads)]
    for ci, hh in bodies:
        cols = slice(hh * dk, (hh + 1) * dk)
        lbh = lb[:, cols]
        f = lbh + (1.0 - lbh) * _sigmoid(f_ref[ci * c:(ci + 1) * c, cols])
        f_s[hh * nck + ci] = f
        bc_s[hh * nck + ci] = LOG2_E * sum(_dot(tri, part) for part in _split3(jnp.log(f)))

    def body(ci, hh):
        r0 = ci * c
        cols = slice(hh * dk, (hh + 1) * dk)
        qf = _silu(q_ref[r0:r0 + c, cols].astype(F32))
        f = f_s[hh * nck + ci]
        kk = 1.0 - f
        qb, kb = qf.astype(BF16), kk.astype(BF16)
        bc = bc_s[hh * nck + ci]
        vb = v_ref[r0:r0 + c, cols]

        scores = jnp.where(lvl_b == 0,
                           lax.dot_general(qb, kb, NT_DIMS,
                                           preferred_element_type=F32).astype(BF16), 0)
        for p in range(1, n_lvl):
            w = 1 << (p - 1)
            if w == 1:
                dec = jnp.where(right_b[p] > 0, f.astype(BF16), 1)
            elif w == 2:
                dec = jnp.where(r4 == 0, pltpu.roll(f, c - 1, 0),
                                jnp.where(r4 == 1, 1.0,
                                          jnp.where(r4 == 2, f, f * pltpu.roll(f, 1, 0)))
                                ).astype(BF16)
            else:
                gat = jnp.concatenate(
                    [jnp.broadcast_to(bc_s[hh * nck + ci, b * 2 * w + w - 1:b * 2 * w + w, :],
                                      (2 * w, dk)) for b in range(c // (2 * w))], axis=0)
                dec = jnp.exp2((bc - gat) * side[p]).astype(BF16)
            z = jnp.where(right_b[p] > 0, qb, kb) * dec
            scores = jnp.where(lvl_b == p,
                               lax.dot_general(z, z, NT_DIMS,
                                               preferred_element_type=F32).astype(BF16), scores)
            yield
        o = _dot(scores, vb)

        b_last = bc[c - 1:c, :]
        o = o + lax.dot_general((qf * jnp.exp2(bc)).astype(BF16), st[hh].astype(BF16), NT_DIMS,
                                preferred_element_type=F32)
        kd = (kk * jnp.exp2(b_last - bc)).astype(BF16)
        st[hh] = st[hh] * jnp.exp2(b_last) + lax.dot_general(vb, kd, TN_DIMS,
                                                             preferred_element_type=F32)

        o = o * lax.rsqrt(jnp.mean(o * o, axis=-1, keepdims=True) + NORM_EPS)
        gv = g_ref[r0:r0 + c, cols].astype(F32)
        o_ref[r0:r0 + c, cols] = (o * ng_ref[:, cols] * _silu(gv)).astype(o_ref.dtype)
        yield

    for ci in range(nck):
        for _ in zip(*[body(ci, hh) for hh in range(heads)]):
            pass


def _hgrn2(proj, fproj, lb_logits, norm_g, layer, bsz, seq, col0, ts=MIX_TILE,
           heads=HGRN2_HEADS_PER_STEP):
    dk = LANES
    wd = heads * dk
    width = fproj.shape[1]
    nt = seq // ts
    q0, v0, g0 = (o // heads for o in col0)
    row = lambda b, t: b * nt + t
    n_layers = lb_logits.shape[0]
    return pl.pallas_call(
        functools.partial(_hgrn2_kernel, layer=layer),
        grid=(bsz, B_HEADS // heads, nt),
        in_specs=[pl.BlockSpec((ts, wd), lambda b, h, t: (row(b, t), q0 + h)),
                  pl.BlockSpec((ts, wd), lambda b, h, t: (row(b, t), h)),
                  pl.BlockSpec((ts, wd), lambda b, h, t: (row(b, t), v0 + h)),
                  pl.BlockSpec((ts, wd), lambda b, h, t: (row(b, t), g0 + h)),
                  pl.BlockSpec((n_layers, wd), lambda b, h, t: (0, h)),
                  pl.BlockSpec((1, wd), lambda b, h, t: (0, h))],
        out_specs=pl.BlockSpec((ts, wd), lambda b, h, t: (row(b, t), h)),
        out_shape=jax.ShapeDtypeStruct((bsz * seq, width), BF16),
        scratch_shapes=[pltpu.VMEM((heads, dk, dk), F32),
                        pltpu.VMEM((heads * (ts // CHUNK), CHUNK, dk), F32),
                        pltpu.VMEM((heads * (ts // CHUNK), CHUNK, dk), F32)],
        compiler_params=_params(3), name="hgrn2",
    )(proj, fproj, proj, proj, lb_logits, norm_g.reshape(1, width))


def _ssd_expand_matrices(nck):
    e = np.zeros((nck, LANES, M_HPG * LANES + M_HPG * M_HEAD_DIM), np.float32)
    for c in range(nck):
        for r in range(M_HPG):
            e[c, SUBLANES * c + r, r * LANES:(r + 1) * LANES] = 1.0
            e[c, SUBLANES * (nck + c) + r,
              M_HPG * LANES + r * M_HEAD_DIM:M_HPG * LANES + (r + 1) * M_HEAD_DIM] = 1.0
    return e


def _ssd_kernel(*refs, groups):
    n_in = 16
    (z, x, b, c, dt, cwx, cbx, cwb, cbb, cwc, cbc, bias, alog, drow, ng, e_ref) = refs[:n_in]
    o_ref, scratch = refs[n_in], refs[n_in + 1:]
    gw, n = x.shape[1] // groups, b.shape[1] // groups
    row_forms = [_ssd_row_forms(dt.at[gi], bias.at[gi], alog.at[gi]) for gi in range(groups)]
    for gi in range(groups):
        wide = (slice(None), slice(gi * gw, (gi + 1) * gw))
        narrow = (slice(None), slice(gi * n, (gi + 1) * n))
        _ssd_group(z.at[wide], x.at[wide], b.at[narrow], c.at[narrow], row_forms[gi],
                   cwx.at[wide], cbx.at[wide], cwb.at[narrow], cbb.at[narrow], cwc.at[narrow],
                   cbc.at[narrow], drow.at[wide], ng.at[wide], e_ref, o_ref.at[wide],
                   *[sc.at[gi] for sc in scratch])


def _ssd_row_forms(dt_ref, bias_ref, alog_ref):
    nck, _, c = dt_ref.shape
    tt = lax.broadcasted_iota(jnp.int32, (c, c), 0)
    ss = lax.broadcasted_iota(jnp.int32, (c, c), 1)
    upper = (tt <= ss).astype(F32)
    bias = jnp.concatenate([bias_ref[...]] * nck, axis=0)
    neg_a = jnp.concatenate([-jnp.exp(alog_ref[...])] * nck, axis=0)
    dt = _softplus(dt_ref[...].reshape(nck * SUBLANES, c) + bias)
    a_row = LOG2_E * _dot_f32(dt * neg_a, upper)
    dtw = dt * jnp.exp2(a_row[:, c - 1:c] - a_row)
    src_row = a_row - LOG2_E * jnp.log(dt)
    rows = [a_row, dtw]
    if 2 * nck * SUBLANES < c:
        rows.append(jnp.zeros((c - 2 * nck * SUBLANES, c), F32))
    return _split3(jnp.concatenate(rows, axis=0).T) + (src_row,)


def _ssd_group(z_ref, x_ref, b_ref, c_ref, row_forms, cwx_ref, cbx_ref, cwb_ref, cbb_ref,
               cwc_ref, cbc_ref, drow_ref, ng_ref, e_ref, o_ref,
               xbuf, bbuf, cbuf, xc_s, bc_s, cc_s, lhs_s, xw_s, sb_s, dec_s, st):
    ts, gw = x_ref.shape
    n = D_STATE
    c = CHUNK
    nck = ts // c
    nb = M_HPG * LANES

    @pl.when(pl.program_id(2) == 0)
    def _():
        xbuf[0:SUBLANES, :] = jnp.zeros((SUBLANES, gw), F32)
        bbuf[0:SUBLANES, :] = jnp.zeros((SUBLANES, n), F32)
        cbuf[0:SUBLANES, :] = jnp.zeros((SUBLANES, n), F32)
        st[...] = jnp.zeros_like(st)

    _conv_silu(x_ref, xbuf, cwx_ref, cbx_ref, xc_s)
    _conv_silu(b_ref, bbuf, cwb_ref, cbb_ref, bc_s)
    _conv_silu(c_ref, cbuf, cwc_ref, cbc_ref, cc_s)

    tt = lax.broadcasted_iota(jnp.int32, (c, c), 0)
    ss = lax.broadcasted_iota(jnp.int32, (c, c), 1)
    causal = ss <= tt
    low_half = lax.broadcasted_iota(jnp.int32, (1, LANES), 1) < M_HEAD_DIM
    low_lanes = lax.broadcasted_iota(jnp.int32, (c + D_STATE, LANES), 1) < M_HEAD_DIM
    hi, mid, lo, src_row = row_forms

    for ci in range(nck):
        r0 = ci * c
        e_c = e_ref[ci]
        ex = _dot(hi, e_c)
        a_col = ex[:, :nb] + _dot(mid, e_c[:, :nb]) + _dot(lo, e_c[:, :nb])
        xw_s[r0:r0 + c, :] = (xc_s[r0:r0 + c, :] * ex[:, nb:]).astype(BF16)
        cm = cc_s[r0:r0 + c, :]
        cmf = cm.astype(F32)
        cb = lax.dot_general(cm, bc_s[r0:r0 + c, :], NT_DIMS, preferred_element_type=F32)
        for r in range(M_HPG):
            a_t = a_col[:, r * LANES:(r + 1) * LANES]
            src = src_row[ci * SUBLANES + r:ci * SUBLANES + r + 1, :]
            lmat = jnp.exp2(jnp.where(causal, a_t - src, -1e30))
            lhs_s[ci * M_HPG + r] = jnp.concatenate(
                [(cb * lmat).astype(BF16), (cmf * jnp.exp2(a_t)).astype(BF16)], axis=1)
        e_last = [jnp.exp2(a_col[c - 1:c, r * LANES:(r + 1) * LANES]) for r in range(M_HPG)]
        dec_s[ci:ci + 1, :] = jnp.concatenate([jnp.where(low_half, e_last[0], e_last[1]),
                                               jnp.where(low_half, e_last[2], e_last[3])], axis=1)

    for ci in range(nck):
        r0 = ci * c
        sb_s[ci] = st[...].astype(BF16)
        st[...] = st[...] * dec_s[ci:ci + 1, :] + lax.dot_general(
            bc_s[r0:r0 + c, :], xw_s[r0:r0 + c, :], TN_DIMS, preferred_element_type=F32)

    for ci in range(nck):
        r0 = ci * c
        xc = xc_s[r0:r0 + c, :]
        rhs = jnp.concatenate([xc.astype(BF16), sb_s[ci]], axis=0)
        tiles = []
        for j in range(gw // LANES):
            rt = rhs[:, j * LANES:(j + 1) * LANES]
            tiles.append(_dot(lhs_s[ci * M_HPG + 2 * j], jnp.where(low_lanes, rt, 0))
                         + _dot(lhs_s[ci * M_HPG + 2 * j + 1], jnp.where(low_lanes, 0, rt)))
        y = drow_ref[...] * xc + jnp.concatenate(tiles, axis=1)
        y = y * _silu(z_ref[r0:r0 + c, :].astype(F32))
        y = y * lax.rsqrt(jnp.mean(y * y, axis=-1, keepdims=True) + NORM_EPS)
        o_ref[r0:r0 + c, :] = (y * ng_ref[...]).astype(o_ref.dtype)


def _ssd(proj, dt_t, conv_w, conv_b, dt_bias, a_log, d_skip, norm_g, bsz, seq, ts=MIX_TILE,
         groups=SSD_GROUPS_PER_STEP):
    gw = M_HPG * M_HEAD_DIM
    n = D_STATE
    d_inner = M_GROUPS * gw
    nt = seq // ts
    nck = ts // CHUNK
    row = lambda b, t: b * nt + t
    gp = groups
    zb, xb = 0, d_inner // (gw * gp)
    bb, cb = 2 * d_inner // (n * gp), (2 * d_inner // n + M_GROUPS) // gp
    wbb, wcb = d_inner // (n * gp), (d_inner // n + M_GROUPS) // gp
    cw2 = conv_w.reshape(CONV_W, -1)
    cb2 = conv_b.reshape(1, -1)

    def per_head(v):
        v = jnp.pad(v.reshape(M_GROUPS, M_HPG), ((0, 0), (0, SUBLANES - M_HPG)))
        return jnp.broadcast_to(v[:, :, None], (M_GROUPS, SUBLANES, LANES)).astype(F32)

    drow = jnp.repeat(d_skip.astype(F32), M_HEAD_DIM).reshape(1, d_inner)
    e_mat = jnp.asarray(_ssd_expand_matrices(nck), BF16)
    g3 = lambda b, g, t: (g, 0, 0)
    return pl.pallas_call(
        functools.partial(_ssd_kernel, groups=gp),
        grid=(bsz, M_GROUPS // gp, nt),
        in_specs=[pl.BlockSpec((ts, gp * gw), lambda b, g, t: (row(b, t), zb + g)),
                  pl.BlockSpec((ts, gp * gw), lambda b, g, t: (row(b, t), xb + g)),
                  pl.BlockSpec((ts, gp * n), lambda b, g, t: (row(b, t), bb + g)),
                  pl.BlockSpec((ts, gp * n), lambda b, g, t: (row(b, t), cb + g)),
                  pl.BlockSpec((gp, nck, SUBLANES, LANES), lambda b, g, t: (g, row(b, t), 0, 0)),
                  pl.BlockSpec((CONV_W, gp * gw), lambda b, g, t: (0, g)),
                  pl.BlockSpec((1, gp * gw), lambda b, g, t: (0, g)),
                  pl.BlockSpec((CONV_W, gp * n), lambda b, g, t: (0, wbb + g)),
                  pl.BlockSpec((1, gp * n), lambda b, g, t: (0, wbb + g)),
                  pl.BlockSpec((CONV_W, gp * n), lambda b, g, t: (0, wcb + g)),
                  pl.BlockSpec((1, gp * n), lambda b, g, t: (0, wcb + g)),
                  pl.BlockSpec((gp, SUBLANES, LANES), g3),
                  pl.BlockSpec((gp, SUBLANES, LANES), g3),
                  pl.BlockSpec((1, gp * gw), lambda b, g, t: (0, g)),
                  pl.BlockSpec((1, gp * gw), lambda b, g, t: (0, g)),
                  _resident(e_mat.shape)],
        out_specs=pl.BlockSpec((ts, gp * gw), lambda b, g, t: (row(b, t), g)),
        out_shape=jax.ShapeDtypeStruct((bsz * seq, d_inner), BF16),
        scratch_shapes=[pltpu.VMEM((gp,) + shape, dt) for shape, dt in [
            ((ts + SUBLANES, gw), F32), ((ts + SUBLANES, n), F32), ((ts + SUBLANES, n), F32),
            ((ts, gw), F32), ((ts, n), BF16), ((ts, n), BF16),
            ((nck * M_HPG, CHUNK, CHUNK + n), BF16), ((ts, gw), BF16), ((nck, n, gw), BF16),
            ((max(nck, SUBLANES), gw), F32), ((n, gw), F32)]],
        compiler_params=_params(3), name="ssd",
    )(proj, proj, proj, proj, dt_t, cw2, cb2, cw2, cb2, cw2, cb2,
      per_head(dt_bias), per_head(a_log), drow, norm_g.reshape(1, d_inner), e_mat)


def _post_kernel(*refs, n_mix, final_norm, ff_chunk):
    mix_refs = refs[:n_mix]
    wmix_refs = refs[n_mix:2 * n_mix]
    (x_ref, gq_ref, wq_ref, k_ref, v_ref, wo_ref, gf_ref, w1_ref, w2_ref, gfin_ref,
     o_ref) = refs[2 * n_mix:]
    d = x_ref.shape[1]
    hd = d // X_HEADS

    x = x_ref[...]
    for m_ref, w_ref in zip(mix_refs, wmix_refs):
        x = x + _dot(m_ref[...], w_ref[...])

    q = _dot(_rms(x, gq_ref[...]).astype(BF16), wq_ref[...]).astype(BF16)
    heads = []
    for h in range(X_HEADS):
        sc = lax.dot_general(q[:, h * hd:(h + 1) * hd], k_ref[0, :, h * hd:(h + 1) * hd], NT_DIMS,
                             preferred_element_type=F32) * (hd ** -0.5)
        p = jnp.exp(sc - jnp.max(sc, axis=-1, keepdims=True))
        p = p / jnp.sum(p, axis=-1, keepdims=True)
        heads.append(_dot(p.astype(BF16), v_ref[0, :, h * hd:(h + 1) * hd]))
    x = x + _dot(jnp.concatenate(heads, axis=-1).astype(BF16), wo_ref[...])

    hn = _rms(x, gf_ref[...]).astype(BF16)
    y = x
    for c in range(0, w1_ref.shape[1], ff_chunk):
        hid = jnp.square(jnp.maximum(_dot(hn, w1_ref[:, c:c + ff_chunk]), 0.0))
        y = y + _dot(hid.astype(BF16), w2_ref[c:c + ff_chunk, :])
    if final_norm:
        y = _rms(y, gfin_ref[...])
    o_ref[...] = y


def _post(mixes, wmix_specs, wmix_arrays, x2d, gq, k, v, gf, gfin, attn_ffn, layer, final_norm,
          bsz, seq, tm=TOKEN_TILE, ff_chunk=1024):
    t, d = x2d.shape
    nt = seq // tm
    mem_len = k.shape[1]
    wq, wo, w1, w2 = attn_ffn
    tile = lambda w: pl.BlockSpec((tm, w), lambda b, i: (b * nt + i, 0))
    mem = lambda: pl.BlockSpec((1, mem_len, d), lambda b, i: (b, 0, 0))
    in_specs = [tile(m.shape[1]) for m in mixes] + list(wmix_specs)
    in_specs += [tile(d), _resident((1, d)), _layer_resident(wq, layer), mem(), mem(),
                 _layer_resident(wo, layer), _resident((1, d)), _layer_resident(w1, layer),
                 _layer_resident(w2, layer), _resident((1, d))]
    return pl.pallas_call(
        functools.partial(_post_kernel, n_mix=len(mixes), final_norm=final_norm,
                          ff_chunk=ff_chunk),
        grid=(bsz, nt), in_specs=in_specs, out_specs=tile(d),
        out_shape=jax.ShapeDtypeStruct((t, d), F32),
        compiler_params=_params(2), name="post",
    )(*mixes, *wmix_arrays, x2d, gq.reshape(1, d), wq, k, v, wo, gf.reshape(1, d), w1, w2,
      gfin.reshape(1, d))


def kernel(x, mem, norm_mix_g, norm_mem_q_g, norm_mem_kv_g, norm_ffn_g, final_norm_g, e_w_in, a_conv_w, a_conv_b, a_gate_r_w, a_gate_r_b, a_gate_i_w, a_gate_i_b, a_lambda, b_lb_logits, b_norm_g, e_w_out, o_w_in, m_conv_w, m_conv_b, m_dt_bias, m_a_log, m_d, m_norm_g, o_w_out, xq_w, xk_w, xv_w, xo_w, ffn_w1, ffn_w2):
    bsz, seq, d = x.shape
    mem_len = mem.shape[1]
    depth = norm_mix_g.shape[0]
    t = bsz * seq
    bf = lambda w: w.astype(BF16)
    xs = x.reshape(t, d)
    mem2d = mem.reshape(bsz * mem_len, d)

    attn_ffn = (bf(xq_w), bf(xo_w), bf(ffn_w1), bf(ffn_w2))
    xk, xv = bf(xk_w), bf(xv_w)
    all_cols = lambda n: tuple(range(0, n, 1024))
    for l in range(depth):
        k, v = _inproj(mem2d, norm_mem_kv_g[l], [(xk, l), (xv, l)],
                       [("nn", 0, all_cols(d)), ("nn", 1, all_cols(d))], [BF16, BF16])
        k, v = k.reshape(bsz, mem_len, d), v.reshape(bsz, mem_len, d)
        if l % 2 == 0:
            e = l // 2
            wa = d
            proj, fproj = _inproj(xs, norm_mix_g[l], [(bf(e_w_in), e)],
                                  [("nn", 0, (0, wa, 2 * wa, 4 * wa, 5 * wa)), ("nn", 0, (3 * wa,))],
                                  [BF16, F32])
            wg = bf(0.5 * jnp.concatenate([a_gate_r_w[e], a_gate_i_w[e]], axis=-1))
            ya = _rglru(proj, a_conv_w[e], a_conv_b[e], wg, 0.5 * a_gate_r_b[e],
                        0.5 * a_gate_i_b[e], a_lambda[e], bsz, seq)
            blk = wa // LANES
            yb = _hgrn2(proj, fproj, b_lb_logits, b_norm_g[e], l, bsz, seq,
                        (2 * blk, 3 * blk, 4 * blk))
            mixes = [ya, yb]
            w_out = bf(e_w_out)
            wmix_specs = [_layer_resident(w_out, e, (0, wa)), _layer_resident(w_out, e, (1, wa))]
            wmix_arrays = [w_out, w_out]
        else:
            o = l // 2
            n_main = o_w_in.shape[-1] - M_GROUPS * M_HPG
            w_dt = o_w_in[o][:, n_main:].T.reshape(M_GROUPS, M_HPG, d)
            w_dt = jnp.pad(w_dt, ((0, 0), (0, SUBLANES - M_HPG), (0, 0))).reshape(1, -1, d)
            proj, dt_t = _inproj(xs, norm_mix_g[l], [(bf(o_w_in), o), (bf(w_dt), 0)],
                                 [("nn", 0, all_cols(n_main)), ("nt", 1, ())], [BF16, F32])
            ym = _ssd(proj, dt_t, m_conv_w[o], m_conv_b[o], m_dt_bias[o], m_a_log[o], m_d[o],
                      m_norm_g[o], bsz, seq)
            mixes = [ym]
            w_out = bf(o_w_out)
            wmix_specs = [_layer_resident(w_out, o)]
            wmix_arrays = [w_out]
        xs = _post(mixes, wmix_specs, wmix_arrays, xs, norm_mem_q_g[l], k, v, norm_ffn_g[l],
                   final_norm_g, attn_ffn, l, l == depth - 1, bsz, seq)
    return xs.reshape(bsz, seq, d)
```

```python
import functools

import jax
import jax.numpy as jnp
import numpy as np
from jax import lax
from jax.experimental import pallas as pl
from jax.experimental.pallas import tpu as pltpu

F32 = jnp.float32
BF16 = jnp.bfloat16

NORM_EPS = 1e-6
LOG2_E = 1.4426950408889634
CONV_W = 4
LANES = 128
SUBLANES = 8
VMEM_LIMIT_BYTES = 56 * 1024 * 1024

A_HEADS = 8
LRU_C = 8.0
B_HEADS = 8
M_HEAD_DIM = 64
M_GROUPS = 8
M_HPG = 4
D_STATE = 128
X_HEADS = 4
CHUNK = 128

TOKEN_TILE = 512
LRU_TILE = 256
LRU_SEQS_PER_STEP = 4
MIX_TILE = 1024
HGRN2_HEADS_PER_STEP = 4
SSD_GROUPS_PER_STEP = 4

NT_DIMS = (((1,), (1,)), ((), ()))
TN_DIMS = (((0,), (0,)), ((), ()))


def _params(n_axes):
    return pltpu.CompilerParams(
        dimension_semantics=("arbitrary",) * n_axes, vmem_limit_bytes=VMEM_LIMIT_BYTES)


def _resident(shape):
    nd = len(shape)
    return pl.BlockSpec(shape, lambda *_: (0,) * nd, pipeline_mode=pl.Buffered(1))


def _rms(x, g):
    return x * lax.rsqrt(jnp.mean(x * x, axis=-1, keepdims=True) + NORM_EPS) * g


def _sigmoid(x):
    return 0.5 * jnp.tanh(0.5 * x) + 0.5


def _silu(x):
    h = 0.5 * x
    return h * jnp.tanh(h) + h


def _softplus(x):
    return jnp.maximum(x, 0.0) + jnp.log1p(jnp.exp(-jnp.abs(x)))


def _dot(a, b):
    return jnp.dot(a, b, preferred_element_type=F32)


def _dot_f32(a, b):
    return jnp.dot(a, b, preferred_element_type=F32, precision=lax.Precision.HIGHEST)


def _split3(x):
    hi = x.astype(BF16)
    r = x - hi.astype(F32)
    mid = r.astype(BF16)
    lo = (r - mid.astype(F32)).astype(BF16)
    return hi, mid, lo


def _layer_resident(w, layer, rows=None):
    r0, nr = (0, w.shape[1]) if rows is None else rows
    return pl.BlockSpec((None, nr, w.shape[2]), lambda *_: (layer, r0, 0),
                        pipeline_mode=pl.Buffered(1))


def _inproj_kernel(x_ref, g_ref, *refs, n_w, outs, col_chunk):
    w_refs, o_refs = refs[:n_w], refs[n_w:]
    xn = _rms(x_ref[...], g_ref[...]).astype(BF16)
    for (kind, wi, cols), o_ref in zip(outs, o_refs):
        w_ref = w_refs[wi]
        if kind == "nn":
            for j, c in enumerate(cols):
                o_ref[:, j * col_chunk:(j + 1) * col_chunk] = _dot(
                    xn, w_ref[:, c:c + col_chunk]).astype(o_ref.dtype)
        else:
            y = lax.dot_general(w_ref[...], xn, NT_DIMS, preferred_element_type=F32)
            for gi in range(y.shape[0] // SUBLANES):
                for ci in range(y.shape[1] // LANES):
                    o_ref[gi, ci] = y[gi * SUBLANES:(gi + 1) * SUBLANES,
                                      ci * LANES:(ci + 1) * LANES].astype(o_ref.dtype)


def _inproj(x2d, g, weights, outs, out_dtypes, tm=TOKEN_TILE, col_chunk=1024):
    t, d = x2d.shape
    tm = min(tm, t)
    assert t % tm == 0
    in_specs = [pl.BlockSpec((tm, d), lambda i: (i, 0)), _resident((1, d))]
    in_specs += [_layer_resident(w, l) for w, l in weights]
    out_specs, out_shapes = [], []
    for (kind, wi, cols), dt in zip(outs, out_dtypes):
        if kind == "nn":
            n = len(cols) * col_chunk
            out_specs.append(pl.BlockSpec((tm, n), lambda i: (i, 0)))
            out_shapes.append(jax.ShapeDtypeStruct((t, n), dt))
        else:
            rb = weights[wi][0].shape[1] // SUBLANES
            out_specs.append(pl.BlockSpec((rb, tm // LANES, SUBLANES, LANES),
                                          lambda i: (0, i, 0, 0)))
            out_shapes.append(jax.ShapeDtypeStruct((rb, t // LANES, SUBLANES, LANES), dt))
    return pl.pallas_call(
        functools.partial(_inproj_kernel, n_w=len(weights), outs=tuple(outs),
                          col_chunk=col_chunk),
        grid=(t // tm,), in_specs=in_specs, out_specs=out_specs, out_shape=out_shapes,
        compiler_params=_params(1), name="inproj",
    )(x2d, g.reshape(1, d), *[w for w, _ in weights])


def _conv_silu(src_ref, buf, w_ref, b_ref, dst_ref, rows=CHUNK):
    ts = src_ref.shape[0]
    buf[SUBLANES:SUBLANES + ts, :] = src_ref[...].astype(F32)
    for r0 in range(0, ts, rows):
        y = b_ref[...] + w_ref[CONV_W - 1:CONV_W, :] * buf[SUBLANES + r0:SUBLANES + r0 + rows, :]
        for k in range(CONV_W - 1):
            off = SUBLANES - (CONV_W - 1) + k + r0
            y = y + w_ref[k:k + 1, :] * buf[off:off + rows, :]
        dst_ref[r0:r0 + rows, :] = _silu(y).astype(dst_ref.dtype)
    buf[0:SUBLANES, :] = buf[ts:ts + SUBLANES, :]


def _segment_permutation(ts):
    seg = ts // SUBLANES
    p = np.zeros((ts, ts), np.float32)
    for j in range(seg):
        for s in range(SUBLANES):
            p[SUBLANES * j + s, seg * s + j] = 1.0
    return p


def _segment_conv(xa, xe_s, tail, w, b):
    ts, width = xa.shape
    halo = (CONV_W - 1) * SUBLANES
    first_sublane = lax.broadcasted_iota(jnp.int32, (SUBLANES, width), 0) == 0
    for i in range(CONV_W - 1):
        blk = xa[ts - halo + i * SUBLANES:ts - halo + (i + 1) * SUBLANES, :]
        xe_s[i * SUBLANES:(i + 1) * SUBLANES, :] = jnp.where(
            first_sublane, tail[i:i + 1, :], pltpu.roll(blk, 1, 0))
        tail[i:i + 1, :] = blk[SUBLANES - 1:SUBLANES, :]
    xe_s[halo:halo + ts, :] = xa
    conv = b[...] + w[CONV_W - 1:CONV_W, :] * xa
    for k in range(CONV_W - 1):
        conv = conv + w[k:k + 1, :] * xe_s[k * SUBLANES:k * SUBLANES + ts, :]
    return conv


def _rglru_kernel(xa_ref, ga_ref, *refs):
    n_par = 7
    params, o_ref, scratch = refs[:n_par], refs[n_par], refs[n_par + 1:]
    tiles = [_rglru_tile(xa_ref.at[bi], ga_ref.at[bi], *params, o_ref.at[bi],
                         *[sc.at[bi] for sc in scratch]) for bi in range(xa_ref.shape[0])]
    for _ in zip(*tiles):
        pass


def _rglru_tile(xa_ref, ga_ref, perm_ref, cw_ref, cb_ref, wg_ref, br_ref, bi_ref, lam_ref, o_ref,
                xe_s, a_s, u_s, h_s, ap_s, tail, hc):
    ts, width = xa_ref.shape
    hd = width // A_HEADS
    seg = ts // SUBLANES

    @pl.when(pl.program_id(1) == 0)
    def _():
        tail[...] = jnp.zeros_like(tail)
        hc[...] = jnp.zeros_like(hc)

    perm = perm_ref[...]
    xa = _dot(perm, xa_ref[...])
    ga = _dot(perm, ga_ref[...])
    conv = _segment_conv(xa, xe_s, tail, cw_ref, cb_ref)

    yield
    convb = conv.astype(BF16)
    r_pre, i_pre = [], []
    for h in range(A_HEADS):
        gate = _dot(convb[:, h * hd:(h + 1) * hd], wg_ref[h])
        r_pre.append(gate[:, :hd])
        i_pre.append(gate[:, hd:])
    tr = jnp.tanh(jnp.concatenate(r_pre, axis=-1) + br_ref[...])
    ti = jnp.tanh(jnp.concatenate(i_pre, axis=-1) + bi_ref[...])
    half_c = (-0.5 * LRU_C) * _softplus(-lam_ref[...])
    a = jnp.exp(half_c * tr + half_c)
    a_s[...] = a
    v = 1.0 - a * a
    half_conv = 0.5 * conv
    u_s[...] = (v * lax.rsqrt(jnp.maximum(v, 1e-30))) * (half_conv * ti + half_conv)

    yield
    h = jnp.zeros((SUBLANES, width), F32)
    ap = jnp.ones((SUBLANES, width), F32)
    for j in range(seg):
        rows = slice(j * SUBLANES, (j + 1) * SUBLANES)
        aj = a_s[rows, :]
        h = aj * h + u_s[rows, :]
        ap = aj * ap
        h_s[rows, :] = h
        ap_s[rows, :] = ap

    yield
    sub = lax.broadcasted_iota(jnp.int32, (SUBLANES, width), 0)
    for sh in (1, 2, 4):
        keep = sub >= sh
        h = jnp.where(keep, ap * pltpu.roll(h, sh, 0) + h, h)
        ap = jnp.where(keep, ap * pltpu.roll(ap, sh, 0), ap)
    leave = h + ap * hc[0:1, :]
    enter = jnp.where(sub == 0, hc[0:1, :], pltpu.roll(leave, 1, 0))
    hc[0:1, :] = leave[SUBLANES - 1:SUBLANES, :]

    enter_t = jnp.concatenate([enter] * seg, axis=0)
    y = ((h_s[...] + ap_s[...] * enter_t) * jax.nn.gelu(ga)).astype(BF16)
    o_ref[...] = lax.dot_general(perm, y, TN_DIMS, preferred_element_type=F32).astype(o_ref.dtype)
    yield


def _rglru(proj, conv_w, conv_b, wg, b_r, b_i, lam, bsz, seq, ts=LRU_TILE, nb=LRU_SEQS_PER_STEP):
    width = conv_w.shape[1]
    nt = seq // ts
    vec = lambda: _resident((1, width))
    perm = jnp.asarray(_segment_permutation(ts), BF16)
    assert bsz % nb == 0
    proj3 = proj.reshape(bsz, seq, proj.shape[1])
    return pl.pallas_call(
        _rglru_kernel,
        grid=(bsz // nb, nt),
        in_specs=[pl.BlockSpec((nb, ts, width), lambda b, t: (b, t, 0)),
                  pl.BlockSpec((nb, ts, width), lambda b, t: (b, t, 1)),
                  _resident(perm.shape), _resident(conv_w.shape), vec(), _resident(wg.shape),
                  vec(), vec(), vec()],
        out_specs=pl.BlockSpec((nb, ts, width), lambda b, t: (b, t, 0)),
        out_shape=jax.ShapeDtypeStruct((bsz, seq, width), BF16),
        scratch_shapes=[pltpu.VMEM((nb,) + shape, F32) for shape in [
            (ts + (CONV_W - 1) * SUBLANES, width), (ts, width), (ts, width), (ts, width),
            (ts, width), (SUBLANES, width), (SUBLANES, width)]],
        compiler_params=_params(2), name="rglru",
    )(proj3, proj3, perm, conv_w, conv_b.reshape(1, width), wg, b_r.reshape(1, width),
      b_i.reshape(1, width), lam.reshape(1, width)).reshape(bsz * seq, width)


def _pair_level(n):
    t = lax.broadcasted_iota(jnp.int32, (n, n), 0)
    s = lax.broadcasted_iota(jnp.int32, (n, n), 1)
    x = t ^ s
    lvl = jnp.zeros((n, n), jnp.int32)
    w = 1
    while w < n:
        lvl = lvl + (x >= w).astype(jnp.int32)
        w *= 2
    return jnp.where(s > t, -1, lvl)


def _hgrn2_kernel(q_ref, f_ref, v_ref, g_ref, lbl_ref, ng_ref, o_ref, st, bc_s, f_s, *, layer):
    ts, dk = q_ref.shape[0], LANES
    c = CHUNK
    nck = ts // c
    heads = q_ref.shape[1] // dk

    @pl.when(pl.program_id(2) == 0)
    def _():
        st[...] = jnp.zeros_like(st)

    lg = lbl_ref[...]
    e = jnp.exp(lg - jnp.max(lg, axis=0, keepdims=True))
    lb = jnp.sum(e[0:layer + 1, :], axis=0, keepdims=True) / jnp.sum(e, axis=0, keepdims=True)

    row = lax.broadcasted_iota(jnp.int32, (c, dk), 0)
    tt = lax.broadcasted_iota(jnp.int32, (c, c), 0)
    ss = lax.broadcasted_iota(jnp.int32, (c, c), 1)
    tri = (ss <= tt).astype(BF16)
    lvl = _pair_level(c)
    n_lvl = c.bit_length()
    lvl_b = lvl.astype(F32).astype(BF16)
    right_b = [None] + [((row >> (p - 1)) & 1).astype(F32).astype(BF16) for p in range(1, n_lvl)]
    side = [None] + [jnp.where((row & (1 << (p - 1))) != 0, 1.0, -1.0) for p in range(1, n_lvl)]
    r4 = row & 3

    bodies = [(ci, hh) for ci in range(nck) for hh in range(heads)]
    for ci, hh in bodies:
        cols = slice(hh * dk, (hh + 1) * dk)
        lbh = lb[:, cols]
        f = lbh + (1.0 - lbh) * _sigmoid(f_ref[ci * c:(ci + 1) * c, cols])
        f_s[hh * nck + ci] = f
        bc_s[hh * nck + ci] = LOG2_E * sum(_dot(tri, part) for part in _split3(jnp.log(f)))

    def body(ci, hh):
        r0 = ci * c
        cols = slice(hh * dk, (hh + 1) * dk)
        qf = _silu(q_ref[r0:r0 + c, cols].astype(F32))
        f = f_s[hh * nck + ci]
        kk = 1.0 - f
        qb, kb = qf.astype(BF16), kk.astype(BF16)
        bc = bc_s[hh * nck + ci]
        vb = v_ref[r0:r0 + c, cols]

        scores = jnp.where(lvl_b == 0,
                           lax.dot_general(qb, kb, NT_DIMS,
                                           preferred_element_type=F32).astype(BF16), 0)
        for p in range(1, n_lvl):
            w = 1 << (p - 1)
            if w == 1:
                dec = jnp.where(right_b[p] > 0, f.astype(BF16), 1)
            elif w == 2:
                dec = jnp.where(r4 == 0, pltpu.roll(f, c - 1, 0),
                                jnp.where(r4 == 1, 1.0,
                                          jnp.where(r4 == 2, f, f * pltpu.roll(f, 1, 0)))
                                ).astype(BF16)
            else:
                gat = jnp.concatenate(
                    [jnp.broadcast_to(bc_s[hh * nck + ci, b * 2 * w + w - 1:b * 2 * w + w, :],
                                      (2 * w, dk)) for b in range(c // (2 * w))], axis=0)
                dec = jnp.exp2((bc - gat) * side[p]).astype(BF16)
            z = jnp.where(right_b[p] > 0, qb, kb) * dec
            scores = jnp.where(lvl_b == p,
                               lax.dot_general(z, z, NT_DIMS,
                                               preferred_element_type=F32).astype(BF16), scores)
            yield
        o = _dot(scores, vb)

        b_last = bc[c - 1:c, :]
        o = o + lax.dot_general((qf * jnp.exp2(bc)).astype(BF16), st[hh].astype(BF16), NT_DIMS,
                                preferred_element_type=F32)
        kd = (kk * jnp.exp2(b_last - bc)).astype(BF16)
        st[hh] = st[hh] * jnp.exp2(b_last) + lax.dot_general(vb, kd, TN_DIMS,
                                                             preferred_element_type=F32)

        o = o * lax.rsqrt(jnp.mean(o * o, axis=-1, keepdims=True) + NORM_EPS)
        gv = g_ref[r0:r0 + c, cols].astype(F32)
        o_ref[r0:r0 + c, cols] = (o * ng_ref[:, cols] * _silu(gv)).astype(o_ref.dtype)
        yield

    for ci in range(nck):
        for _ in zip(*[body(ci, hh) for hh in range(heads)]):
            pass


def _hgrn2(proj, fproj, lb_logits, norm_g, layer, bsz, seq, col0, ts=MIX_TILE,
           heads=HGRN2_HEADS_PER_STEP):
    dk = LANES
    wd = heads * dk
    width = fproj.shape[1]
    nt = seq // ts
    q0, v0, g0 = (o // heads for o in col0)
    row = lambda b, t: b * nt + t
    n_layers = lb_logits.shape[0]
    return pl.pallas_call(
        functools.partial(_hgrn2_kernel, layer=layer),
        grid=(bsz, B_HEADS // heads, nt),
        in_specs=[pl.BlockSpec((ts, wd), lambda b, h, t: (row(b, t), q0 + h)),
                  pl.BlockSpec((ts, wd), lambda b, h, t: (row(b, t), h)),
                  pl.BlockSpec((ts, wd), lambda b, h, t: (row(b, t), v0 + h)),
                  pl.BlockSpec((ts, wd), lambda b, h, t: (row(b, t), g0 + h)),
                  pl.BlockSpec((n_layers, wd), lambda b, h, t: (0, h)),
                  pl.BlockSpec((1, wd), lambda b, h, t: (0, h))],
        out_specs=pl.BlockSpec((ts, wd), lambda b, h, t: (row(b, t), h)),
        out_shape=jax.ShapeDtypeStruct((bsz * seq, width), BF16),
        scratch_shapes=[pltpu.VMEM((heads, dk, dk), F32),
                        pltpu.VMEM((heads * (ts // CHUNK), CHUNK, dk), F32),
                        pltpu.VMEM((heads * (ts // CHUNK), CHUNK, dk), F32)],
        compiler_params=_params(3), name="hgrn2",
    )(proj, fproj, proj, proj, lb_logits, norm_g.reshape(1, width))


def _ssd_expand_matrices(nck):
    e = np.zeros((nck, LANES, M_HPG * LANES + M_HPG * M_HEAD_DIM), np.float32)
    for c in range(nck):
        for r in range(M_HPG):
            e[c, SUBLANES * c + r, r * LANES:(r + 1) * LANES] = 1.0
            e[c, SUBLANES * (nck + c) + r,
              M_HPG * LANES + r * M_HEAD_DIM:M_HPG * LANES + (r + 1) * M_HEAD_DIM] = 1.0
    return e


def _ssd_kernel(*refs, groups):
    n_in = 16
    (z, x, b, c, dt, cwx, cbx, cwb, cbb, cwc, cbc, bias, alog, drow, ng, e_ref) = refs[:n_in]
    o_ref, scratch = refs[n_in], refs[n_in + 1:]
    gw, n = x.shape[1] // groups, b.shape[1] // groups
    row_forms = [_ssd_row_forms(dt.at[gi], bias.at[gi], alog.at[gi]) for gi in range(groups)]
    for gi in range(groups):
        wide = (slice(None), slice(gi * gw, (gi + 1) * gw))
        narrow = (slice(None), slice(gi * n, (gi + 1) * n))
        _ssd_group(z.at[wide], x.at[wide], b.at[narrow], c.at[narrow], row_forms[gi],
                   cwx.at[wide], cbx.at[wide], cwb.at[narrow], cbb.at[narrow], cwc.at[narrow],
                   cbc.at[narrow], drow.at[wide], ng.at[wide], e_ref, o_ref.at[wide],
                   *[sc.at[gi] for sc in scratch])


def _ssd_row_forms(dt_ref, bias_ref, alog_ref):
    nck, _, c = dt_ref.shape
    tt = lax.broadcasted_iota(jnp.int32, (c, c), 0)
    ss = lax.broadcasted_iota(jnp.int32, (c, c), 1)
    upper = (tt <= ss).astype(F32)
    bias = jnp.concatenate([bias_ref[...]] * nck, axis=0)
    neg_a = jnp.concatenate([-jnp.exp(alog_ref[...])] * nck, axis=0)
    dt = _softplus(dt_ref[...].reshape(nck * SUBLANES, c) + bias)
    a_row = LOG2_E * _dot_f32(dt * neg_a, upper)
    dtw = dt * jnp.exp2(a_row[:, c - 1:c] - a_row)
    src_row = a_row - LOG2_E * jnp.log(dt)
    rows = [a_row, dtw]
    if 2 * nck * SUBLANES < c:
        rows.append(jnp.zeros((c - 2 * nck * SUBLANES, c), F32))
    return _split3(jnp.concatenate(rows, axis=0).T) + (src_row,)


def _ssd_group(z_ref, x_ref, b_ref, c_ref, row_forms, cwx_ref, cbx_ref, cwb_ref, cbb_ref,
               cwc_ref, cbc_ref, drow_ref, ng_ref, e_ref, o_ref,
               xbuf, bbuf, cbuf, xc_s, bc_s, cc_s, lhs_s, xw_s, sb_s, dec_s, st):
    ts, gw = x_ref.shape
    n = D_STATE
    c = CHUNK
    nck = ts // c
    nb = M_HPG * LANES

    @pl.when(pl.program_id(2) == 0)
    def _():
        xbuf[0:SUBLANES, :] = jnp.zeros((SUBLANES, gw), F32)
        bbuf[0:SUBLANES, :] = jnp.zeros((SUBLANES, n), F32)
        cbuf[0:SUBLANES, :] = jnp.zeros((SUBLANES, n), F32)
        st[...] = jnp.zeros_like(st)

    _conv_silu(x_ref, xbuf, cwx_ref, cbx_ref, xc_s)
    _conv_silu(b_ref, bbuf, cwb_ref, cbb_ref, bc_s)
    _conv_silu(c_ref, cbuf, cwc_ref, cbc_ref, cc_s)

    tt = lax.broadcasted_iota(jnp.int32, (c, c), 0)
    ss = lax.broadcasted_iota(jnp.int32, (c, c), 1)
    causal = ss <= tt
    low_half = lax.broadcasted_iota(jnp.int32, (1, LANES), 1) < M_HEAD_DIM
    low_lanes = lax.broadcasted_iota(jnp.int32, (c + D_STATE, LANES), 1) < M_HEAD_DIM
    hi, mid, lo, src_row = row_forms

    for ci in range(nck):
        r0 = ci * c
        e_c = e_ref[ci]
        ex = _dot(hi, e_c)
        a_col = ex[:, :nb] + _dot(mid, e_c[:, :nb]) + _dot(lo, e_c[:, :nb])
        xw_s[r0:r0 + c, :] = (xc_s[r0:r0 + c, :] * ex[:, nb:]).astype(BF16)
        cm = cc_s[r0:r0 + c, :]
        cmf = cm.astype(F32)
        cb = lax.dot_general(cm, bc_s[r0:r0 + c, :], NT_DIMS, preferred_element_type=F32)
        for r in range(M_HPG):
            a_t = a_col[:, r * LANES:(r + 1) * LANES]
            src = src_row[ci * SUBLANES + r:ci * SUBLANES + r + 1, :]
            lmat = jnp.exp2(jnp.where(causal, a_t - src, -1e30))
            lhs_s[ci * M_HPG + r] = jnp.concatenate(
                [(cb * lmat).astype(BF16), (cmf * jnp.exp2(a_t)).astype(BF16)], axis=1)
        e_last = [jnp.exp2(a_col[c - 1:c, r * LANES:(r + 1) * LANES]) for r in range(M_HPG)]
        dec_s[ci:ci + 1, :] = jnp.concatenate([jnp.where(low_half, e_last[0], e_last[1]),
                                               jnp.where(low_half, e_last[2], e_last[3])], axis=1)

    for ci in range(nck):
        r0 = ci * c
        sb_s[ci] = st[...].astype(BF16)
        st[...] = st[...] * dec_s[ci:ci + 1, :] + lax.dot_general(
            bc_s[r0:r0 + c, :], xw_s[r0:r0 + c, :], TN_DIMS, preferred_element_type=F32)

    for ci in range(nck):
        r0 = ci * c
        xc = xc_s[r0:r0 + c, :]
        rhs = jnp.concatenate([xc.astype(BF16), sb_s[ci]], axis=0)
        tiles = []
        for j in range(gw // LANES):
            rt = rhs[:, j * LANES:(j + 1) * LANES]
            tiles.append(_dot(lhs_s[ci * M_HPG + 2 * j], jnp.where(low_lanes, rt, 0))
                         + _dot(lhs_s[ci * M_HPG + 2 * j + 1], jnp.where(low_lanes, 0, rt)))
        y = drow_ref[...] * xc + jnp.concatenate(tiles, axis=1)
        y = y * _silu(z_ref[r0:r0 + c, :].astype(F32))
        y = y * lax.rsqrt(jnp.mean(y * y, axis=-1, keepdims=True) + NORM_EPS)
        o_ref[r0:r0 + c, :] = (y * ng_ref[...]).astype(o_ref.dtype)


def _ssd(proj, dt_t, conv_w, conv_b, dt_bias, a_log, d_skip, norm_g, bsz, seq, ts=MIX_TILE,
         groups=SSD_GROUPS_PER_STEP):
    gw = M_HPG * M_HEAD_DIM
    n = D_STATE
    d_inner = M_GROUPS * gw
    nt = seq // ts
    nck = ts // CHUNK
    row = lambda b, t: b * nt + t
    gp = groups
    zb, xb = 0, d_inner // (gw * gp)
    bb, cb = 2 * d_inner // (n * gp), (2 * d_inner // n + M_GROUPS) // gp
    wbb, wcb = d_inner // (n * gp), (d_inner // n + M_GROUPS) // gp
    cw2 = conv_w.reshape(CONV_W, -1)
    cb2 = conv_b.reshape(1, -1)

    def per_head(v):
        v = jnp.pad(v.reshape(M_GROUPS, M_HPG), ((0, 0), (0, SUBLANES - M_HPG)))
        return jnp.broadcast_to(v[:, :, None], (M_GROUPS, SUBLANES, LANES)).astype(F32)

    drow = jnp.repeat(d_skip.astype(F32), M_HEAD_DIM).reshape(1, d_inner)
    e_mat = jnp.asarray(_ssd_expand_matrices(nck), BF16)
    g3 = lambda b, g, t: (g, 0, 0)
    return pl.pallas_call(
        functools.partial(_ssd_kernel, groups=gp),
        grid=(bsz, M_GROUPS // gp, nt),
        in_specs=[pl.BlockSpec((ts, gp * gw), lambda b, g, t: (row(b, t), zb + g)),
                  pl.BlockSpec((ts, gp * gw), lambda b, g, t: (row(b, t), xb + g)),
                  pl.BlockSpec((ts, gp * n), lambda b, g, t: (row(b, t), bb + g)),
                  pl.BlockSpec((ts, gp * n), lambda b, g, t: (row(b, t), cb + g)),
                  pl.BlockSpec((gp, nck, SUBLANES, LANES), lambda b, g, t: (g, row(b, t), 0, 0)),
                  pl.BlockSpec((CONV_W, gp * gw), lambda b, g, t: (0, g)),
                  pl.BlockSpec((1, gp * gw), lambda b, g, t: (0, g)),
                  pl.BlockSpec((CONV_W, gp * n), lambda b, g, t: (0, wbb + g)),
                  pl.BlockSpec((1, gp * n), lambda b, g, t: (0, wbb + g)),
                  pl.BlockSpec((CONV_W, gp * n), lambda b, g, t: (0, wcb + g)),
                  pl.BlockSpec((1, gp * n), lambda b, g, t: (0, wcb + g)),
                  pl.BlockSpec((gp, SUBLANES, LANES), g3),
                  pl.BlockSpec((gp, SUBLANES, LANES), g3),
                  pl.BlockSpec((1, gp * gw), lambda b, g, t: (0, g)),
                  pl.BlockSpec((1, gp * gw), lambda b, g, t: (0, g)),
                  _resident(e_mat.shape)],
        out_specs=pl.BlockSpec((ts, gp * gw), lambda b, g, t: (row(b, t), g)),
        out_shape=jax.ShapeDtypeStruct((bsz * seq, d_inner), BF16),
        scratch_shapes=[pltpu.VMEM((gp,) + shape, dt) for shape, dt in [
            ((ts + SUBLANES, gw), F32), ((ts + SUBLANES, n), F32), ((ts + SUBLANES, n), F32),
            ((ts, gw), F32), ((ts, n), BF16), ((ts, n), BF16),
            ((nck * M_HPG, CHUNK, CHUNK + n), BF16), ((ts, gw), BF16), ((nck, n, gw), BF16),
            ((max(nck, SUBLANES), gw), F32), ((n, gw), F32)]],
        compiler_params=_params(3), name="ssd",
    )(proj, proj, proj, proj, dt_t, cw2, cb2, cw2, cb2, cw2, cb2,
      per_head(dt_bias), per_head(a_log), drow, norm_g.reshape(1, d_inner), e_mat)


def _post_kernel(*refs, n_mix, final_norm, ff_chunk):
    mix_refs = refs[:n_mix]
    wmix_refs = refs[n_mix:2 * n_mix]
    (x_ref, gq_ref, wq_ref, k_ref, v_ref, wo_ref, gf_ref, w1_ref, w2_ref, gfin_ref,
     o_ref) = refs[2 * n_mix:]
    d = x_ref.shape[1]
    hd = d // X_HEADS

    x = x_ref[...]
    for m_ref, w_ref in zip(mix_refs, wmix_refs):
        x = x + _dot(m_ref[...], w_ref[...])

    q = _dot(_rms(x, gq_ref[...]).astype(BF16), wq_ref[...]).astype(BF16)
    hs = range(X_HEADS)
    head = lambda ref, h: ref[0, :, h * hd:(h + 1) * hd]
    sc = [lax.dot_general(q[:, h * hd:(h + 1) * hd], head(k_ref, h), NT_DIMS,
                          preferred_element_type=F32) * (hd ** -0.5) for h in hs]
    p = [jnp.exp(sc[h] - jnp.max(sc[h], axis=-1, keepdims=True)) for h in hs]
    p = [p[h] / jnp.sum(p[h], axis=-1, keepdims=True) for h in hs]
    heads = [_dot(p[h].astype(BF16), head(v_ref, h)) for h in hs]
    x = x + _dot(jnp.concatenate(heads, axis=-1).astype(BF16), wo_ref[...])

    hn = _rms(x, gf_ref[...]).astype(BF16)
    chunks = list(range(0, w1_ref.shape[1], ff_chunk))
    up = lambda c: _dot(hn, w1_ref[:, c:c + ff_chunk])
    y = x
    pre = up(chunks[0])
    for i, c in enumerate(chunks):
        nxt = up(chunks[i + 1]) if i + 1 < len(chunks) else None
        hid = jnp.square(jnp.maximum(pre, 0.0))
        y = y + _dot(hid.astype(BF16), w2_ref[c:c + ff_chunk, :])
        pre = nxt
    if final_norm:
        y = _rms(y, gfin_ref[...])
    o_ref[...] = y


def _post(mixes, wmix_specs, wmix_arrays, x2d, gq, k, v, gf, gfin, attn_ffn, layer, final_norm,
          bsz, seq, tm=TOKEN_TILE, ff_chunk=1024):
    t, d = x2d.shape
    nt = seq // tm
    mem_len = k.shape[1]
    wq, wo, w1, w2 = attn_ffn
    tile = lambda w: pl.BlockSpec((tm, w), lambda b, i: (b * nt + i, 0))
    mem = lambda: pl.BlockSpec((1, mem_len, d), lambda b, i: (b, 0, 0))
    in_specs = [tile(m.shape[1]) for m in mixes] + list(wmix_specs)
    in_specs += [tile(d), _resident((1, d)), _layer_resident(wq, layer), mem(), mem(),
                 _layer_resident(wo, layer), _resident((1, d)), _layer_resident(w1, layer),
                 _layer_resident(w2, layer), _resident((1, d))]
    return pl.pallas_call(
        functools.partial(_post_kernel, n_mix=len(mixes), final_norm=final_norm,
                          ff_chunk=ff_chunk),
        grid=(bsz, nt), in_specs=in_specs, out_specs=tile(d),
        out_shape=jax.ShapeDtypeStruct((t, d), F32),
        compiler_params=_params(2), name="post",
    )(*mixes, *wmix_arrays, x2d, gq.reshape(1, d), wq, k, v, wo, gf.reshape(1, d), w1, w2,
      gfin.reshape(1, d))


def kernel(x, mem, norm_mix_g, norm_mem_q_g, norm_mem_kv_g, norm_ffn_g, final_norm_g, e_w_in, a_conv_w, a_conv_b, a_gate_r_w, a_gate_r_b, a_gate_i_w, a_gate_i_b, a_lambda, b_lb_logits, b_norm_g, e_w_out, o_w_in, m_conv_w, m_conv_b, m_dt_bias, m_a_log, m_d, m_norm_g, o_w_out, xq_w, xk_w, xv_w, xo_w, ffn_w1, ffn_w2):
    bsz, seq, d = x.shape
    mem_len = mem.shape[1]
    depth = norm_mix_g.shape[0]
    t = bsz * seq
    bf = lambda w: w.astype(BF16)
    xs = x.reshape(t, d)
    mem2d = mem.reshape(bsz * mem_len, d)

    attn_ffn = (bf(xq_w), bf(xo_w), bf(ffn_w1), bf(ffn_w2))
    xk, xv = bf(xk_w), bf(xv_w)
    all_cols = lambda n: tuple(range(0, n, 1024))
    for l in range(depth):
        k, v = _inproj(mem2d, norm_mem_kv_g[l], [(xk, l), (xv, l)],
                       [("nn", 0, all_cols(d)), ("nn", 1, all_cols(d))], [BF16, BF16])
        k, v = k.reshape(bsz, mem_len, d), v.reshape(bsz, mem_len, d)
        if l % 2 == 0:
            e = l // 2
            wa = d
            proj, fproj = _inproj(xs, norm_mix_g[l], [(bf(e_w_in), e)],
                                  [("nn", 0, (0, wa, 2 * wa, 4 * wa, 5 * wa)), ("nn", 0, (3 * wa,))],
                                  [BF16, F32])
            wg = bf(0.5 * jnp.concatenate([a_gate_r_w[e], a_gate_i_w[e]], axis=-1))
            ya = _rglru(proj, a_conv_w[e], a_conv_b[e], wg, 0.5 * a_gate_r_b[e],
                        0.5 * a_gate_i_b[e], a_lambda[e], bsz, seq)
            blk = wa // LANES
            yb = _hgrn2(proj, fproj, b_lb_logits, b_norm_g[e], l, bsz, seq,
                        (2 * blk, 3 * blk, 4 * blk))
            mixes = [ya, yb]
            w_out = bf(e_w_out)
            wmix_specs = [_layer_resident(w_out, e, (0, wa)), _layer_resident(w_out, e, (1, wa))]
            wmix_arrays = [w_out, w_out]
        else:
            o = l // 2
            n_main = o_w_in.shape[-1] - M_GROUPS * M_HPG
            w_dt = o_w_in[o][:, n_main:].T.reshape(M_GROUPS, M_HPG, d)
            w_dt = jnp.pad(w_dt, ((0, 0), (0, SUBLANES - M_HPG), (0, 0))).reshape(1, -1, d)
            proj, dt_t = _inproj(xs, norm_mix_g[l], [(bf(o_w_in), o), (bf(w_dt), 0)],
                                 [("nn", 0, all_cols(n_main)), ("nt", 1, ())], [BF16, F32])
            ym = _ssd(proj, dt_t, m_conv_w[o], m_conv_b[o], m_dt_bias[o], m_a_log[o], m_d[o],
                      m_norm_g[o], bsz, seq)
            mixes = [ym]
            w_out = bf(o_w_out)
            wmix_specs = [_layer_resident(w_out, o)]
            wmix_arrays = [w_out]
        xs = _post(mixes, wmix_specs, wmix_arrays, xs, norm_mem_q_g[l], k, v, norm_ffn_g[l],
                   final_norm_g, attn_ffn, l, l == depth - 1, bsz, seq)
    return xs.reshape(bsz, seq, d)
```

```python
import functools

import jax
import jax.numpy as jnp
import numpy as np
from jax import lax
from jax.experimental import pallas as pl
from jax.experimental.pallas import tpu as pltpu

F32 = jnp.float32
BF16 = jnp.bfloat16

NORM_EPS = 1e-6
LOG2_E = 1.4426950408889634
CONV_W = 4
LANES = 128
SUBLANES = 8
VMEM_LIMIT_BYTES = 56 * 1024 * 1024

A_HEADS = 8
LRU_C = 8.0
B_HEADS = 8
M_HEAD_DIM = 64
M_GROUPS = 8
M_HPG = 4
D_STATE = 128
X_HEADS = 4
CHUNK = 128

TOKEN_TILE = 512
LRU_TILE = 256
LRU_SEQS_PER_STEP = 4
MIX_TILE = 1024
HGRN2_HEADS_PER_STEP = 4
SSD_GROUPS_PER_STEP = 4
POST_SUB_TILES = 2

NT_DIMS = (((1,), (1,)), ((), ()))
TN_DIMS = (((0,), (0,)), ((), ()))


def _params(n_axes):
    return pltpu.CompilerParams(
        dimension_semantics=("arbitrary",) * n_axes, vmem_limit_bytes=VMEM_LIMIT_BYTES)


def _resident(shape):
    nd = len(shape)
    return pl.BlockSpec(shape, lambda *_: (0,) * nd, pipeline_mode=pl.Buffered(1))


def _rms(x, g):
    return x * lax.rsqrt(jnp.mean(x * x, axis=-1, keepdims=True) + NORM_EPS) * g


def _sigmoid(x):
    return 0.5 * jnp.tanh(0.5 * x) + 0.5


def _silu(x):
    h = 0.5 * x
    return h * jnp.tanh(h) + h


def _softplus(x):
    return jnp.maximum(x, 0.0) + jnp.log1p(jnp.exp(-jnp.abs(x)))


def _dot(a, b):
    return jnp.dot(a, b, preferred_element_type=F32)


def _dot_f32(a, b):
    return jnp.dot(a, b, preferred_element_type=F32, precision=lax.Precision.HIGHEST)


def _split3(x):
    hi = x.astype(BF16)
    r = x - hi.astype(F32)
    mid = r.astype(BF16)
    lo = (r - mid.astype(F32)).astype(BF16)
    return hi, mid, lo


def _layer_resident(w, layer, rows=None):
    r0, nr = (0, w.shape[1]) if rows is None else rows
    return pl.BlockSpec((None, nr, w.shape[2]), lambda *_: (layer, r0, 0),
                        pipeline_mode=pl.Buffered(1))


def _inproj_kernel(x_ref, g_ref, *refs, n_w, outs, col_chunk):
    w_refs, o_refs = refs[:n_w], refs[n_w:]
    xn = _rms(x_ref[...], g_ref[...]).astype(BF16)
    for (kind, wi, cols), o_ref in zip(outs, o_refs):
        w_ref = w_refs[wi]
        if kind == "nn":
            for j, c in enumerate(cols):
                o_ref[:, j * col_chunk:(j + 1) * col_chunk] = _dot(
                    xn, w_ref[:, c:c + col_chunk]).astype(o_ref.dtype)
        else:
            y = lax.dot_general(w_ref[...], xn, NT_DIMS, preferred_element_type=F32)
            for gi in range(y.shape[0] // SUBLANES):
                for ci in range(y.shape[1] // LANES):
                    o_ref[gi, ci] = y[gi * SUBLANES:(gi + 1) * SUBLANES,
                                      ci * LANES:(ci + 1) * LANES].astype(o_ref.dtype)


def _inproj(x2d, g, weights, outs, out_dtypes, tm=TOKEN_TILE, col_chunk=1024):
    t, d = x2d.shape
    tm = min(tm, t)
    assert t % tm == 0
    in_specs = [pl.BlockSpec((tm, d), lambda i: (i, 0)), _resident((1, d))]
    in_specs += [_layer_resident(w, l) for w, l in weights]
    out_specs, out_shapes = [], []
    for (kind, wi, cols), dt in zip(outs, out_dtypes):
        if kind == "nn":
            n = len(cols) * col_chunk
            out_specs.append(pl.BlockSpec((tm, n), lambda i: (i, 0)))
            out_shapes.append(jax.ShapeDtypeStruct((t, n), dt))
        else:
            rb = weights[wi][0].shape[1] // SUBLANES
            out_specs.append(pl.BlockSpec((rb, tm // LANES, SUBLANES, LANES),
                                          lambda i: (0, i, 0, 0)))
            out_shapes.append(jax.ShapeDtypeStruct((rb, t // LANES, SUBLANES, LANES), dt))
    return pl.pallas_call(
        functools.partial(_inproj_kernel, n_w=len(weights), outs=tuple(outs),
                          col_chunk=col_chunk),
        grid=(t // tm,), in_specs=in_specs, out_specs=out_specs, out_shape=out_shapes,
        compiler_params=_params(1), name="inproj",
    )(x2d, g.reshape(1, d), *[w for w, _ in weights])


def _conv_silu(src_ref, buf, w_ref, b_ref, dst_ref, rows=CHUNK):
    ts = src_ref.shape[0]
    buf[SUBLANES:SUBLANES + ts, :] = src_ref[...].astype(F32)
    for r0 in range(0, ts, rows):
        y = b_ref[...] + w_ref[CONV_W - 1:CONV_W, :] * buf[SUBLANES + r0:SUBLANES + r0 + rows, :]
        for k in range(CONV_W - 1):
            off = SUBLANES - (CONV_W - 1) + k + r0
            y = y + w_ref[k:k + 1, :] * buf[off:off + rows, :]
        dst_ref[r0:r0 + rows, :] = _silu(y).astype(dst_ref.dtype)
    buf[0:SUBLANES, :] = buf[ts:ts + SUBLANES, :]


def _segment_permutation(ts):
    seg = ts // SUBLANES
    p = np.zeros((ts, ts), np.float32)
    for j in range(seg):
        for s in range(SUBLANES):
            p[SUBLANES * j + s, seg * s + j] = 1.0
    return p


def _segment_conv(xa, xe_s, tail, w, b):
    ts, width = xa.shape
    halo = (CONV_W - 1) * SUBLANES
    first_sublane = lax.broadcasted_iota(jnp.int32, (SUBLANES, width), 0) == 0
    for i in range(CONV_W - 1):
        blk = xa[ts - halo + i * SUBLANES:ts - halo + (i + 1) * SUBLANES, :]
        xe_s[i * SUBLANES:(i + 1) * SUBLANES, :] = jnp.where(
            first_sublane, tail[i:i + 1, :], pltpu.roll(blk, 1, 0))
        tail[i:i + 1, :] = blk[SUBLANES - 1:SUBLANES, :]
    xe_s[halo:halo + ts, :] = xa
    conv = b[...] + w[CONV_W - 1:CONV_W, :] * xa
    for k in range(CONV_W - 1):
        conv = conv + w[k:k + 1, :] * xe_s[k * SUBLANES:k * SUBLANES + ts, :]
    return conv


def _rglru_kernel(xa_ref, ga_ref, *refs):
    n_par = 7
    params, o_ref, scratch = refs[:n_par], refs[n_par], refs[n_par + 1:]
    tiles = [_rglru_tile(xa_ref.at[bi], ga_ref.at[bi], *params, o_ref.at[bi],
                         *[sc.at[bi] for sc in scratch]) for bi in range(xa_ref.shape[0])]
    for _ in zip(*tiles):
        pass


def _rglru_tile(xa_ref, ga_ref, perm_ref, cw_ref, cb_ref, wg_ref, br_ref, bi_ref, lam_ref, o_ref,
                xe_s, a_s, u_s, h_s, ap_s, tail, hc):
    ts, width = xa_ref.shape
    hd = width // A_HEADS
    seg = ts // SUBLANES

    @pl.when(pl.program_id(1) == 0)
    def _():
        tail[...] = jnp.zeros_like(tail)
        hc[...] = jnp.zeros_like(hc)

    perm = perm_ref[...]
    xa = _dot(perm, xa_ref[...])
    ga = _dot(perm, ga_ref[...])
    conv = _segment_conv(xa, xe_s, tail, cw_ref, cb_ref)

    yield
    convb = conv.astype(BF16)
    r_pre, i_pre = [], []
    for h in range(A_HEADS):
        gate = _dot(convb[:, h * hd:(h + 1) * hd], wg_ref[h])
        r_pre.append(gate[:, :hd])
        i_pre.append(gate[:, hd:])
    tr = jnp.tanh(jnp.concatenate(r_pre, axis=-1) + br_ref[...])
    ti = jnp.tanh(jnp.concatenate(i_pre, axis=-1) + bi_ref[...])
    half_c = (-0.5 * LRU_C) * _softplus(-lam_ref[...])
    a = jnp.exp(half_c * tr + half_c)
    a_s[...] = a
    v = 1.0 - a * a
    half_conv = 0.5 * conv
    u_s[...] = (v * lax.rsqrt(jnp.maximum(v, 1e-30))) * (half_conv * ti + half_conv)

    yield
    h = jnp.zeros((SUBLANES, width), F32)
    ap = jnp.ones((SUBLANES, width), F32)
    for j in range(seg):
        rows = slice(j * SUBLANES, (j + 1) * SUBLANES)
        aj = a_s[rows, :]
        h = aj * h + u_s[rows, :]
        ap = aj * ap
        h_s[rows, :] = h
        ap_s[rows, :] = ap

    yield
    sub = lax.broadcasted_iota(jnp.int32, (SUBLANES, width), 0)
    for sh in (1, 2, 4):
        keep = sub >= sh
        h = jnp.where(keep, ap * pltpu.roll(h, sh, 0) + h, h)
        ap = jnp.where(keep, ap * pltpu.roll(ap, sh, 0), ap)
    leave = h + ap * hc[0:1, :]
    enter = jnp.where(sub == 0, hc[0:1, :], pltpu.roll(leave, 1, 0))
    hc[0:1, :] = leave[SUBLANES - 1:SUBLANES, :]

    enter_t = jnp.concatenate([enter] * seg, axis=0)
    y = ((h_s[...] + ap_s[...] * enter_t) * jax.nn.gelu(ga)).astype(BF16)
    o_ref[...] = lax.dot_general(perm, y, TN_DIMS, preferred_element_type=F32).astype(o_ref.dtype)
    yield


def _rglru(proj, conv_w, conv_b, wg, b_r, b_i, lam, bsz, seq, ts=LRU_TILE, nb=LRU_SEQS_PER_STEP):
    width = conv_w.shape[1]
    nt = seq // ts
    vec = lambda: _resident((1, width))
    perm = jnp.asarray(_segment_permutation(ts), BF16)
    assert bsz % nb == 0
    proj3 = proj.reshape(bsz, seq, proj.shape[1])
    return pl.pallas_call(
        _rglru_kernel,
        grid=(bsz // nb, nt),
        in_specs=[pl.BlockSpec((nb, ts, width), lambda b, t: (b, t, 0)),
                  pl.BlockSpec((nb, ts, width), lambda b, t: (b, t, 1)),
                  _resident(perm.shape), _resident(conv_w.shape), vec(), _resident(wg.shape),
                  vec(), vec(), vec()],
        out_specs=pl.BlockSpec((nb, ts, width), lambda b, t: (b, t, 0)),
        out_shape=jax.ShapeDtypeStruct((bsz, seq, width), BF16),
        scratch_shapes=[pltpu.VMEM((nb,) + shape, F32) for shape in [
            (ts + (CONV_W - 1) * SUBLANES, width), (ts, width), (ts, width), (ts, width),
            (ts, width), (SUBLANES, width), (SUBLANES, width)]],
        compiler_params=_params(2), name="rglru",
    )(proj3, proj3, perm, conv_w, conv_b.reshape(1, width), wg, b_r.reshape(1, width),
      b_i.reshape(1, width), lam.reshape(1, width)).reshape(bsz * seq, width)


def _pair_level(n):
    t = lax.broadcasted_iota(jnp.int32, (n, n), 0)
    s = lax.broadcasted_iota(jnp.int32, (n, n), 1)
    x = t ^ s
    lvl = jnp.zeros((n, n), jnp.int32)
    w = 1
    while w < n:
        lvl = lvl + (x >= w).astype(jnp.int32)
        w *= 2
    return jnp.where(s > t, -1, lvl)


def _hgrn2_kernel(q_ref, f_ref, v_ref, g_ref, lbl_ref, ng_ref, o_ref, st, bc_s, f_s, *, layer):
    ts, dk = q_ref.shape[0], LANES
    c = CHUNK
    nck = ts // c
    heads = q_ref.shape[1] // dk

    @pl.when(pl.program_id(2) == 0)
    def _():
        st[...] = jnp.zeros_like(st)

    lg = lbl_ref[...]
    e = jnp.exp(lg - jnp.max(lg, axis=0, keepdims=True))
    lb = jnp.sum(e[0:layer + 1, :], axis=0, keepdims=True) / jnp.sum(e, axis=0, keepdims=True)

    row = lax.broadcasted_iota(jnp.int32, (c, dk), 0)
    tt = lax.broadcasted_iota(jnp.int32, (c, c), 0)
    ss = lax.broadcasted_iota(jnp.int32, (c, c), 1)
    tri = (ss <= tt).astype(BF16)
    lvl = _pair_level(c)
    n_lvl = c.bit_length()
    lvl_b = lvl.astype(F32).astype(BF16)
    right_b = [None] + [((row >> (p - 1)) & 1).astype(F32).astype(BF16) for p in range(1, n_lvl)]
    side = [None] + [jnp.where((row & (1 << (p - 1))) != 0, 1.0, -1.0) for p in range(1, n_lvl)]
    r4 = row & 3

    bodies = [(ci, hh) for ci in range(nck) for hh in range(heads)]
    for ci, hh in bodies:
        cols = slice(hh * dk, (hh + 1) * dk)
        lbh = lb[:, cols]
        f = lbh + (1.0 - lbh) * _sigmoid(f_ref[ci * c:(ci + 1) * c, cols])
        f_s[hh * nck + ci] = f
        bc_s[hh * nck + ci] = LOG2_E * sum(_dot(tri, part) for part in _split3(jnp.log(f)))

    def body(ci, hh):
        r0 = ci * c
        cols = slice(hh * dk, (hh + 1) * dk)
        qf = _silu(q_ref[r0:r0 + c, cols].astype(F32))
        f = f_s[hh * nck + ci]
        kk = 1.0 - f
        qb, kb = qf.astype(BF16), kk.astype(BF16)
        bc = bc_s[hh * nck + ci]
        vb = v_ref[r0:r0 + c, cols]

        scores = jnp.where(lvl_b == 0,
                           lax.dot_general(qb, kb, NT_DIMS,
                                           preferred_element_type=F32).astype(BF16), 0)
        for p in range(1, n_lvl):
            w = 1 << (p - 1)
            if w == 1:
                dec = jnp.where(right_b[p] > 0, f.astype(BF16), 1)
            elif w == 2:
                dec = jnp.where(r4 == 0, pltpu.roll(f, c - 1, 0),
                                jnp.where(r4 == 1, 1.0,
                                          jnp.where(r4 == 2, f, f * pltpu.roll(f, 1, 0)))
                                ).astype(BF16)
            else:
                gat = jnp.concatenate(
                    [jnp.broadcast_to(bc_s[hh * nck + ci, b * 2 * w + w - 1:b * 2 * w + w, :],
                                      (2 * w, dk)) for b in range(c // (2 * w))], axis=0)
                dec = jnp.exp2((bc - gat) * side[p]).astype(BF16)
            z = jnp.where(right_b[p] > 0, qb, kb) * dec
            scores = jnp.where(lvl_b == p,
                               lax.dot_general(z, z, NT_DIMS,
                                               preferred_element_type=F32).astype(BF16), scores)
            yield
        o = _dot(scores, vb)

        b_last = bc[c - 1:c, :]
        o = o + lax.dot_general((qf * jnp.exp2(bc)).astype(BF16), st[hh].astype(BF16), NT_DIMS,
                                preferred_element_type=F32)
        kd = (kk * jnp.exp2(b_last - bc)).astype(BF16)
        st[hh] = st[hh] * jnp.exp2(b_last) + lax.dot_general(vb, kd, TN_DIMS,
                                                             preferred_element_type=F32)

        o = o * lax.rsqrt(jnp.mean(o * o, axis=-1, keepdims=True) + NORM_EPS)
        gv = g_ref[r0:r0 + c, cols].astype(F32)
        o_ref[r0:r0 + c, cols] = (o * ng_ref[:, cols] * _silu(gv)).astype(o_ref.dtype)
        yield

    for ci in range(nck):
        for _ in zip(*[body(ci, hh) for hh in range(heads)]):
            pass


def _hgrn2(proj, fproj, lb_logits, norm_g, layer, bsz, seq, col0, ts=MIX_TILE,
           heads=HGRN2_HEADS_PER_STEP):
    dk = LANES
    wd = heads * dk
    width = fproj.shape[1]
    nt = seq // ts
    q0, v0, g0 = (o // heads for o in col0)
    row = lambda b, t: b * nt + t
    n_layers = lb_logits.shape[0]
    return pl.pallas_call(
        functools.partial(_hgrn2_kernel, layer=layer),
        grid=(bsz, B_HEADS // heads, nt),
        in_specs=[pl.BlockSpec((ts, wd), lambda b, h, t: (row(b, t), q0 + h)),
                  pl.BlockSpec((ts, wd), lambda b, h, t: (row(b, t), h)),
                  pl.BlockSpec((ts, wd), lambda b, h, t: (row(b, t), v0 + h)),
                  pl.BlockSpec((ts, wd), lambda b, h, t: (row(b, t), g0 + h)),
                  pl.BlockSpec((n_layers, wd), lambda b, h, t: (0, h)),
                  pl.BlockSpec((1, wd), lambda b, h, t: (0, h))],
        out_specs=pl.BlockSpec((ts, wd), lambda b, h, t: (row(b, t), h)),
        out_shape=jax.ShapeDtypeStruct((bsz * seq, width), BF16),
        scratch_shapes=[pltpu.VMEM((heads, dk, dk), F32),
                        pltpu.VMEM((heads * (ts // CHUNK), CHUNK, dk), F32),
                        pltpu.VMEM((heads * (ts // CHUNK), CHUNK, dk), F32)],
        compiler_params=_params(3), name="hgrn2",
    )(proj, fproj, proj, proj, lb_logits, norm_g.reshape(1, width))


def _ssd_expand_matrices(nck):
    e = np.zeros((nck, LANES, M_HPG * LANES + M_HPG * M_HEAD_DIM), np.float32)
    for c in range(nck):
        for r in range(M_HPG):
            e[c, SUBLANES * c + r, r * LANES:(r + 1) * LANES] = 1.0
            e[c, SUBLANES * (nck + c) + r,
              M_HPG * LANES + r * M_HEAD_DIM:M_HPG * LANES + (r + 1) * M_HEAD_DIM] = 1.0
    return e


def _ssd_kernel(*refs, groups):
    n_in = 16
    (z, x, b, c, dt, cwx, cbx, cwb, cbb, cwc, cbc, bias, alog, drow, ng, e_ref) = refs[:n_in]
    o_ref, scratch = refs[n_in], refs[n_in + 1:]
    gw, n = x.shape[1] // groups, b.shape[1] // groups
    row_forms = [_ssd_row_forms(dt.at[gi], bias.at[gi], alog.at[gi]) for gi in range(groups)]
    for gi in range(groups):
        wide = (slice(None), slice(gi * gw, (gi + 1) * gw))
        narrow = (slice(None), slice(gi * n, (gi + 1) * n))
        _ssd_group(z.at[wide], x.at[wide], b.at[narrow], c.at[narrow], row_forms[gi],
                   cwx.at[wide], cbx.at[wide], cwb.at[narrow], cbb.at[narrow], cwc.at[narrow],
                   cbc.at[narrow], drow.at[wide], ng.at[wide], e_ref, o_ref.at[wide],
                   *[sc.at[gi] for sc in scratch])


def _ssd_row_forms(dt_ref, bias_ref, alog_ref):
    nck, _, c = dt_ref.shape
    tt = lax.broadcasted_iota(jnp.int32, (c, c), 0)
    ss = lax.broadcasted_iota(jnp.int32, (c, c), 1)
    upper = (tt <= ss).astype(F32)
    bias = jnp.concatenate([bias_ref[...]] * nck, axis=0)
    neg_a = jnp.concatenate([-jnp.exp(alog_ref[...])] * nck, axis=0)
    dt = _softplus(dt_ref[...].reshape(nck * SUBLANES, c) + bias)
    a_row = LOG2_E * _dot_f32(dt * neg_a, upper)
    dtw = dt * jnp.exp2(a_row[:, c - 1:c] - a_row)
    src_row = a_row - LOG2_E * jnp.log(dt)
    rows = [a_row, dtw]
    if 2 * nck * SUBLANES < c:
        rows.append(jnp.zeros((c - 2 * nck * SUBLANES, c), F32))
    return _split3(jnp.concatenate(rows, axis=0).T) + (src_row,)


def _ssd_group(z_ref, x_ref, b_ref, c_ref, row_forms, cwx_ref, cbx_ref, cwb_ref, cbb_ref,
               cwc_ref, cbc_ref, drow_ref, ng_ref, e_ref, o_ref,
               xbuf, bbuf, cbuf, xc_s, bc_s, cc_s, lhs_s, xw_s, sb_s, dec_s, st):
    ts, gw = x_ref.shape
    n = D_STATE
    c = CHUNK
    nck = ts // c
    nb = M_HPG * LANES

    @pl.when(pl.program_id(2) == 0)
    def _():
        xbuf[0:SUBLANES, :] = jnp.zeros((SUBLANES, gw), F32)
        bbuf[0:SUBLANES, :] = jnp.zeros((SUBLANES, n), F32)
        cbuf[0:SUBLANES, :] = jnp.zeros((SUBLANES, n), F32)
        st[...] = jnp.zeros_like(st)

    _conv_silu(x_ref, xbuf, cwx_ref, cbx_ref, xc_s)
    _conv_silu(b_ref, bbuf, cwb_ref, cbb_ref, bc_s)
    _conv_silu(c_ref, cbuf, cwc_ref, cbc_ref, cc_s)

    tt = lax.broadcasted_iota(jnp.int32, (c, c), 0)
    ss = lax.broadcasted_iota(jnp.int32, (c, c), 1)
    causal = ss <= tt
    low_half = lax.broadcasted_iota(jnp.int32, (1, LANES), 1) < M_HEAD_DIM
    low_lanes = lax.broadcasted_iota(jnp.int32, (c + D_STATE, LANES), 1) < M_HEAD_DIM
    hi, mid, lo, src_row = row_forms

    for ci in range(nck):
        r0 = ci * c
        e_c = e_ref[ci]
        ex = _dot(hi, e_c)
        a_col = ex[:, :nb] + _dot(mid, e_c[:, :nb]) + _dot(lo, e_c[:, :nb])
        xw_s[r0:r0 + c, :] = (xc_s[r0:r0 + c, :] * ex[:, nb:]).astype(BF16)
        cm = cc_s[r0:r0 + c, :]
        cmf = cm.astype(F32)
        cb = lax.dot_general(cm, bc_s[r0:r0 + c, :], NT_DIMS, preferred_element_type=F32)
        for r in range(M_HPG):
            a_t = a_col[:, r * LANES:(r + 1) * LANES]
            src = src_row[ci * SUBLANES + r:ci * SUBLANES + r + 1, :]
            lmat = jnp.exp2(jnp.where(causal, a_t - src, -1e30))
            lhs_s[ci * M_HPG + r] = jnp.concatenate(
                [(cb * lmat).astype(BF16), (cmf * jnp.exp2(a_t)).astype(BF16)], axis=1)
        e_last = [jnp.exp2(a_col[c - 1:c, r * LANES:(r + 1) * LANES]) for r in range(M_HPG)]
        dec_s[ci:ci + 1, :] = jnp.concatenate([jnp.where(low_half, e_last[0], e_last[1]),
                                               jnp.where(low_half, e_last[2], e_last[3])], axis=1)

    for ci in range(nck):
        r0 = ci * c
        sb_s[ci] = st[...].astype(BF16)
        st[...] = st[...] * dec_s[ci:ci + 1, :] + lax.dot_general(
            bc_s[r0:r0 + c, :], xw_s[r0:r0 + c, :], TN_DIMS, preferred_element_type=F32)

    for ci in range(nck):
        r0 = ci * c
        xc = xc_s[r0:r0 + c, :]
        rhs = jnp.concatenate([xc.astype(BF16), sb_s[ci]], axis=0)
        tiles = []
        for j in range(gw // LANES):
            rt = rhs[:, j * LANES:(j + 1) * LANES]
            tiles.append(_dot(lhs_s[ci * M_HPG + 2 * j], jnp.where(low_lanes, rt, 0))
                         + _dot(lhs_s[ci * M_HPG + 2 * j + 1], jnp.where(low_lanes, 0, rt)))
        y = drow_ref[...] * xc + jnp.concatenate(tiles, axis=1)
        y = y * _silu(z_ref[r0:r0 + c, :].astype(F32))
        y = y * lax.rsqrt(jnp.mean(y * y, axis=-1, keepdims=True) + NORM_EPS)
        o_ref[r0:r0 + c, :] = (y * ng_ref[...]).astype(o_ref.dtype)


def _ssd(proj, dt_t, conv_w, conv_b, dt_bias, a_log, d_skip, norm_g, bsz, seq, ts=MIX_TILE,
         groups=SSD_GROUPS_PER_STEP):
    gw = M_HPG * M_HEAD_DIM
    n = D_STATE
    d_inner = M_GROUPS * gw
    nt = seq // ts
    nck = ts // CHUNK
    row = lambda b, t: b * nt + t
    gp = groups
    zb, xb = 0, d_inner // (gw * gp)
    bb, cb = 2 * d_inner // (n * gp), (2 * d_inner // n + M_GROUPS) // gp
    wbb, wcb = d_inner // (n * gp), (d_inner // n + M_GROUPS) // gp
    cw2 = conv_w.reshape(CONV_W, -1)
    cb2 = conv_b.reshape(1, -1)

    def per_head(v):
        v = jnp.pad(v.reshape(M_GROUPS, M_HPG), ((0, 0), (0, SUBLANES - M_HPG)))
        return jnp.broadcast_to(v[:, :, None], (M_GROUPS, SUBLANES, LANES)).astype(F32)

    drow = jnp.repeat(d_skip.astype(F32), M_HEAD_DIM).reshape(1, d_inner)
    e_mat = jnp.asarray(_ssd_expand_matrices(nck), BF16)
    g3 = lambda b, g, t: (g, 0, 0)
    return pl.pallas_call(
        functools.partial(_ssd_kernel, groups=gp),
        grid=(bsz, M_GROUPS // gp, nt),
        in_specs=[pl.BlockSpec((ts, gp * gw), lambda b, g, t: (row(b, t), zb + g)),
                  pl.BlockSpec((ts, gp * gw), lambda b, g, t: (row(b, t), xb + g)),
                  pl.BlockSpec((ts, gp * n), lambda b, g, t: (row(b, t), bb + g)),
                  pl.BlockSpec((ts, gp * n), lambda b, g, t: (row(b, t), cb + g)),
                  pl.BlockSpec((gp, nck, SUBLANES, LANES), lambda b, g, t: (g, row(b, t), 0, 0)),
                  pl.BlockSpec((CONV_W, gp * gw), lambda b, g, t: (0, g)),
                  pl.BlockSpec((1, gp * gw), lambda b, g, t: (0, g)),
                  pl.BlockSpec((CONV_W, gp * n), lambda b, g, t: (0, wbb + g)),
                  pl.BlockSpec((1, gp * n), lambda b, g, t: (0, wbb + g)),
                  pl.BlockSpec((CONV_W, gp * n), lambda b, g, t: (0, wcb + g)),
                  pl.BlockSpec((1, gp * n), lambda b, g, t: (0, wcb + g)),
                  pl.BlockSpec((gp, SUBLANES, LANES), g3),
                  pl.BlockSpec((gp, SUBLANES, LANES), g3),
                  pl.BlockSpec((1, gp * gw), lambda b, g, t: (0, g)),
                  pl.BlockSpec((1, gp * gw), lambda b, g, t: (0, g)),
                  _resident(e_mat.shape)],
        out_specs=pl.BlockSpec((ts, gp * gw), lambda b, g, t: (row(b, t), g)),
        out_shape=jax.ShapeDtypeStruct((bsz * seq, d_inner), BF16),
        scratch_shapes=[pltpu.VMEM((gp,) + shape, dt) for shape, dt in [
            ((ts + SUBLANES, gw), F32), ((ts + SUBLANES, n), F32), ((ts + SUBLANES, n), F32),
            ((ts, gw), F32), ((ts, n), BF16), ((ts, n), BF16),
            ((nck * M_HPG, CHUNK, CHUNK + n), BF16), ((ts, gw), BF16), ((nck, n, gw), BF16),
            ((max(nck, SUBLANES), gw), F32), ((n, gw), F32)]],
        compiler_params=_params(3), name="ssd",
    )(proj, proj, proj, proj, dt_t, cw2, cb2, cw2, cb2, cw2, cb2,
      per_head(dt_bias), per_head(a_log), drow, norm_g.reshape(1, d_inner), e_mat)


def _post_kernel(*refs, n_mix, final_norm, ff_chunk, sub_tiles):
    mix_refs = refs[:n_mix]
    wmix_refs = refs[n_mix:2 * n_mix]
    (x_ref, gq_ref, wq_ref, k_ref, v_ref, wo_ref, gf_ref, w1_ref, w2_ref, gfin_ref,
     o_ref) = refs[2 * n_mix:]
    tm, d = x_ref.shape
    hd = d // X_HEADS
    rs = tm // sub_tiles

    def rows_chain(rows):
        x = x_ref[rows, :]
        for m_ref, w_ref in zip(mix_refs, wmix_refs):
            x = x + _dot(m_ref[rows, :], w_ref[...])
        yield
        q = _dot(_rms(x, gq_ref[...]).astype(BF16), wq_ref[...]).astype(BF16)
        yield
        hs = range(X_HEADS)
        head = lambda ref, h: ref[0, :, h * hd:(h + 1) * hd]
        sc = [lax.dot_general(q[:, h * hd:(h + 1) * hd], head(k_ref, h), NT_DIMS,
                              preferred_element_type=F32) * (hd ** -0.5) for h in hs]
        p = [jnp.exp(sc[h] - jnp.max(sc[h], axis=-1, keepdims=True)) for h in hs]
        p = [p[h] / jnp.sum(p[h], axis=-1, keepdims=True) for h in hs]
        yield
        heads = [_dot(p[h].astype(BF16), head(v_ref, h)) for h in hs]
        x = x + _dot(jnp.concatenate(heads, axis=-1).astype(BF16), wo_ref[...])
        yield
        hn = _rms(x, gf_ref[...]).astype(BF16)
        chunks = list(range(0, w1_ref.shape[1], ff_chunk))
        up = lambda c: _dot(hn, w1_ref[:, c:c + ff_chunk])
        y = x
        pre = up(chunks[0])
        for i, c in enumerate(chunks):
            nxt = up(chunks[i + 1]) if i + 1 < len(chunks) else None
            hid = jnp.square(jnp.maximum(pre, 0.0))
            y = y + _dot(hid.astype(BF16), w2_ref[c:c + ff_chunk, :])
            pre = nxt
            yield
        if final_norm:
            y = _rms(y, gfin_ref[...])
        o_ref[rows, :] = y
        yield

    for _ in zip(*[rows_chain(slice(i * rs, (i + 1) * rs)) for i in range(sub_tiles)]):
        pass


def _post(mixes, wmix_specs, wmix_arrays, x2d, gq, k, v, gf, gfin, attn_ffn, layer, final_norm,
          bsz, seq, tm=TOKEN_TILE, ff_chunk=1024, sub_tiles=POST_SUB_TILES):
    t, d = x2d.shape
    nt = seq // tm
    mem_len = k.shape[1]
    wq, wo, w1, w2 = attn_ffn
    tile = lambda w: pl.BlockSpec((tm, w), lambda b, i: (b * nt + i, 0))
    mem = lambda: pl.BlockSpec((1, mem_len, d), lambda b, i: (b, 0, 0))
    in_specs = [tile(m.shape[1]) for m in mixes] + list(wmix_specs)
    in_specs += [tile(d), _resident((1, d)), _layer_resident(wq, layer), mem(), mem(),
                 _layer_resident(wo, layer), _resident((1, d)), _layer_resident(w1, layer),
                 _layer_resident(w2, layer), _resident((1, d))]
    return pl.pallas_call(
        functools.partial(_post_kernel, n_mix=len(mixes), final_norm=final_norm,
                          ff_chunk=ff_chunk, sub_tiles=sub_tiles),
        grid=(bsz, nt), in_specs=in_specs, out_specs=tile(d),
        out_shape=jax.ShapeDtypeStruct((t, d), F32),
        compiler_params=_params(2), name="post",
    )(*mixes, *wmix_arrays, x2d, gq.reshape(1, d), wq, k, v, wo, gf.reshape(1, d), w1, w2,
      gfin.reshape(1, d))


def kernel(x, mem, norm_mix_g, norm_mem_q_g, norm_mem_kv_g, norm_ffn_g, final_norm_g, e_w_in, a_conv_w, a_conv_b, a_gate_r_w, a_gate_r_b, a_gate_i_w, a_gate_i_b, a_lambda, b_lb_logits, b_norm_g, e_w_out, o_w_in, m_conv_w, m_conv_b, m_dt_bias, m_a_log, m_d, m_norm_g, o_w_out, xq_w, xk_w, xv_w, xo_w, ffn_w1, ffn_w2):
    bsz, seq, d = x.shape
    mem_len = mem.shape[1]
    depth = norm_mix_g.shape[0]
    t = bsz * seq
    bf = lambda w: w.astype(BF16)
    xs = x.reshape(t, d)
    mem2d = mem.reshape(bsz * mem_len, d)

    attn_ffn = (bf(xq_w), bf(xo_w), bf(ffn_w1), bf(ffn_w2))
    xk, xv = bf(xk_w), bf(xv_w)
    all_cols = lambda n: tuple(range(0, n, 1024))
    for l in range(depth):
        k, v = _inproj(mem2d, norm_mem_kv_g[l], [(xk, l), (xv, l)],
                       [("nn", 0, all_cols(d)), ("nn", 1, all_cols(d))], [BF16, BF16])
        k, v = k.reshape(bsz, mem_len, d), v.reshape(bsz, mem_len, d)
        if l % 2 == 0:
            e = l // 2
            wa = d
            proj, fproj = _inproj(xs, norm_mix_g[l], [(bf(e_w_in), e)],
                                  [("nn", 0, (0, wa, 2 * wa, 4 * wa, 5 * wa)), ("nn", 0, (3 * wa,))],
                                  [BF16, F32])
            wg = bf(0.5 * jnp.concatenate([a_gate_r_w[e], a_gate_i_w[e]], axis=-1))
            ya = _rglru(proj, a_conv_w[e], a_conv_b[e], wg, 0.5 * a_gate_r_b[e],
                        0.5 * a_gate_i_b[e], a_lambda[e], bsz, seq)
            blk = wa // LANES
            yb = _hgrn2(proj, fproj, b_lb_logits, b_norm_g[e], l, bsz, seq,
                        (2 * blk, 3 * blk, 4 * blk))
            mixes = [ya, yb]
            w_out = bf(e_w_out)
            wmix_specs = [_layer_resident(w_out, e, (0, wa)), _layer_resident(w_out, e, (1, wa))]
            wmix_arrays = [w_out, w_out]
        else:
            o = l // 2
            n_main = o_w_in.shape[-1] - M_GROUPS * M_HPG
            w_dt = o_w_in[o][:, n_main:].T.reshape(M_GROUPS, M_HPG, d)
            w_dt = jnp.pad(w_dt, ((0, 0), (0, SUBLANES - M_HPG), (0, 0))).reshape(1, -1, d)
            proj, dt_t = _inproj(xs, norm_mix_g[l], [(bf(o_w_in), o), (bf(w_dt), 0)],
                                 [("nn", 0, all_cols(n_main)), ("nt", 1, ())], [BF16, F32])
            ym = _ssd(proj, dt_t, m_conv_w[o], m_conv_b[o], m_dt_bias[o], m_a_log[o], m_d[o],
                      m_norm_g[o], bsz, seq)
            mixes = [ym]
            w_out = bf(o_w_out)
            wmix_specs = [_layer_resident(w_out, o)]
            wmix_arrays = [w_out]
        xs = _post(mixes, wmix_specs, wmix_arrays, xs, norm_mem_q_g[l], k, v, norm_ffn_g[l],
                   final_norm_g, attn_ffn, l, l == depth - 1, bsz, seq)
    return xs.reshape(bsz, seq, d)
```

```python
import functools

import jax
import jax.numpy as jnp
import numpy as np
from jax import lax
from jax.experimental import pallas as pl
from jax.experimental.pallas import tpu as pltpu

F32 = jnp.float32
BF16 = jnp.bfloat16

NORM_EPS = 1e-6
LOG2_E = 1.4426950408889634
CONV_W = 4
LANES = 128
SUBLANES = 8
VMEM_LIMIT_BYTES = 56 * 1024 * 1024

A_HEADS = 8
LRU_C = 8.0
B_HEADS = 8
M_HEAD_DIM = 64
M_GROUPS = 8
M_HPG = 4
D_STATE = 128
X_HEADS = 4
CHUNK = 128

TOKEN_TILE = 512
LRU_TILE = 256
LRU_SEQS_PER_STEP = 4
MIX_TILE = 1024
HGRN2_HEADS_PER_STEP = 8
SSD_GROUPS_PER_STEP = 4
POST_SUB_TILES = 2

NT_DIMS = (((1,), (1,)), ((), ()))
TN_DIMS = (((0,), (0,)), ((), ()))


def _params(n_axes):
    return pltpu.CompilerParams(
        dimension_semantics=("arbitrary",) * n_axes, vmem_limit_bytes=VMEM_LIMIT_BYTES)


def _resident(shape):
    nd = len(shape)
    return pl.BlockSpec(shape, lambda *_: (0,) * nd, pipeline_mode=pl.Buffered(1))


def _rms(x, g):
    return x * lax.rsqrt(jnp.mean(x * x, axis=-1, keepdims=True) + NORM_EPS) * g


def _sigmoid(x):
    return 0.5 * jnp.tanh(0.5 * x) + 0.5


def _silu(x):
    h = 0.5 * x
    return h * jnp.tanh(h) + h


def _softplus(x):
    return jnp.maximum(x, 0.0) + jnp.log1p(jnp.exp(-jnp.abs(x)))


def _dot(a, b):
    return jnp.dot(a, b, preferred_element_type=F32)


def _dot_f32(a, b):
    return jnp.dot(a, b, preferred_element_type=F32, precision=lax.Precision.HIGHEST)


def _split3(x):
    hi = x.astype(BF16)
    r = x - hi.astype(F32)
    mid = r.astype(BF16)
    lo = (r - mid.astype(F32)).astype(BF16)
    return hi, mid, lo


def _layer_resident(w, layer, rows=None):
    r0, nr = (0, w.shape[1]) if rows is None else rows
    return pl.BlockSpec((None, nr, w.shape[2]), lambda *_: (layer, r0, 0),
                        pipeline_mode=pl.Buffered(1))


def _inproj_kernel(x_ref, g_ref, *refs, n_w, outs, col_chunk):
    w_refs, o_refs = refs[:n_w], refs[n_w:]
    xn = _rms(x_ref[...], g_ref[...]).astype(BF16)
    for (kind, wi, cols), o_ref in zip(outs, o_refs):
        w_ref = w_refs[wi]
        if kind == "nn":
            for j, c in enumerate(cols):
                o_ref[:, j * col_chunk:(j + 1) * col_chunk] = _dot(
                    xn, w_ref[:, c:c + col_chunk]).astype(o_ref.dtype)
        else:
            y = lax.dot_general(w_ref[...], xn, NT_DIMS, preferred_element_type=F32)
            for gi in range(y.shape[0] // SUBLANES):
                for ci in range(y.shape[1] // LANES):
                    o_ref[gi, ci] = y[gi * SUBLANES:(gi + 1) * SUBLANES,
                                      ci * LANES:(ci + 1) * LANES].astype(o_ref.dtype)


def _inproj(x2d, g, weights, outs, out_dtypes, tm=TOKEN_TILE, col_chunk=1024):
    t, d = x2d.shape
    tm = min(tm, t)
    assert t % tm == 0
    in_specs = [pl.BlockSpec((tm, d), lambda i: (i, 0)), _resident((1, d))]
    in_specs += [_layer_resident(w, l) for w, l in weights]
    out_specs, out_shapes = [], []
    for (kind, wi, cols), dt in zip(outs, out_dtypes):
        if kind == "nn":
            n = len(cols) * col_chunk
            out_specs.append(pl.BlockSpec((tm, n), lambda i: (i, 0)))
            out_shapes.append(jax.ShapeDtypeStruct((t, n), dt))
        else:
            rb = weights[wi][0].shape[1] // SUBLANES
            out_specs.append(pl.BlockSpec((rb, tm // LANES, SUBLANES, LANES),
                                          lambda i: (0, i, 0, 0)))
            out_shapes.append(jax.ShapeDtypeStruct((rb, t // LANES, SUBLANES, LANES), dt))
    return pl.pallas_call(
        functools.partial(_inproj_kernel, n_w=len(weights), outs=tuple(outs),
                          col_chunk=col_chunk),
        grid=(t // tm,), in_specs=in_specs, out_specs=out_specs, out_shape=out_shapes,
        compiler_params=_params(1), name="inproj",
    )(x2d, g.reshape(1, d), *[w for w, _ in weights])


def _conv_silu(src_ref, buf, w_ref, b_ref, dst_ref, rows=CHUNK):
    ts = src_ref.shape[0]
    buf[SUBLANES:SUBLANES + ts, :] = src_ref[...].astype(F32)
    for r0 in range(0, ts, rows):
        y = b_ref[...] + w_ref[CONV_W - 1:CONV_W, :] * buf[SUBLANES + r0:SUBLANES + r0 + rows, :]
        for k in range(CONV_W - 1):
            off = SUBLANES - (CONV_W - 1) + k + r0
            y = y + w_ref[k:k + 1, :] * buf[off:off + rows, :]
        dst_ref[r0:r0 + rows, :] = _silu(y).astype(dst_ref.dtype)
    buf[0:SUBLANES, :] = buf[ts:ts + SUBLANES, :]


def _segment_permutation(ts):
    seg = ts // SUBLANES
    p = np.zeros((ts, ts), np.float32)
    for j in range(seg):
        for s in range(SUBLANES):
            p[SUBLANES * j + s, seg * s + j] = 1.0
    return p


def _segment_conv(xa, xe_s, tail, w, b):
    ts, width = xa.shape
    halo = (CONV_W - 1) * SUBLANES
    first_sublane = lax.broadcasted_iota(jnp.int32, (SUBLANES, width), 0) == 0
    for i in range(CONV_W - 1):
        blk = xa[ts - halo + i * SUBLANES:ts - halo + (i + 1) * SUBLANES, :]
        xe_s[i * SUBLANES:(i + 1) * SUBLANES, :] = jnp.where(
            first_sublane, tail[i:i + 1, :], pltpu.roll(blk, 1, 0))
        tail[i:i + 1, :] = blk[SUBLANES - 1:SUBLANES, :]
    xe_s[halo:halo + ts, :] = xa
    conv = b[...] + w[CONV_W - 1:CONV_W, :] * xa
    for k in range(CONV_W - 1):
        conv = conv + w[k:k + 1, :] * xe_s[k * SUBLANES:k * SUBLANES + ts, :]
    return conv


def _rglru_kernel(xa_ref, ga_ref, *refs):
    n_par = 7
    params, o_ref, scratch = refs[:n_par], refs[n_par], refs[n_par + 1:]
    tiles = [_rglru_tile(xa_ref.at[bi], ga_ref.at[bi], *params, o_ref.at[bi],
                         *[sc.at[bi] for sc in scratch]) for bi in range(xa_ref.shape[0])]
    for _ in zip(*tiles):
        pass


def _rglru_tile(xa_ref, ga_ref, perm_ref, cw_ref, cb_ref, wg_ref, br_ref, bi_ref, lam_ref, o_ref,
                xe_s, a_s, u_s, h_s, ap_s, tail, hc):
    ts, width = xa_ref.shape
    hd = width // A_HEADS
    seg = ts // SUBLANES

    @pl.when(pl.program_id(1) == 0)
    def _():
        tail[...] = jnp.zeros_like(tail)
        hc[...] = jnp.zeros_like(hc)

    perm = perm_ref[...]
    xa = _dot(perm, xa_ref[...])
    ga = _dot(perm, ga_ref[...])
    conv = _segment_conv(xa, xe_s, tail, cw_ref, cb_ref)

    yield
    convb = conv.astype(BF16)
    r_pre, i_pre = [], []
    for h in range(A_HEADS):
        gate = _dot(convb[:, h * hd:(h + 1) * hd], wg_ref[h])
        r_pre.append(gate[:, :hd])
        i_pre.append(gate[:, hd:])
    tr = jnp.tanh(jnp.concatenate(r_pre, axis=-1) + br_ref[...])
    ti = jnp.tanh(jnp.concatenate(i_pre, axis=-1) + bi_ref[...])
    half_c = (-0.5 * LRU_C) * _softplus(-lam_ref[...])
    a = jnp.exp(half_c * tr + half_c)
    a_s[...] = a
    v = 1.0 - a * a
    half_conv = 0.5 * conv
    u_s[...] = (v * lax.rsqrt(jnp.maximum(v, 1e-30))) * (half_conv * ti + half_conv)

    yield
    h = jnp.zeros((SUBLANES, width), F32)
    ap = jnp.ones((SUBLANES, width), F32)
    for j in range(seg):
        rows = slice(j * SUBLANES, (j + 1) * SUBLANES)
        aj = a_s[rows, :]
        h = aj * h + u_s[rows, :]
        ap = aj * ap
        h_s[rows, :] = h
        ap_s[rows, :] = ap

    yield
    sub = lax.broadcasted_iota(jnp.int32, (SUBLANES, width), 0)
    for sh in (1, 2, 4):
        keep = sub >= sh
        h = jnp.where(keep, ap * pltpu.roll(h, sh, 0) + h, h)
        ap = jnp.where(keep, ap * pltpu.roll(ap, sh, 0), ap)
    leave = h + ap * hc[0:1, :]
    enter = jnp.where(sub == 0, hc[0:1, :], pltpu.roll(leave, 1, 0))
    hc[0:1, :] = leave[SUBLANES - 1:SUBLANES, :]

    enter_t = jnp.concatenate([enter] * seg, axis=0)
    y = ((h_s[...] + ap_s[...] * enter_t) * jax.nn.gelu(ga)).astype(BF16)
    o_ref[...] = lax.dot_general(perm, y, TN_DIMS, preferred_element_type=F32).astype(o_ref.dtype)
    yield


def _rglru(proj, conv_w, conv_b, wg, b_r, b_i, lam, bsz, seq, ts=LRU_TILE, nb=LRU_SEQS_PER_STEP):
    width = conv_w.shape[1]
    nt = seq // ts
    vec = lambda: _resident((1, width))
    perm = jnp.asarray(_segment_permutation(ts), BF16)
    assert bsz % nb == 0
    proj3 = proj.reshape(bsz, seq, proj.shape[1])
    return pl.pallas_call(
        _rglru_kernel,
        grid=(bsz // nb, nt),
        in_specs=[pl.BlockSpec((nb, ts, width), lambda b, t: (b, t, 0)),
                  pl.BlockSpec((nb, ts, width), lambda b, t: (b, t, 1)),
                  _resident(perm.shape), _resident(conv_w.shape), vec(), _resident(wg.shape),
                  vec(), vec(), vec()],
        out_specs=pl.BlockSpec((nb, ts, width), lambda b, t: (b, t, 0)),
        out_shape=jax.ShapeDtypeStruct((bsz, seq, width), BF16),
        scratch_shapes=[pltpu.VMEM((nb,) + shape, F32) for shape in [
            (ts + (CONV_W - 1) * SUBLANES, width), (ts, width), (ts, width), (ts, width),
            (ts, width), (SUBLANES, width), (SUBLANES, width)]],
        compiler_params=_params(2), name="rglru",
    )(proj3, proj3, perm, conv_w, conv_b.reshape(1, width), wg, b_r.reshape(1, width),
      b_i.reshape(1, width), lam.reshape(1, width)).reshape(bsz * seq, width)


def _pair_level(n):
    t = lax.broadcasted_iota(jnp.int32, (n, n), 0)
    s = lax.broadcasted_iota(jnp.int32, (n, n), 1)
    x = t ^ s
    lvl = jnp.zeros((n, n), jnp.int32)
    w = 1
    while w < n:
        lvl = lvl + (x >= w).astype(jnp.int32)
        w *= 2
    return jnp.where(s > t, -1, lvl)


def _hgrn2_kernel(q_ref, f_ref, v_ref, g_ref, lbl_ref, ng_ref, o_ref, st, bc_s, f_s, *, layer):
    ts, dk = q_ref.shape[0], LANES
    c = CHUNK
    nck = ts // c
    heads = q_ref.shape[1] // dk

    @pl.when(pl.program_id(2) == 0)
    def _():
        st[...] = jnp.zeros_like(st)

    lg = lbl_ref[...]
    e = jnp.exp(lg - jnp.max(lg, axis=0, keepdims=True))
    lb = jnp.sum(e[0:layer + 1, :], axis=0, keepdims=True) / jnp.sum(e, axis=0, keepdims=True)

    row = lax.broadcasted_iota(jnp.int32, (c, dk), 0)
    tt = lax.broadcasted_iota(jnp.int32, (c, c), 0)
    ss = lax.broadcasted_iota(jnp.int32, (c, c), 1)
    tri = (ss <= tt).astype(BF16)
    lvl = _pair_level(c)
    n_lvl = c.bit_length()
    lvl_b = lvl.astype(F32).astype(BF16)
    right_b = [None] + [((row >> (p - 1)) & 1).astype(F32).astype(BF16) for p in range(1, n_lvl)]
    side = [None] + [jnp.where((row & (1 << (p - 1))) != 0, 1.0, -1.0) for p in range(1, n_lvl)]
    r4 = row & 3

    bodies = [(ci, hh) for ci in range(nck) for hh in range(heads)]
    for ci, hh in bodies:
        cols = slice(hh * dk, (hh + 1) * dk)
        lbh = lb[:, cols]
        f = lbh + (1.0 - lbh) * _sigmoid(f_ref[ci * c:(ci + 1) * c, cols])
        f_s[hh * nck + ci] = f
        bc_s[hh * nck + ci] = LOG2_E * sum(_dot(tri, part) for part in _split3(jnp.log(f)))

    def body(ci, hh):
        r0 = ci * c
        cols = slice(hh * dk, (hh + 1) * dk)
        qf = _silu(q_ref[r0:r0 + c, cols].astype(F32))
        f = f_s[hh * nck + ci]
        kk = 1.0 - f
        qb, kb = qf.astype(BF16), kk.astype(BF16)
        bc = bc_s[hh * nck + ci]
        vb = v_ref[r0:r0 + c, cols]

        scores = jnp.where(lvl_b == 0,
                           lax.dot_general(qb, kb, NT_DIMS,
                                           preferred_element_type=F32).astype(BF16), 0)
        for p in range(1, n_lvl):
            w = 1 << (p - 1)
            if w == 1:
                dec = jnp.where(right_b[p] > 0, f.astype(BF16), 1)
            elif w == 2:
                dec = jnp.where(r4 == 0, pltpu.roll(f, c - 1, 0),
                                jnp.where(r4 == 1, 1.0,
                                          jnp.where(r4 == 2, f, f * pltpu.roll(f, 1, 0)))
                                ).astype(BF16)
            else:
                gat = jnp.concatenate(
                    [jnp.broadcast_to(bc_s[hh * nck + ci, b * 2 * w + w - 1:b * 2 * w + w, :],
                                      (2 * w, dk)) for b in range(c // (2 * w))], axis=0)
                dec = jnp.exp2((bc - gat) * side[p]).astype(BF16)
            z = jnp.where(right_b[p] > 0, qb, kb) * dec
            scores = jnp.where(lvl_b == p,
                               lax.dot_general(z, z, NT_DIMS,
                                               preferred_element_type=F32).astype(BF16), scores)
            yield
        o = _dot(scores, vb)

        b_last = bc[c - 1:c, :]
        o = o + lax.dot_general((qf * jnp.exp2(bc)).astype(BF16), st[hh].astype(BF16), NT_DIMS,
                                preferred_element_type=F32)
        kd = (kk * jnp.exp2(b_last - bc)).astype(BF16)
        st[hh] = st[hh] * jnp.exp2(b_last) + lax.dot_general(vb, kd, TN_DIMS,
                                                             preferred_element_type=F32)

        o = o * lax.rsqrt(jnp.mean(o * o, axis=-1, keepdims=True) + NORM_EPS)
        gv = g_ref[r0:r0 + c, cols].astype(F32)
        o_ref[r0:r0 + c, cols] = (o * ng_ref[:, cols] * _silu(gv)).astype(o_ref.dtype)
        yield

    for ci in range(nck):
        for _ in zip(*[body(ci, hh) for hh in range(heads)]):
            pass


def _hgrn2(proj, fproj, lb_logits, norm_g, layer, bsz, seq, col0, ts=MIX_TILE,
           heads=HGRN2_HEADS_PER_STEP):
    dk = LANES
    wd = heads * dk
    width = fproj.shape[1]
    nt = seq // ts
    q0, v0, g0 = (o // heads for o in col0)
    row = lambda b, t: b * nt + t
    n_layers = lb_logits.shape[0]
    return pl.pallas_call(
        functools.partial(_hgrn2_kernel, layer=layer),
        grid=(bsz, B_HEADS // heads, nt),
        in_specs=[pl.BlockSpec((ts, wd), lambda b, h, t: (row(b, t), q0 + h)),
                  pl.BlockSpec((ts, wd), lambda b, h, t: (row(b, t), h)),
                  pl.BlockSpec((ts, wd), lambda b, h, t: (row(b, t), v0 + h)),
                  pl.BlockSpec((ts, wd), lambda b, h, t: (row(b, t), g0 + h)),
                  pl.BlockSpec((n_layers, wd), lambda b, h, t: (0, h)),
                  pl.BlockSpec((1, wd), lambda b, h, t: (0, h))],
        out_specs=pl.BlockSpec((ts, wd), lambda b, h, t: (row(b, t), h)),
        out_shape=jax.ShapeDtypeStruct((bsz * seq, width), BF16),
        scratch_shapes=[pltpu.VMEM((heads, dk, dk), F32),
                        pltpu.VMEM((heads * (ts // CHUNK), CHUNK, dk), F32),
                        pltpu.VMEM((heads * (ts // CHUNK), CHUNK, dk), F32)],
        compiler_params=_params(3), name="hgrn2",
    )(proj, fproj, proj, proj, lb_logits, norm_g.reshape(1, width))


def _ssd_expand_matrices(nck):
    e = np.zeros((nck, LANES, M_HPG * LANES + M_HPG * M_HEAD_DIM), np.float32)
    for c in range(nck):
        for r in range(M_HPG):
            e[c, SUBLANES * c + r, r * LANES:(r + 1) * LANES] = 1.0
            e[c, SUBLANES * (nck + c) + r,
              M_HPG * LANES + r * M_HEAD_DIM:M_HPG * LANES + (r + 1) * M_HEAD_DIM] = 1.0
    return e


def _ssd_kernel(*refs, groups):
    n_in = 16
    (z, x, b, c, dt, cwx, cbx, cwb, cbb, cwc, cbc, bias, alog, drow, ng, e_ref) = refs[:n_in]
    o_ref, scratch = refs[n_in], refs[n_in + 1:]
    gw, n = x.shape[1] // groups, b.shape[1] // groups
    row_forms = [_ssd_row_forms(dt.at[gi], bias.at[gi], alog.at[gi]) for gi in range(groups)]
    for gi in range(groups):
        wide = (slice(None), slice(gi * gw, (gi + 1) * gw))
        narrow = (slice(None), slice(gi * n, (gi + 1) * n))
        _ssd_group(z.at[wide], x.at[wide], b.at[narrow], c.at[narrow], row_forms[gi],
                   cwx.at[wide], cbx.at[wide], cwb.at[narrow], cbb.at[narrow], cwc.at[narrow],
                   cbc.at[narrow], drow.at[wide], ng.at[wide], e_ref, o_ref.at[wide],
                   *[sc.at[gi] for sc in scratch])


def _ssd_row_forms(dt_ref, bias_ref, alog_ref):
    nck, _, c = dt_ref.shape
    tt = lax.broadcasted_iota(jnp.int32, (c, c), 0)
    ss = lax.broadcasted_iota(jnp.int32, (c, c), 1)
    upper = (tt <= ss).astype(F32)
    bias = jnp.concatenate([bias_ref[...]] * nck, axis=0)
    neg_a = jnp.concatenate([-jnp.exp(alog_ref[...])] * nck, axis=0)
    dt = _softplus(dt_ref[...].reshape(nck * SUBLANES, c) + bias)
    a_row = LOG2_E * _dot_f32(dt * neg_a, upper)
    dtw = dt * jnp.exp2(a_row[:, c - 1:c] - a_row)
    src_row = a_row - LOG2_E * jnp.log(dt)
    rows = [a_row, dtw]
    if 2 * nck * SUBLANES < c:
        rows.append(jnp.zeros((c - 2 * nck * SUBLANES, c), F32))
    return _split3(jnp.concatenate(rows, axis=0).T) + (src_row,)


def _ssd_group(z_ref, x_ref, b_ref, c_ref, row_forms, cwx_ref, cbx_ref, cwb_ref, cbb_ref,
               cwc_ref, cbc_ref, drow_ref, ng_ref, e_ref, o_ref,
               xbuf, bbuf, cbuf, xc_s, bc_s, cc_s, lhs_s, xw_s, sb_s, dec_s, st):
    ts, gw = x_ref.shape
    n = D_STATE
    c = CHUNK
    nck = ts // c
    nb = M_HPG * LANES

    @pl.when(pl.program_id(2) == 0)
    def _():
        xbuf[0:SUBLANES, :] = jnp.zeros((SUBLANES, gw), F32)
        bbuf[0:SUBLANES, :] = jnp.zeros((SUBLANES, n), F32)
        cbuf[0:SUBLANES, :] = jnp.zeros((SUBLANES, n), F32)
        st[...] = jnp.zeros_like(st)

    _conv_silu(x_ref, xbuf, cwx_ref, cbx_ref, xc_s)
    _conv_silu(b_ref, bbuf, cwb_ref, cbb_ref, bc_s)
    _conv_silu(c_ref, cbuf, cwc_ref, cbc_ref, cc_s)

    tt = lax.broadcasted_iota(jnp.int32, (c, c), 0)
    ss = lax.broadcasted_iota(jnp.int32, (c, c), 1)
    causal = ss <= tt
    low_half = lax.broadcasted_iota(jnp.int32, (1, LANES), 1) < M_HEAD_DIM
    low_lanes = lax.broadcasted_iota(jnp.int32, (c + D_STATE, LANES), 1) < M_HEAD_DIM
    hi, mid, lo, src_row = row_forms

    for ci in range(nck):
        r0 = ci * c
        e_c = e_ref[ci]
        ex = _dot(hi, e_c)
        a_col = ex[:, :nb] + _dot(mid, e_c[:, :nb]) + _dot(lo, e_c[:, :nb])
        xw_s[r0:r0 + c, :] = (xc_s[r0:r0 + c, :] * ex[:, nb:]).astype(BF16)
        cm = cc_s[r0:r0 + c, :]
        cmf = cm.astype(F32)
        cb = lax.dot_general(cm, bc_s[r0:r0 + c, :], NT_DIMS, preferred_element_type=F32)
        for r in range(M_HPG):
            a_t = a_col[:, r * LANES:(r + 1) * LANES]
            src = src_row[ci * SUBLANES + r:ci * SUBLANES + r + 1, :]
            lmat = jnp.exp2(jnp.where(causal, a_t - src, -1e30))
            lhs_s[ci * M_HPG + r] = jnp.concatenate(
                [(cb * lmat).astype(BF16), (cmf * jnp.exp2(a_t)).astype(BF16)], axis=1)
        e_last = [jnp.exp2(a_col[c - 1:c, r * LANES:(r + 1) * LANES]) for r in range(M_HPG)]
        dec_s[ci:ci + 1, :] = jnp.concatenate([jnp.where(low_half, e_last[0], e_last[1]),
                                               jnp.where(low_half, e_last[2], e_last[3])], axis=1)

    for ci in range(nck):
        r0 = ci * c
        sb_s[ci] = st[...].astype(BF16)
        st[...] = st[...] * dec_s[ci:ci + 1, :] + lax.dot_general(
            bc_s[r0:r0 + c, :], xw_s[r0:r0 + c, :], TN_DIMS, preferred_element_type=F32)

    for ci in range(nck):
        r0 = ci * c
        xc = xc_s[r0:r0 + c, :]
        rhs = jnp.concatenate([xc.astype(BF16), sb_s[ci]], axis=0)
        tiles = []
        for j in range(gw // LANES):
            rt = rhs[:, j * LANES:(j + 1) * LANES]
            tiles.append(_dot(lhs_s[ci * M_HPG + 2 * j], jnp.where(low_lanes, rt, 0))
                         + _dot(lhs_s[ci * M_HPG + 2 * j + 1], jnp.where(low_lanes, 0, rt)))
        y = drow_ref[...] * xc + jnp.concatenate(tiles, axis=1)
        y = y * _silu(z_ref[r0:r0 + c, :].astype(F32))
        y = y * lax.rsqrt(jnp.mean(y * y, axis=-1, keepdims=True) + NORM_EPS)
        o_ref[r0:r0 + c, :] = (y * ng_ref[...]).astype(o_ref.dtype)


def _ssd(proj, dt_t, conv_w, conv_b, dt_bias, a_log, d_skip, norm_g, bsz, seq, ts=MIX_TILE,
         groups=SSD_GROUPS_PER_STEP):
    gw = M_HPG * M_HEAD_DIM
    n = D_STATE
    d_inner = M_GROUPS * gw
    nt = seq // ts
    nck = ts // CHUNK
    row = lambda b, t: b * nt + t
    gp = groups
    zb, xb = 0, d_inner // (gw * gp)
    bb, cb = 2 * d_inner // (n * gp), (2 * d_inner // n + M_GROUPS) // gp
    wbb, wcb = d_inner // (n * gp), (d_inner // n + M_GROUPS) // gp
    cw2 = conv_w.reshape(CONV_W, -1)
    cb2 = conv_b.reshape(1, -1)

    def per_head(v):
        v = jnp.pad(v.reshape(M_GROUPS, M_HPG), ((0, 0), (0, SUBLANES - M_HPG)))
        return jnp.broadcast_to(v[:, :, None], (M_GROUPS, SUBLANES, LANES)).astype(F32)

    drow = jnp.repeat(d_skip.astype(F32), M_HEAD_DIM).reshape(1, d_inner)
    e_mat = jnp.asarray(_ssd_expand_matrices(nck), BF16)
    g3 = lambda b, g, t: (g, 0, 0)
    return pl.pallas_call(
        functools.partial(_ssd_kernel, groups=gp),
        grid=(bsz, M_GROUPS // gp, nt),
        in_specs=[pl.BlockSpec((ts, gp * gw), lambda b, g, t: (row(b, t), zb + g)),
                  pl.BlockSpec((ts, gp * gw), lambda b, g, t: (row(b, t), xb + g)),
                  pl.BlockSpec((ts, gp * n), lambda b, g, t: (row(b, t), bb + g)),
                  pl.BlockSpec((ts, gp * n), lambda b, g, t: (row(b, t), cb + g)),
                  pl.BlockSpec((gp, nck, SUBLANES, LANES), lambda b, g, t: (g, row(b, t), 0, 0)),
                  pl.BlockSpec((CONV_W, gp * gw), lambda b, g, t: (0, g)),
                  pl.BlockSpec((1, gp * gw), lambda b, g, t: (0, g)),
                  pl.BlockSpec((CONV_W, gp * n), lambda b, g, t: (0, wbb + g)),
                  pl.BlockSpec((1, gp * n), lambda b, g, t: (0, wbb + g)),
                  pl.BlockSpec((CONV_W, gp * n), lambda b, g, t: (0, wcb + g)),
                  pl.BlockSpec((1, gp * n), lambda b, g, t: (0, wcb + g)),
                  pl.BlockSpec((gp, SUBLANES, LANES), g3),
                  pl.BlockSpec((gp, SUBLANES, LANES), g3),
                  pl.BlockSpec((1, gp * gw), lambda b, g, t: (0, g)),
                  pl.BlockSpec((1, gp * gw), lambda b, g, t: (0, g)),
                  _resident(e_mat.shape)],
        out_specs=pl.BlockSpec((ts, gp * gw), lambda b, g, t: (row(b, t), g)),
        out_shape=jax.ShapeDtypeStruct((bsz * seq, d_inner), BF16),
        scratch_shapes=[pltpu.VMEM((gp,) + shape, dt) for shape, dt in [
            ((ts + SUBLANES, gw), F32), ((ts + SUBLANES, n), F32), ((ts + SUBLANES, n), F32),
            ((ts, gw), F32), ((ts, n), BF16), ((ts, n), BF16),
            ((nck * M_HPG, CHUNK, CHUNK + n), BF16), ((ts, gw), BF16), ((nck, n, gw), BF16),
            ((max(nck, SUBLANES), gw), F32), ((n, gw), F32)]],
        compiler_params=_params(3), name="ssd",
    )(proj, proj, proj, proj, dt_t, cw2, cb2, cw2, cb2, cw2, cb2,
      per_head(dt_bias), per_head(a_log), drow, norm_g.reshape(1, d_inner), e_mat)


def _post_kernel(*refs, n_mix, final_norm, ff_chunk, sub_tiles):
    mix_refs = refs[:n_mix]
    wmix_refs = refs[n_mix:2 * n_mix]
    (x_ref, gq_ref, wq_ref, k_ref, v_ref, wo_ref, gf_ref, w1_ref, w2_ref, gfin_ref,
     o_ref) = refs[2 * n_mix:]
    tm, d = x_ref.shape
    hd = d // X_HEADS
    rs = tm // sub_tiles

    def rows_chain(rows):
        x = x_ref[rows, :]
        for m_ref, w_ref in zip(mix_refs, wmix_refs):
            x = x + _dot(m_ref[rows, :], w_ref[...])
        yield
        q = _dot(_rms(x, gq_ref[...]).astype(BF16), wq_ref[...]).astype(BF16)
        yield
        hs = range(X_HEADS)
        head = lambda ref, h: ref[0, :, h * hd:(h + 1) * hd]
        sc = [lax.dot_general(q[:, h * hd:(h + 1) * hd], head(k_ref, h), NT_DIMS,
                              preferred_element_type=F32) * (hd ** -0.5) for h in hs]
        p = [jnp.exp(sc[h] - jnp.max(sc[h], axis=-1, keepdims=True)) for h in hs]
        p = [p[h] / jnp.sum(p[h], axis=-1, keepdims=True) for h in hs]
        yield
        heads = [_dot(p[h].astype(BF16), head(v_ref, h)) for h in hs]
        x = x + _dot(jnp.concatenate(heads, axis=-1).astype(BF16), wo_ref[...])
        yield
        hn = _rms(x, gf_ref[...]).astype(BF16)
        chunks = list(range(0, w1_ref.shape[1], ff_chunk))
        up = lambda c: _dot(hn, w1_ref[:, c:c + ff_chunk])
        y = x
        pre = up(chunks[0])
        for i, c in enumerate(chunks):
            nxt = up(chunks[i + 1]) if i + 1 < len(chunks) else None
            hid = jnp.square(jnp.maximum(pre, 0.0))
            y = y + _dot(hid.astype(BF16), w2_ref[c:c + ff_chunk, :])
            pre = nxt
            yield
        if final_norm:
            y = _rms(y, gfin_ref[...])
        o_ref[rows, :] = y
        yield

    for _ in zip(*[rows_chain(slice(i * rs, (i + 1) * rs)) for i in range(sub_tiles)]):
        pass


def _post(mixes, wmix_specs, wmix_arrays, x2d, gq, k, v, gf, gfin, attn_ffn, layer, final_norm,
          bsz, seq, tm=TOKEN_TILE, ff_chunk=1024, sub_tiles=POST_SUB_TILES):
    t, d = x2d.shape
    nt = seq // tm
    mem_len = k.shape[1]
    wq, wo, w1, w2 = attn_ffn
    tile = lambda w: pl.BlockSpec((tm, w), lambda b, i: (b * nt + i, 0))
    mem = lambda: pl.BlockSpec((1, mem_len, d), lambda b, i: (b, 0, 0))
    in_specs = [tile(m.shape[1]) for m in mixes] + list(wmix_specs)
    in_specs += [tile(d), _resident((1, d)), _layer_resident(wq, layer), mem(), mem(),
                 _layer_resident(wo, layer), _resident((1, d)), _layer_resident(w1, layer),
                 _layer_resident(w2, layer), _resident((1, d))]
    return pl.pallas_call(
        functools.partial(_post_kernel, n_mix=len(mixes), final_norm=final_norm,
                          ff_chunk=ff_chunk, sub_tiles=sub_tiles),
        grid=(bsz, nt), in_specs=in_specs, out_specs=tile(d),
        out_shape=jax.ShapeDtypeStruct((t, d), F32),
        compiler_params=_params(2), name="post",
    )(*mixes, *wmix_arrays, x2d, gq.reshape(1, d), wq, k, v, wo, gf.reshape(1, d), w1, w2,
      gfin.reshape(1, d))


def kernel(x, mem, norm_mix_g, norm_mem_q_g, norm_mem_kv_g, norm_ffn_g, final_norm_g, e_w_in, a_conv_w, a_conv_b, a_gate_r_w, a_gate_r_b, a_gate_i_w, a_gate_i_b, a_lambda, b_lb_logits, b_norm_g, e_w_out, o_w_in, m_conv_w, m_conv_b, m_dt_bias, m_a_log, m_d, m_norm_g, o_w_out, xq_w, xk_w, xv_w, xo_w, ffn_w1, ffn_w2):
    bsz, seq, d = x.shape
    mem_len = mem.shape[1]
    depth = norm_mix_g.shape[0]
    t = bsz * seq
    bf = lambda w: w.astype(BF16)
    xs = x.reshape(t, d)
    mem2d = mem.reshape(bsz * mem_len, d)

    attn_ffn = (bf(xq_w), bf(xo_w), bf(ffn_w1), bf(ffn_w2))
    xk, xv = bf(xk_w), bf(xv_w)
    all_cols = lambda n: tuple(range(0, n, 1024))
    for l in range(depth):
        k, v = _inproj(mem2d, norm_mem_kv_g[l], [(xk, l), (xv, l)],
                       [("nn", 0, all_cols(d)), ("nn", 1, all_cols(d))], [BF16, BF16])
        k, v = k.reshape(bsz, mem_len, d), v.reshape(bsz, mem_len, d)
        if l % 2 == 0:
            e = l // 2
            wa = d
            proj, fproj = _inproj(xs, norm_mix_g[l], [(bf(e_w_in), e)],
                                  [("nn", 0, (0, wa, 2 * wa, 4 * wa, 5 * wa)), ("nn", 0, (3 * wa,))],
                                  [BF16, F32])
            wg = bf(0.5 * jnp.concatenate([a_gate_r_w[e], a_gate_i_w[e]], axis=-1))
            ya = _rglru(proj, a_conv_w[e], a_conv_b[e], wg, 0.5 * a_gate_r_b[e],
                        0.5 * a_gate_i_b[e], a_lambda[e], bsz, seq)
            blk = wa // LANES
            yb = _hgrn2(proj, fproj, b_lb_logits, b_norm_g[e], l, bsz, seq,
                        (2 * blk, 3 * blk, 4 * blk))
            mixes = [ya, yb]
            w_out = bf(e_w_out)
            wmix_specs = [_layer_resident(w_out, e, (0, wa)), _layer_resident(w_out, e, (1, wa))]
            wmix_arrays = [w_out, w_out]
        else:
            o = l // 2
            n_main = o_w_in.shape[-1] - M_GROUPS * M_HPG
            w_dt = o_w_in[o][:, n_main:].T.reshape(M_GROUPS, M_HPG, d)
            w_dt = jnp.pad(w_dt, ((0, 0), (0, SUBLANES - M_HPG), (0, 0))).reshape(1, -1, d)
            proj, dt_t = _inproj(xs, norm_mix_g[l], [(bf(o_w_in), o), (bf(w_dt), 0)],
                                 [("nn", 0, all_cols(n_main)), ("nt", 1, ())], [BF16, F32])
            ym = _ssd(proj, dt_t, m_conv_w[o], m_conv_b[o], m_dt_bias[o], m_a_log[o], m_d[o],
                      m_norm_g[o], bsz, seq)
            mixes = [ym]
            w_out = bf(o_w_out)
            wmix_specs = [_layer_resident(w_out, o)]
            wmix_arrays = [w_out]
        xs = _post(mixes, wmix_specs, wmix_arrays, xs, norm_mem_q_g[l], k, v, norm_ffn_g[l],
                   final_norm_g, attn_ffn, l, l == depth - 1, bsz, seq)
    return xs.reshape(bsz, seq, d)
```
